```python
import math
import jax, jax.numpy as jnp
from jax import lax
import numpy as np

D_MODEL = 2048
BATCH = 4
SEQ = 2048
DEPTH = 2
DEC_BATCH = 128
DEC_SEQ = 1
PAST_LEN = 16384
PAGE_SIZE = 128

S5_WIDTH = D_MODEL // 2
S5_GROUP = 16
S5_GROUPS = S5_WIDTH // S5_GROUP
S5_STATE = 64
HG_WIDTH = D_MODEL // 2
HG_HEAD_DIM = 128
HG_HEADS = HG_WIDTH // HG_HEAD_DIM
HG_CHUNK = 64
SPLIT_SIZES = (S5_WIDTH, HG_WIDTH, HG_WIDTH, HG_WIDTH, HG_WIDTH, D_MODEL, D_MODEL)
IN_COLS = sum(SPLIT_SIZES)
SPLIT_POINTS = tuple(np.cumsum(SPLIT_SIZES)[:-1].tolist())
N_GROUPS = 4
EXPERTS_PER_GROUP = 8
N_EXPERTS = N_GROUPS * EXPERTS_PER_GROUP
TOP_K = 2
D_EXPERT = 256
PLE_DIM = 256
DEEPNORM_ALPHA = (2 * DEPTH) ** 0.25
DEEPNORM_BETA = (8 * DEPTH) ** -0.25
LN_EPS = 1e-5
RMS_EPS = 1e-6
DT_MIN = 1e-3
DT_MAX = 1e-1

kernel_name = "s5_hgrn2_gated_parallel_hiermoe_decoder_step"

F32 = jnp.float32


def layer_norm(x, gain, bias):
    xf = x.astype(F32)
    mu = jnp.mean(xf, axis=-1, keepdims=True)
    var = jnp.mean(jnp.square(xf - mu), axis=-1, keepdims=True)
    return ((xf - mu) * lax.rsqrt(var + LN_EPS) * gain.astype(F32) + bias.astype(F32)).astype(x.dtype)


def rms_norm(x, gain):
    xf = x.astype(F32)
    return xf * lax.rsqrt(jnp.mean(jnp.square(xf), axis=-1, keepdims=True) + RMS_EPS) * gain.astype(F32)


def s5_mixer(u, s_re, s_im, lam_re, lam_im, log_dt, b_re, b_im, c_re, c_im, d_skip, w_glu):
    bt, t, _ = u.shape
    uf = u.astype(F32)
    dt = jnp.exp(log_dt.astype(F32))[:, None]
    lr = lam_re.astype(F32)
    li = lam_im.astype(F32)
    mag = jnp.exp(lr * dt)
    ab_re = mag * jnp.cos(li * dt)
    ab_im = mag * jnp.sin(li * dt)
    den = lr * lr + li * li
    nr = ab_re - 1.0
    zr = (nr * lr + ab_im * li) / den
    zi = (ab_im * lr - nr * li) / den
    bre = b_re.astype(F32)
    bim = b_im.astype(F32)
    bb_re = zr[..., None] * bre - zi[..., None] * bim
    bb_im = zr[..., None] * bim + zi[..., None] * bre
    ug = uf.reshape(bt, t, S5_GROUPS, S5_GROUP)
    bu_re = jnp.einsum('gnc,btgc->btgn', bb_re, ug)
    bu_im = jnp.einsum('gnc,btgc->btgn', bb_im, ug)
    sr = s_re.astype(F32)
    si = s_im.astype(F32)
    bu_re = bu_re.at[:, 0].add(ab_re * sr - ab_im * si)
    bu_im = bu_im.at[:, 0].add(ab_re * si + ab_im * sr)
    a_re = jnp.broadcast_to(ab_re, bu_re.shape)
    a_im = jnp.broadcast_to(ab_im, bu_im.shape)

    def combine(e1, e2):
        a1r, a1i, b1r, b1i = e1
        a2r, a2i, b2r, b2i = e2
        return (a2r * a1r - a2i * a1i,
                a2r * a1i + a2i * a1r,
                a2r * b1r - a2i * b1i + b2r,
                a2r * b1i + a2i * b1r + b2i)

    _, _, x_re, x_im = lax.associative_scan(combine, (a_re, a_im, bu_re, bu_im), axis=1)
    y = (jnp.einsum('gcn,btgn->btgc', c_re.astype(F32), x_re)
         - jnp.einsum('gcn,btgn->btgc', c_im.astype(F32), x_im))
    y = y.reshape(bt, t, S5_WIDTH) + d_skip.astype(F32) * uf
    y = jax.nn.gelu(y)
    y = y * jax.nn.sigmoid(y @ w_glu.astype(F32))
    return y.astype(u.dtype), x_re[:, -1], x_im[:, -1]


def gated_recurrence_blocks(q, k, v, logf, s0):
    bt, t, h, dk = q.shape
    c = min(HG_CHUNK, t)
    n_blocks = -(-t // c)
    pad = n_blocks * c - t
    if pad:
        pw = ((0, 0), (0, pad), (0, 0), (0, 0))
        q, k, v, logf = [jnp.pad(a, pw) for a in (q, k, v, logf)]

    def to_blocks(a):
        return jnp.moveaxis(a.reshape(bt, n_blocks, c, h, a.shape[-1]), 1, 0)

    causal = jnp.tril(jnp.ones((c, c), dtype=bool))[None, :, :, None, None]

    def step(s, blk):
        qc, kc, vc, gc = blk
        g_cum = jnp.cumsum(gc, axis=1)
        inter = jnp.einsum('bthk,bhkv->bthv', qc * jnp.exp(g_cum), s)
        diff = g_cum[:, :, None] - g_cum[:, None, :]
        decay = jnp.exp(jnp.where(causal, diff, -jnp.inf))
        scores = jnp.einsum('bthk,btshk,bshk->btsh', qc, decay, kc)
        intra = jnp.einsum('btsh,bshv->bthv', scores, vc)
        g_last = g_cum[:, -1]
        k_dec = kc * jnp.exp(g_last[:, None] - g_cum)
        s_new = jnp.exp(g_last)[..., None] * s + jnp.einsum('bshk,bshv->bhkv', k_dec, vc)
        return s_new, inter + intra

    s_fin, o = lax.scan(step, s0, (to_blocks(q), to_blocks(k), to_blocks(v), to_blocks(logf)))
    o = jnp.moveaxis(o, 0, 1).reshape(bt, n_blocks * c, h, v.shape[-1])[:, :t]
    return o, s_fin


def hgrn2_mixer(q_pre, f_pre, i_pre, g_pre, lb, s0, gn_gain):
    bt, t, _ = q_pre.shape
    shp = (bt, t, HG_HEADS, HG_HEAD_DIM)
    q = jax.nn.silu(q_pre.astype(F32)).reshape(shp)
    f = lb + (1.0 - lb) * jax.nn.sigmoid(f_pre.astype(F32))
    logf = jnp.log(f).reshape(shp)
    k = (1.0 - f).reshape(shp)
    v = i_pre.astype(F32).reshape(shp)
    o, s_new = gated_recurrence_blocks(q, k, v, logf, s0.astype(F32))
    o = rms_norm(o, gn_gain) * jax.nn.silu(g_pre.astype(F32).reshape(shp))
    return o.reshape(bt, t, HG_WIDTH).astype(q_pre.dtype), s_new


def hierarchical_moe(h, w_gr, b_gr, w_er, b_er, w_g, w_u, w_d):
    lead = h.shape[:-1]
    tok = h.reshape(-1, D_MODEL)
    grp_logits = (tok @ w_gr).astype(F32) + b_gr.astype(F32)
    grp_prob = jax.nn.softmax(grp_logits, axis=-1)
    grp_p, grp_idx = lax.top_k(grp_prob, 1)
    exp_logits = ((tok @ w_er).astype(F32) + b_er.astype(F32)).reshape(-1, N_GROUPS, EXPERTS_PER_GROUP)
    in_grp = jnp.einsum('mge,mg->me', exp_logits, jax.nn.one_hot(grp_idx[:, 0], N_GROUPS, dtype=F32))
    top_logit, top_local = lax.top_k(in_grp, TOP_K)
    top_w = jax.nn.softmax(top_logit, axis=-1) * grp_p
    top_e = grp_idx * EXPERTS_PER_GROUP + top_local
    combine = jnp.sum(jax.nn.one_hot(top_e, N_EXPERTS, dtype=F32) * top_w[..., None], axis=1)
    hid = jax.nn.silu(jnp.einsum('md,edf->mef', tok, w_g)) * jnp.einsum('md,edf->mef', tok, w_u)
    hid = hid * combine[:, :, None].astype(hid.dtype)
    out = jnp.einsum('mef,efd->md', hid, w_d)
    return out.reshape(*lead, D_MODEL)


def trunk_layer(x, p, s5_re, s5_im, hg_s, lb,
                w_in, lam_re, lam_im, log_dt, b_re, b_im, c_re, c_im, d_skip, w_glu, gn_gain,
                w_ba, w_bb, w_o, ln1_g, ln1_b,
                w_gr, b_gr, w_er, b_er, w_g, w_u, w_d,
                w_pp, w_pg, ln2_g, ln2_b):
    z = x @ w_in
    u_a, q_b, f_b, i_b, g_b, gate_a, gate_b = jnp.split(z, SPLIT_POINTS, axis=-1)
    y_a, s5_re_new, s5_im_new = s5_mixer(u_a, s5_re, s5_im, lam_re, lam_im, log_dt,
                                         b_re, b_im, c_re, c_im, d_skip, w_glu)
    y_b, hg_new = hgrn2_mixer(q_b, f_b, i_b, g_b, lb, hg_s, gn_gain)
    merged = jax.nn.sigmoid(gate_a) * (y_a @ w_ba) + jax.nn.sigmoid(gate_b) * (y_b @ w_bb)
    h = layer_norm(DEEPNORM_ALPHA * x + merged @ w_o, ln1_g, ln1_b)
    moe = hierarchical_moe(h, w_gr, b_gr, w_er, b_er, w_g, w_u, w_d)
    ple = jax.nn.sigmoid(h @ w_pg) * (p @ w_pp)
    x_new = layer_norm(DEEPNORM_ALPHA * h + moe + ple, ln2_g, ln2_b)
    return x_new, s5_re_new, s5_im_new, hg_new


def setup_inputs(seed: int = 0) -> dict:
    key = jax.random.key(seed)
    ks = iter(jax.random.split(key, 48))

    def nrm(shape, scale):
        return scale * jax.random.normal(next(ks), shape, F32)

    lam_im_base = jnp.pi * jnp.arange(S5_STATE, dtype=F32)
    return {
        "x_prompt": nrm((BATCH, SEQ, D_MODEL), 1.0),
        "x_sample": nrm((DEC_BATCH, DEC_SEQ, D_MODEL), 1.0),
        "p_prompt": nrm((DEPTH, BATCH, SEQ, PLE_DIM), 1.0),
        "p_sample": nrm((DEPTH, DEC_BATCH, DEC_SEQ, PLE_DIM), 1.0),
        "state_s5_re": nrm((DEPTH, DEC_BATCH, S5_GROUPS, S5_STATE), 0.5),
        "state_s5_im": nrm((DEPTH, DEC_BATCH, S5_GROUPS, S5_STATE), 0.5),
        "state_hgrn": nrm((DEPTH, DEC_BATCH, HG_HEADS, HG_HEAD_DIM, HG_HEAD_DIM), 0.5),
        "w_in": nrm((DEPTH, D_MODEL, IN_COLS), D_MODEL ** -0.5),
        "s5_lambda_re": -0.5 + nrm((DEPTH, S5_GROUPS, S5_STATE), 0.01),
        "s5_lambda_im": lam_im_base + nrm((DEPTH, S5_GROUPS, S5_STATE), 0.01),
        "s5_log_dt": jax.random.uniform(next(ks), (DEPTH, S5_GROUPS), F32,
                                        minval=math.log(DT_MIN), maxval=math.log(DT_MAX)),
        "s5_b_re": nrm((DEPTH, S5_GROUPS, S5_STATE, S5_GROUP), (2 * S5_GROUP) ** -0.5),
        "s5_b_im": nrm((DEPTH, S5_GROUPS, S5_STATE, S5_GROUP), (2 * S5_GROUP) ** -0.5),
        "s5_c_re": nrm((DEPTH, S5_GROUPS, S5_GROUP, S5_STATE), (2 * S5_STATE) ** -0.5),
        "s5_c_im": nrm((DEPTH, S5_GROUPS, S5_GROUP, S5_STATE), (2 * S5_STATE) ** -0.5),
        "s5_d": nrm((DEPTH, S5_WIDTH), 1.0),
        "s5_w_glu": nrm((DEPTH, S5_WIDTH, S5_WIDTH), S5_WIDTH ** -0.5),
        "hg_lower_bounds": 1.0 + nrm((DEPTH, HG_WIDTH), 0.3),
        "hg_norm_gain": 1.0 + nrm((DEPTH, HG_HEAD_DIM), 0.02),
        "w_branch_a": nrm((DEPTH, S5_WIDTH, D_MODEL), S5_WIDTH ** -0.5),
        "w_branch_b": nrm((DEPTH, HG_WIDTH, D_MODEL), HG_WIDTH ** -0.5),
        "w_out": nrm((DEPTH, D_MODEL, D_MODEL), DEEPNORM_BETA * D_MODEL ** -0.5),
        "ln1_gain": 1.0 + nrm((DEPTH, D_MODEL), 0.02),
        "ln1_bias": nrm((DEPTH, D_MODEL), 0.02),
        "w_group_router": nrm((DEPTH, D_MODEL, N_GROUPS), D_MODEL ** -0.5),
        "b_group_router": nrm((DEPTH, N_GROUPS), 0.01),
        "w_expert_router": nrm((DEPTH, D_MODEL, N_EXPERTS), D_MODEL ** -0.5),
        "b_expert_router": nrm((DEPTH, N_EXPERTS), 0.01),
        "w_exp_gate": nrm((DEPTH, N_EXPERTS, D_MODEL, D_EXPERT), D_MODEL ** -0.5),
        "w_exp_up": nrm((DEPTH, N_EXPERTS, D_MODEL, D_EXPERT), D_MODEL ** -0.5),
        "w_exp_down": nrm((DEPTH, N_EXPERTS, D_EXPERT, D_MODEL), DEEPNORM_BETA * D_EXPERT ** -0.5),
        "w_ple_proj": nrm((DEPTH, PLE_DIM, D_MODEL), DEEPNORM_BETA * PLE_DIM ** -0.5),
        "w_ple_gate": nrm((DEPTH, D_MODEL, D_MODEL), D_MODEL ** -0.5),
        "ln2_gain": 1.0 + nrm((DEPTH, D_MODEL), 0.02),
        "ln2_bias": nrm((DEPTH, D_MODEL), 0.02),
    }


def reference(x_prompt, x_sample, p_prompt, p_sample, state_s5_re, state_s5_im, state_hgrn,
              w_in, s5_lambda_re, s5_lambda_im, s5_log_dt, s5_b_re, s5_b_im, s5_c_re, s5_c_im,
              s5_d, s5_w_glu, hg_lower_bounds, hg_norm_gain, w_branch_a, w_branch_b, w_out,
              ln1_gain, ln1_bias, w_group_router, b_group_router, w_expert_router, b_expert_router,
              w_exp_gate, w_exp_up, w_exp_down, w_ple_proj, w_ple_gate, ln2_gain, ln2_bias):
    lb_soft = jax.nn.softmax(hg_lower_bounds.astype(F32), axis=0)
    lb_all = jnp.cumsum(lb_soft, axis=0) - lb_soft[0]

    bp = x_prompt.shape[0]
    zero_re = jnp.zeros((bp, S5_GROUPS, S5_STATE), F32)
    zero_hg = jnp.zeros((bp, HG_HEADS, HG_HEAD_DIM, HG_HEAD_DIM), F32)

    xp, xs = x_prompt, x_sample
    pr_re, pr_im, pr_hg, sm_re, sm_im, sm_hg = [], [], [], [], [], []
    for i in range(DEPTH):
        layer_w = (w_in[i], s5_lambda_re[i], s5_lambda_im[i], s5_log_dt[i], s5_b_re[i], s5_b_im[i],
                   s5_c_re[i], s5_c_im[i], s5_d[i], s5_w_glu[i], hg_norm_gain[i],
                   w_branch_a[i], w_branch_b[i], w_out[i], ln1_gain[i], ln1_bias[i],
                   w_group_router[i], b_group_router[i], w_expert_router[i], b_expert_router[i],
                   w_exp_gate[i], w_exp_up[i], w_exp_down[i], w_ple_proj[i], w_ple_gate[i],
                   ln2_gain[i], ln2_bias[i])
        xp, a_re, a_im, a_hg = trunk_layer(xp, p_prompt[i], zero_re, zero_re, zero_hg, lb_all[i], *layer_w)
        xs, b_re, b_im, b_hg = trunk_layer(xs, p_sample[i], state_s5_re[i], state_s5_im[i],
                                           state_hgrn[i], lb_all[i], *layer_w)
        pr_re.append(a_re); pr_im.append(a_im); pr_hg.append(a_hg)
        sm_re.append(b_re); sm_im.append(b_im); sm_hg.append(b_hg)

    return (xp, xs,
            jnp.stack(pr_re), jnp.stack(pr_im), jnp.stack(pr_hg),
            jnp.stack(sm_re), jnp.stack(sm_im), jnp.stack(sm_hg))
```

```python
import functools
import math

import jax
import jax.numpy as jnp
from jax import lax
from jax.experimental import pallas as pl
from jax.experimental.pallas import tpu as pltpu

F32 = jnp.float32
BF16 = jnp.bfloat16
HIGHEST = lax.Precision.HIGHEST

LANES = 128
SUBLANES = 8
VMEM_LIMIT_BYTES = 56 * 1024 * 1024

S5_GROUP = 16
S5_STATE = 64
S5_CHUNK = 16
HG_HEAD_DIM = 128
HG_CHUNK = 64
HG_SUB = 16
N_GROUPS = 4
EXPERTS_PER_GROUP = 8
N_EXPERTS = N_GROUPS * EXPERTS_PER_GROUP
LN_EPS = 1e-5
RMS_EPS = 1e-6


def _params(semantics):
    return pltpu.CompilerParams(dimension_semantics=semantics, vmem_limit_bytes=VMEM_LIMIT_BYTES)


def _dot(a, b):
    return jnp.dot(a, b, preferred_element_type=F32)


def _dot_nt(a, b):
    return lax.dot_general(a, b, (((1,), (1,)), ((), ())), preferred_element_type=F32)


def _dot_tn(a, b):
    return lax.dot_general(a, b, (((0,), (0,)), ((), ())), preferred_element_type=F32)


def _sigmoid(x):
    return jax.nn.sigmoid(x)


def _silu(x):
    return x * jax.nn.sigmoid(x)


def _layer_norm_rows(x, gain, bias):
    mu = jnp.mean(x, axis=-1, keepdims=True)
    xc = x - mu
    var = jnp.mean(xc * xc, axis=-1, keepdims=True)
    return xc * lax.rsqrt(var + LN_EPS) * gain + bias


def _inproj_kernel(x_ref, w_ref, o_ref, xb_ref):
    @pl.when(pl.program_id(1) == 0)
    def _():
        xb_ref[...] = x_ref[...].astype(BF16)

    o_ref[...] = _dot(xb_ref[...], w_ref[...])


def _inproj(x, w_bf16, tm, tn=1024):
    m, k = x.shape
    n = w_bf16.shape[1]
    return pl.pallas_call(
        _inproj_kernel,
        grid=(m // tm, n // tn),
        in_specs=[pl.BlockSpec((tm, k), lambda i, j: (i, 0)),
                  pl.BlockSpec((k, tn), lambda i, j: (0, j))],
        out_specs=pl.BlockSpec((tm, tn), lambda i, j: (i, j)),
        out_shape=jax.ShapeDtypeStruct((m, n), F32),
        scratch_shapes=[pltpu.VMEM((tm, k), BF16)],
        compiler_params=_params(("parallel", "arbitrary")),
        name="inproj",
    )(x, w_bf16)


def _glu_kernel(a_ref, w_ref, o_ref):
    a = a_ref[...]
    o_ref[...] = a * _sigmoid(_dot(a.astype(BF16), w_ref[...]))


def _glu(a, w_bf16, tm):
    m, k = a.shape
    return pl.pallas_call(
        _glu_kernel,
        grid=(m // tm,),
        in_specs=[pl.BlockSpec((tm, k), lambda i: (i, 0)),
                  pl.BlockSpec((k, k), lambda i: (0, 0))],
        out_specs=pl.BlockSpec((tm, k), lambda i: (i, 0)),
        out_shape=jax.ShapeDtypeStruct((m, k), F32),
        compiler_params=_params(("parallel",)),
        name="s5_glu",
    )(a, w_bf16)


def _merge_kernel(ya_ref, yb_ref, ga_ref, gb_ref, wa_ref, wb_ref, o_ref):
    pa = _dot(ya_ref[...].astype(BF16), wa_ref[...])
    pb = _dot(yb_ref[...].astype(BF16), wb_ref[...])
    o_ref[...] = _sigmoid(ga_ref[...]) * pa + _sigmoid(gb_ref[...]) * pb


def _merge(ya, yb, z, wa_bf16, wb_bf16, tm, tn=1024):
    m, k = ya.shape
    n = wa_bf16.shape[1]
    nj = n // tn
    return pl.pallas_call(
        _merge_kernel,
        grid=(m // tm, nj),
        in_specs=[pl.BlockSpec((tm, k), lambda i, j: (i, 0)),
                  pl.BlockSpec((tm, k), lambda i, j: (i, 0)),
                  pl.BlockSpec((tm, tn), lambda i, j: (i, 5 + j)),
                  pl.BlockSpec((tm, tn), lambda i, j: (i, 5 + nj + j)),
                  pl.BlockSpec((k, tn), lambda i, j: (0, j)),
                  pl.BlockSpec((k, tn), lambda i, j: (0, j))],
        out_specs=pl.BlockSpec((tm, tn), lambda i, j: (i, j)),
        out_shape=jax.ShapeDtypeStruct((m, n), F32),
        compiler_params=_params(("parallel", "arbitrary")),
        name="branch_merge",
    )(ya, yb, z, z, wa_bf16, wb_bf16)


def _ln1_kernel(mg_ref, x_ref, w_ref, g_ref, b_ref, o_ref, *, alpha):
    r = alpha * x_ref[...] + _dot(mg_ref[...].astype(BF16), w_ref[...])
    o_ref[...] = _layer_norm_rows(r, g_ref[...], b_ref[...])


def _ln1(merged, x, w_bf16, gain, bias, alpha, tm):
    m, d = x.shape
    row = pl.BlockSpec((tm, d), lambda i: (i, 0))
    vec = pl.BlockSpec((1, d), lambda i: (0, 0))
    return pl.pallas_call(
        functools.partial(_ln1_kernel, alpha=alpha),
        grid=(m // tm,),
        in_specs=[row, row, pl.BlockSpec((d, d), lambda i: (0, 0)), vec, vec],
        out_specs=row,
        out_shape=jax.ShapeDtypeStruct((m, d), F32),
        compiler_params=_params(("parallel",)),
        name="outproj_ln1",
    )(merged, x, w_bf16, gain.reshape(1, d), bias.reshape(1, d))


def _final_kernel(h_ref, moe_ref, p_ref, wg_ref, wp_ref, g_ref, b_ref, o_ref, *, alpha):
    h = h_ref[...]
    gate = _sigmoid(_dot(h.astype(BF16), wg_ref[...]))
    proj = _dot(p_ref[...].astype(BF16), wp_ref[...])
    r = alpha * h + moe_ref[...] + gate * proj
    o_ref[...] = _layer_norm_rows(r, g_ref[...], b_ref[...])


def _final(h, moe, p, wg_bf16, wp_bf16, gain, bias, alpha, tm):
    m, d = h.shape
    dp = p.shape[1]
    row = pl.BlockSpec((tm, d), lambda i: (i, 0))
    vec = pl.BlockSpec((1, d), lambda i: (0, 0))
    return pl.pallas_call(
        functools.partial(_final_kernel, alpha=alpha),
        grid=(m // tm,),
        in_specs=[row, row, pl.BlockSpec((tm, dp), lambda i: (i, 0)),
                  pl.BlockSpec((d, d), lambda i: (0, 0)),
                  pl.BlockSpec((dp, d), lambda i: (0, 0)), vec, vec],
        out_specs=row,
        out_shape=jax.ShapeDtypeStruct((m, d), F32),
        compiler_params=_params(("parallel",)),
        name="ple_ln2",
    )(h, moe, p, wg_bf16, wp_bf16, gain.reshape(1, d), bias.reshape(1, d))


def _router_kernel(h_ref, w_ref, b_ref, o_ref):
    logits = jnp.dot(h_ref[...], w_ref[...], precision=HIGHEST, preferred_element_type=F32) + b_ref[...]
    lane = lax.broadcasted_iota(jnp.int32, logits.shape, 1)
    lane_f = lane.astype(F32)
    neg = -jnp.inf
    gl = jnp.where(lane >= N_EXPERTS, jnp.where(lane < N_EXPERTS + N_GROUPS, logits, neg), neg)
    gmax = jnp.max(gl, axis=-1, keepdims=True)
    denom = jnp.sum(jnp.exp(gl - gmax), axis=-1, keepdims=True)
    grp_p = 1.0 / denom
    gidx = jnp.min(jnp.where(gl == gmax, lane_f, 1e9), axis=-1, keepdims=True) - N_EXPERTS
    lane_grp = (lane // EXPERTS_PER_GROUP).astype(F32)
    el = jnp.where(lane < N_EXPERTS, jnp.where(lane_grp == gidx, logits, neg), neg)
    t1 = jnp.max(el, axis=-1, keepdims=True)
    i1 = jnp.min(jnp.where(el == t1, lane_f, 1e9), axis=-1, keepdims=True)
    el2 = jnp.where(lane_f == i1, neg, el)
    t2 = jnp.max(el2, axis=-1, keepdims=True)
    i2 = jnp.min(jnp.where(el2 == t2, lane_f, 1e9), axis=-1, keepdims=True)
    e2 = jnp.exp(t2 - t1)
    w1 = 1.0 / (1.0 + e2)
    w2 = e2 * w1
    o_ref[...] = grp_p * jnp.where(lane_f == i1, w1, jnp.where(lane_f == i2, w2, 0.0))


def _router(h, w_r, b_r, tm):
    m, d = h.shape
    return pl.pallas_call(
        _router_kernel,
        grid=(m // tm,),
        in_specs=[pl.BlockSpec((tm, d), lambda i: (i, 0)),
                  pl.BlockSpec((d, LANES), lambda i: (0, 0)),
                  pl.BlockSpec((1, LANES), lambda i: (0, 0))],
        out_specs=pl.BlockSpec((tm, LANES), lambda i: (i, 0)),
        out_shape=jax.ShapeDtypeStruct((m, LANES), F32),
        compiler_params=_params(("parallel",)),
        name="router",
    )(h, w_r, b_r)


def _moe_kernel(h_ref, c_ref, wg_ref, wu_ref, wd_ref, o_ref, hb_ref):
    e = pl.program_id(1)

    @pl.when(e == 0)
    def _():
        hb_ref[...] = h_ref[...].astype(BF16)
        o_ref[...] = jnp.zeros_like(o_ref)

    comb = c_ref[...]
    lane = lax.broadcasted_iota(jnp.int32, comb.shape, 1)
    col = jnp.sum(jnp.where(lane == e, comb, 0.0), axis=-1, keepdims=True)
    hb = hb_ref[...]
    hid = _silu(_dot(hb, wg_ref[0])) * _dot(hb, wu_ref[0]) * col
    o_ref[...] += _dot(hid.astype(BF16), wd_ref[0])


def _moe(h, comb, wg_bf16, wu_bf16, wd_bf16, tm):
    m, d = h.shape
    n_e, _, f = wg_bf16.shape
    return pl.pallas_call(
        _moe_kernel,
        grid=(m // tm, n_e),
        in_specs=[pl.BlockSpec((tm, d), lambda i, e: (i, 0)),
                  pl.BlockSpec((tm, LANES), lambda i, e: (i, 0)),
                  pl.BlockSpec((1, d, f), lambda i, e: (e, 0, 0)),
                  pl.BlockSpec((1, d, f), lambda i, e: (e, 0, 0)),
                  pl.BlockSpec((1, f, d), lambda i, e: (e, 0, 0))],
        out_specs=pl.BlockSpec((tm, d), lambda i, e: (i, 0)),
        out_shape=jax.ShapeDtypeStruct((m, d), F32),
        scratch_shapes=[pltpu.VMEM((tm, d), BF16)],
        compiler_params=_params(("parallel", "arbitrary")),
        name="moe_experts",
    )(h, comb, wg_bf16, wu_bf16, wd_bf16)


def _s5_discretise(lam_re, lam_im, log_dt, b_re, b_im):
    dt = jnp.exp(log_dt)[:, None]
    mag = jnp.exp(lam_re * dt)
    ab_re = mag * jnp.cos(lam_im * dt)
    ab_im = mag * jnp.sin(lam_im * dt)
    den = lam_re * lam_re + lam_im * lam_im
    nr = ab_re - 1.0
    zr = (nr * lam_re + ab_im * lam_im) / den
    zi = (ab_im * lam_re - nr * lam_im) / den
    bb_re = zr[..., None] * b_re - zi[..., None] * b_im
    bb_im = zr[..., None] * b_im + zi[..., None] * b_re
    return ab_re, ab_im, bb_re, bb_im


def _s5_prompt_operators(ab_re, ab_im, bb_re, bb_im, c_re, c_im, d_skip):
    g, n = ab_re.shape
    cch = bb_re.shape[-1]
    L = S5_CHUNK
    pw_re = [jnp.ones_like(ab_re)]
    pw_im = [jnp.zeros_like(ab_im)]
    for _ in range(L):
        pr, pi = pw_re[-1], pw_im[-1]
        pw_re.append(pr * ab_re - pi * ab_im)
        pw_im.append(pr * ab_im + pi * ab_re)
    a_re = jnp.stack(pw_re)
    a_im = jnp.stack(pw_im)
    w_re = a_re[:L, :, :, None] * bb_re - a_im[:L, :, :, None] * bb_im
    w_im = a_re[:L, :, :, None] * bb_im + a_im[:L, :, :, None] * bb_re
    kern = (jnp.einsum('gcn,kgnd->kgcd', c_re, w_re, precision=HIGHEST)
            - jnp.einsum('gcn,kgnd->kgcd', c_im, w_im, precision=HIGHEST))
    t_in = jnp.arange(L)[:, None]
    t_out = jnp.arange(L)[None, :]
    lag = t_out - t_in
    toe = jnp.where((lag >= 0)[:, :, None, None, None], kern[jnp.maximum(lag, 0)], 0.0)
    toe = toe.transpose(2, 0, 4, 1, 3).reshape(g, L * cch, L * cch)

    p_re = w_re[::-1].transpose(1, 0, 3, 2).reshape(g, L * cch, n)
    p_im = w_im[::-1].transpose(1, 0, 3, 2).reshape(g, L * cch, n)
    q_re = (jnp.einsum('gcn,tgn->gntc', c_re, a_re[1:]) - jnp.einsum('gcn,tgn->gntc', c_im, a_im[1:]))
    q_im = -(jnp.einsum('gcn,tgn->gntc', c_re, a_im[1:]) + jnp.einsum('gcn,tgn->gntc', c_im, a_re[1:]))
    q_re = q_re.reshape(g, n, L * cch)
    q_im = q_im.reshape(g, n, L * cch)

    hp = g // 2
    w = L * cch
    p2 = jnp.zeros((hp, 2, w, 2, 2, n), F32)
    q2 = jnp.zeros((hp, 2, 2, n, 2, w), F32)
    for gl in range(2):
        p2 = p2.at[:, gl, :, 0, gl, :].set(p_re[gl::2])
        p2 = p2.at[:, gl, :, 1, gl, :].set(p_im[gl::2])
        q2 = q2.at[:, 0, gl, :, gl, :].set(q_re[gl::2])
        q2 = q2.at[:, 1, gl, :, gl, :].set(q_im[gl::2])
    p2 = p2.reshape(hp, 2 * w, 4 * n)
    q2 = q2.reshape(hp, 4 * n, 2 * w)
    a_chunk = jnp.stack([a_re[L].reshape(hp, 2 * n), a_im[L].reshape(hp, 2 * n)], axis=1)
    d2 = jnp.broadcast_to(d_skip.reshape(hp, 2, 1, cch), (hp, 2, L, cch)).reshape(hp, 1, 2 * w)
    return toe.astype(BF16), p2.astype(BF16), q2.astype(BF16), a_chunk, d2


def _s5_prompt_kernel(u_ref, t_ref, p_ref, q_ref, a_ref, d_ref, y_ref, fre_ref, fim_ref,
                      s_sc, xin_sc, *, n_chunks, bt):
    half = a_ref.shape[-1]
    u0 = u_ref[0]
    u1 = u_ref[1]
    ub0 = u0.astype(BF16)
    ub1 = u1.astype(BF16)
    y_intra = jnp.concatenate([_dot(ub0, t_ref[0]), _dot(ub1, t_ref[1])], axis=1)
    s_sc[...] = _dot(jnp.concatenate([ub0, ub1], axis=1), p_ref[0])
    ar = a_ref[0, 0:1, :]
    ai = a_ref[0, 1:2, :]
    xr = jnp.zeros((bt, half), F32)
    xi = jnp.zeros((bt, half), F32)
    for j in range(n_chunks):
        rows = slice(j * bt, (j + 1) * bt)
        xin_sc[rows, 0:half] = xr
        xin_sc[rows, half:2 * half] = xi
        sr = s_sc[rows, 0:half]
        si = s_sc[rows, half:2 * half]
        xr, xi = ar * xr - ai * xi + sr, ar * xi + ai * xr + si
    fre_ref[...] = xr
    fim_ref[...] = xi
    y_carry = _dot(xin_sc[...].astype(BF16), q_ref[0])
    y = y_intra + y_carry + d_ref[0] * jnp.concatenate([u0, u1], axis=1)
    y = jax.nn.gelu(y)
    w = u0.shape[1]
    y_ref[0] = y[:, :w]
    y_ref[1] = y[:, w:]


def _s5_prompt(u, ops, bt, t):
    toe, p2, q2, a_chunk, d2 = ops
    g = toe.shape[0]
    hp = g // 2
    L = S5_CHUNK
    cch = S5_GROUP
    n_chunks = t // L
    r = n_chunks * bt
    w = L * cch
    half = a_chunk.shape[-1]
    u_t = u.reshape(bt, n_chunks, L, g, cch).transpose(3, 1, 0, 2, 4).reshape(g, r, w)
    y_t, f_re, f_im = pl.pallas_call(
        functools.partial(_s5_prompt_kernel, n_chunks=n_chunks, bt=bt),
        grid=(hp,),
        in_specs=[pl.BlockSpec((2, r, w), lambda i: (i, 0, 0)),
                  pl.BlockSpec((2, w, w), lambda i: (i, 0, 0)),
                  pl.BlockSpec((1, 2 * w, 2 * half), lambda i: (i, 0, 0)),
                  pl.BlockSpec((1, 2 * half, 2 * w), lambda i: (i, 0, 0)),
                  pl.BlockSpec((1, 2, half), lambda i: (i, 0, 0)),
                  pl.BlockSpec((1, 1, 2 * w), lambda i: (i, 0, 0))],
        out_specs=[pl.BlockSpec((2, r, w), lambda i: (i, 0, 0)),
                   pl.BlockSpec((bt, half), lambda i: (0, i)),
                   pl.BlockSpec((bt, half), lambda i: (0, i))],
        out_shape=[jax.ShapeDtypeStruct((g, r, w), F32),
                   jax.ShapeDtypeStruct((bt, hp * half), F32),
                   jax.ShapeDtypeStruct((bt, hp * half), F32)],
        scratch_shapes=[pltpu.VMEM((r, 2 * half), F32), pltpu.VMEM((r, 2 * half), F32)],
        compiler_params=_params(("parallel",)),
        name="s5_prompt",
    )(u_t, toe, p2, q2, a_chunk, d2)
    y = y_t.reshape(g, n_chunks, bt, L, cch).transpose(2, 1, 3, 0, 4).reshape(bt * t, g * cch)
    return y, f_re, f_im


S5_SAMPLE_GROUPS = 8


def _s5_sample_operators(ab_re, ab_im, bb_re, bb_im, c_re, c_im, d_skip):
    g, n = ab_re.shape
    cch = bb_re.shape[-1]
    gb = S5_SAMPLE_GROUPS
    nb = g // gb
    b8 = jnp.zeros((nb, gb, cch, 2, gb, n), F32)
    c8 = jnp.zeros((nb, 2, gb, n, gb, cch), F32)
    bre = bb_re.reshape(nb, gb, n, cch).transpose(0, 1, 3, 2)
    bim = bb_im.reshape(nb, gb, n, cch).transpose(0, 1, 3, 2)
    cre = c_re.reshape(nb, gb, cch, n).transpose(0, 1, 3, 2)
    cim = c_im.reshape(nb, gb, cch, n).transpose(0, 1, 3, 2)
    for gl in range(gb):
        b8 = b8.at[:, gl, :, 0, gl, :].set(bre[:, gl])
        b8 = b8.at[:, gl, :, 1, gl, :].set(bim[:, gl])
        c8 = c8.at[:, 0, gl, :, gl, :].set(cre[:, gl])
        c8 = c8.at[:, 1, gl, :, gl, :].set(-cim[:, gl])
    b8 = b8.reshape(nb, gb * cch, 2 * gb * n)
    c8 = c8.reshape(nb, 2 * gb * n, gb * cch)
    a8 = jnp.stack([ab_re.reshape(nb, gb * n), ab_im.reshape(nb, gb * n)], axis=1)
    d8 = d_skip.reshape(nb, 1, gb * cch)
    return b8, c8, a8, d8


def _s5_sample_kernel(u_ref, sr_ref, si_ref, b_ref, c_ref, a_ref, d_ref, y_ref, nr_ref, ni_ref):
    u = u_ref[...]
    half = sr_ref.shape[-1]
    bu = jnp.dot(u, b_ref[0], precision=HIGHEST, preferred_element_type=F32)
    ar = a_ref[0, 0:1, :]
    ai = a_ref[0, 1:2, :]
    sr = sr_ref[...]
    si = si_ref[...]
    xr = ar * sr - ai * si + bu[:, :half]
    xi = ar * si + ai * sr + bu[:, half:]
    nr_ref[...] = xr
    ni_ref[...] = xi
    y = jnp.dot(jnp.concatenate([xr, xi], axis=1), c_ref[0], precision=HIGHEST,
                preferred_element_type=F32) + d_ref[0] * u
    y_ref[...] = jax.nn.gelu(y)


def _s5_sample(z, s_re, s_im, ops):
    b8, c8, a8, d8 = ops
    nb = b8.shape[0]
    b = z.shape[0]
    wu = b8.shape[1]
    ws = a8.shape[-1]
    return pl.pallas_call(
        _s5_sample_kernel,
        grid=(nb,),
        in_specs=[pl.BlockSpec((b, wu), lambda i: (0, i)),
                  pl.BlockSpec((b, ws), lambda i: (0, i)),
                  pl.BlockSpec((b, ws), lambda i: (0, i)),
                  pl.BlockSpec((1, wu, 2 * ws), lambda i: (i, 0, 0)),
                  pl.BlockSpec((1, 2 * ws, wu), lambda i: (i, 0, 0)),
                  pl.BlockSpec((1, 2, ws), lambda i: (i, 0, 0)),
                  pl.BlockSpec((1, 1, wu), lambda i: (i, 0, 0))],
        out_specs=[pl.BlockSpec((b, wu), lambda i: (0, i)),
                   pl.BlockSpec((b, ws), lambda i: (0, i)),
                   pl.BlockSpec((b, ws), lambda i: (0, i))],
        out_shape=[jax.ShapeDtypeStruct((b, nb * wu), F32),
                   jax.ShapeDtypeStruct((b, nb * ws), F32),
                   jax.ShapeDtypeStruct((b, nb * ws), F32)],
        compiler_params=_params(("parallel",)),
        name="s5_sample",
    )(z, s_re, s_im, b8, c8, a8, d8)


def _hgrn_prompt_kernel(q_ref, f_ref, i_ref, g_ref, lb_ref, gain_ref, y_ref, sfin_ref, st_sc,
                        *, n_chunks):
    c = HG_CHUNK
    sub = HG_SUB
    t = pl.program_id(2)

    @pl.when(t == 0)
    def _():
        st_sc[...] = jnp.zeros_like(st_sc)

    lb = lb_ref[...]
    gain = gain_ref[...]
    row = lax.broadcasted_iota(jnp.int32, (c, 1), 0)
    row_in_sub = row % sub
    tri = (lax.broadcasted_iota(jnp.int32, (c, c), 0) >= lax.broadcasted_iota(jnp.int32, (c, c), 1)).astype(F32)
    neg = -jnp.inf

    def chunk(ci, carry):
        r0 = pl.multiple_of(ci * c, c)
        q = _silu(q_ref[pl.ds(r0, c), :])
        f = lb + (1.0 - lb) * _sigmoid(f_ref[pl.ds(r0, c), :])
        kk = 1.0 - f
        v = i_ref[pl.ds(r0, c), :]
        g = jnp.dot(tri, jnp.log(f), precision=HIGHEST, preferred_element_type=F32)
        g_last = g[c - 1:c, :]
        st = st_sc[...]
        o = _dot_nt((q * jnp.exp(g)).astype(BF16), st.astype(BF16))

        blocks = [jnp.zeros((sub, c), F32)]
        for i in range(1, c // sub):
            g_ref_i = g[i * sub - 1:i * sub, :]
            qi = q[i * sub:(i + 1) * sub, :] * jnp.exp(g[i * sub:(i + 1) * sub, :] - g_ref_i)
            earlier = row < i * sub
            ki = kk * jnp.exp(jnp.where(earlier, g_ref_i - g, neg))
            blocks.append(_dot_nt(qi.astype(BF16), ki.astype(BF16)))
        scores = jnp.concatenate(blocks, axis=0)
        o = o + _dot(scores.astype(BF16), v.astype(BF16))

        o = o + jnp.sum(q * kk, axis=-1, keepdims=True) * v
        for d in range(1, sub):
            gs = pltpu.roll(g, d, 0)
            ks = pltpu.roll(kk, d, 0)
            vs = pltpu.roll(v, d, 0)
            dec = jnp.exp(jnp.where(row_in_sub >= d, g - gs, neg))
            o = o + jnp.sum(q * ks * dec, axis=-1, keepdims=True) * vs

        k_dec = kk * jnp.exp(g_last - g)
        st_sc[...] = st * jnp.exp(g_last) + _dot_tn(v.astype(BF16), k_dec.astype(BF16))

        o = o * lax.rsqrt(jnp.mean(o * o, axis=-1, keepdims=True) + RMS_EPS) * gain
        y_ref[pl.ds(r0, c), :] = o * _silu(g_ref[pl.ds(r0, c), :])
        return carry

    lax.fori_loop(0, n_chunks, chunk, 0)

    @pl.when(t == pl.num_programs(2) - 1)
    def _():
        sfin_ref[0, 0] = st_sc[...].T


def _hgrn_prompt(z, lb, gain, bt, t):
    hd = HG_HEAD_DIM
    n_heads = lb.shape[0] // hd
    tb = min(512, t)
    nt = t // tb
    m = bt * t

    def col(block):
        return pl.BlockSpec((tb, hd), lambda b, h, s: (b * nt + s, block * n_heads + h))

    return pl.pallas_call(
        functools.partial(_hgrn_prompt_kernel, n_chunks=tb // HG_CHUNK),
        grid=(bt, n_heads, nt),
        in_specs=[col(1), col(2), col(3), col(4),
                  pl.BlockSpec((1, hd), lambda b, h, s: (0, h)),
                  pl.BlockSpec((1, hd), lambda b, h, s: (0, 0))],
        out_specs=[pl.BlockSpec((tb, hd), lambda b, h, s: (b * nt + s, h)),
                   pl.BlockSpec((1, 1, hd, hd), lambda b, h, s: (b, h, 0, 0))],
        out_shape=[jax.ShapeDtypeStruct((m, n_heads * hd), F32),
                   jax.ShapeDtypeStruct((bt, n_heads, hd, hd), F32)],
        scratch_shapes=[pltpu.VMEM((hd, hd), F32)],
        compiler_params=_params(("parallel", "parallel", "arbitrary")),
        name="hgrn_prompt",
    )(z, z, z, z, lb.reshape(1, -1), gain.reshape(1, hd))


HG_SAMPLE_KEYS = 32


def _hgrn_sample_kernel(q_ref, f_ref, i_ref, g_ref, lb_ref, gain_ref, s_ref, y_ref, snew_ref,
                        qt_sc, ft_sc, kt_sc, vt_sc, o_sc):
    kb = pl.program_id(1)

    @pl.when(kb == 0)
    def _():
        lb = lb_ref[...]
        f = lb + (1.0 - lb) * _sigmoid(f_ref[...])
        qt_sc[...] = _silu(q_ref[...]).T
        ft_sc[...] = f.T
        kt_sc[...] = (1.0 - f).T
        vt_sc[...] = i_ref[...].T
        o_sc[...] = jnp.zeros_like(o_sc)

    vt = vt_sc[...]
    acc = o_sc[...]
    for kl in range(HG_SAMPLE_KEYS):
        k = kb * HG_SAMPLE_KEYS + kl
        st = s_ref[:, 0, kl, :].T
        s_new = st * ft_sc[pl.ds(k, 1), :] + vt * kt_sc[pl.ds(k, 1), :]
        acc = acc + s_new * qt_sc[pl.ds(k, 1), :]
        snew_ref[:, 0, kl, :] = s_new.T
    o_sc[...] = acc

    @pl.when(kb == pl.num_programs(1) - 1)
    def _():
        o = acc.T
        o = o * lax.rsqrt(jnp.mean(o * o, axis=-1, keepdims=True) + RMS_EPS) * gain_ref[...]
        y_ref[...] = o * _silu(g_ref[...])


def _hgrn_sample(z, state, lb, gain):
    b, n_heads, hd, _ = state.shape
    kbs = HG_SAMPLE_KEYS

    def col(block):
        return pl.BlockSpec((b, hd), lambda h, k: (0, block * n_heads + h))

    sblk = pl.BlockSpec((b, 1, kbs, hd), lambda h, k: (0, h, k, 0))
    return pl.pallas_call(
        _hgrn_sample_kernel,
        grid=(n_heads, hd // kbs),
        in_specs=[col(1), col(2), col(3), col(4),
                  pl.BlockSpec((1, hd), lambda h, k: (0, h)),
                  pl.BlockSpec((1, hd), lambda h, k: (0, 0)),
                  sblk],
        out_specs=[pl.BlockSpec((b, hd), lambda h, k: (0, h)), sblk],
        out_shape=[jax.ShapeDtypeStruct((b, n_heads * hd), F32),
                   jax.ShapeDtypeStruct(state.shape, F32)],
        scratch_shapes=[pltpu.VMEM((hd, b), F32)] * 5,
        compiler_params=_params(("parallel", "arbitrary")),
        name="hgrn_sample",
    )(z, z, z, z, lb.reshape(1, -1), gain.reshape(1, hd), state)


def _trunk_tail(x, z, y_a_pre, y_b, p, lw, alpha, tm):
    y_a = _glu(y_a_pre, lw["w_glu"], tm)
    merged = _merge(y_a, y_b, z, lw["w_ba"], lw["w_bb"], tm)
    tr = min(tm, 256)
    h = _ln1(merged, x, lw["w_o"], lw["ln1_g"], lw["ln1_b"], alpha, tr)
    comb = _router(h, lw["w_router"], lw["b_router"], tm)
    moe = _moe(h, comb, lw["w_g"], lw["w_u"], lw["w_d"], tm)
    return _final(h, moe, p, lw["w_pg"], lw["w_pp"], lw["ln2_g"], lw["ln2_b"], alpha, tr)


def _layer_prompt(x, p, lw, alpha, bt, t):
    tm = min(512, bt * t)
    z = _inproj(x, lw["w_in"], tm)
    y_a_pre, f_re, f_im = _s5_prompt(z[:, :lw["s5_width"]], lw["s5_prompt_ops"], bt, t)
    y_b, hg_new = _hgrn_prompt(z, lw["lb"], lw["gn_gain"], bt, t)
    x_new = _trunk_tail(x, z, y_a_pre, y_b, p, lw, alpha, tm)
    return x_new, f_re, f_im, hg_new


def _layer_sample(x, p, s_re, s_im, hg_s, lw, alpha):
    m = x.shape[0]
    z = _inproj(x, lw["w_in"], m)
    y_a_pre, n_re, n_im = _s5_sample(z, s_re, s_im, lw["s5_sample_ops"])
    y_b, hg_new = _hgrn_sample(z, hg_s, lw["lb"], lw["gn_gain"])
    x_new = _trunk_tail(x, z, y_a_pre, y_b, p, lw, alpha, m)
    return x_new, n_re, n_im, hg_new


def _layer_weights(i, lb_all, w_in, s5_lambda_re, s5_lambda_im, s5_log_dt, s5_b_re, s5_b_im, s5_c_re,
                   s5_c_im, s5_d, s5_w_glu, hg_norm_gain, w_branch_a, w_branch_b, w_out, ln1_gain,
                   ln1_bias, w_group_router, b_group_router, w_expert_router, b_expert_router,
                   w_exp_gate, w_exp_up, w_exp_down, w_ple_proj, w_ple_gate, ln2_gain, ln2_bias):
    disc = _s5_discretise(s5_lambda_re[i], s5_lambda_im[i], s5_log_dt[i], s5_b_re[i], s5_b_im[i])
    d_model = w_in.shape[1]
    pad = LANES - N_EXPERTS - N_GROUPS
    w_router = jnp.concatenate([w_expert_router[i], w_group_router[i], jnp.zeros((d_model, pad), F32)], axis=1)
    b_router = jnp.concatenate([b_expert_router[i], b_group_router[i], jnp.zeros((pad,), F32)]).reshape(1, LANES)
    return {
        "w_in": w_in[i].astype(BF16),
        "s5_width": s5_d.shape[1],
        "s5_prompt_ops": _s5_prompt_operators(*disc, s5_c_re[i], s5_c_im[i], s5_d[i]),
        "s5_sample_ops": _s5_sample_operators(*disc, s5_c_re[i], s5_c_im[i], s5_d[i]),
        "w_glu": s5_w_glu[i].astype(BF16),
        "lb": lb_all[i],
        "gn_gain": hg_norm_gain[i],
        "w_ba": w_branch_a[i].astype(BF16),
        "w_bb": w_branch_b[i].astype(BF16),
        "w_o": w_out[i].astype(BF16),
        "ln1_g": ln1_gain[i], "ln1_b": ln1_bias[i],
        "w_router": w_router, "b_router": b_router,
        "w_g": w_exp_gate[i].astype(BF16),
        "w_u": w_exp_up[i].astype(BF16),
        "w_d": w_exp_down[i].astype(BF16),
        "w_pp": w_ple_proj[i].astype(BF16),
        "w_pg": w_ple_gate[i].astype(BF16),
        "ln2_g": ln2_gain[i], "ln2_b": ln2_bias[i],
    }


def kernel(x_prompt, x_sample, p_prompt, p_sample, state_s5_re, state_s5_im, state_hgrn, w_in, s5_lambda_re, s5_lambda_im, s5_log_dt, s5_b_re, s5_b_im, s5_c_re, s5_c_im, s5_d, s5_w_glu, hg_lower_bounds, hg_norm_gain, w_branch_a, w_branch_b, w_out, ln1_gain, ln1_bias, w_group_router, b_group_router, w_expert_router, b_expert_router, w_exp_gate, w_exp_up, w_exp_down, w_ple_proj, w_ple_gate, ln2_gain, ln2_bias):
    depth = w_in.shape[0]
    bt, t, d_model = x_prompt.shape
    bs = x_sample.shape[0]
    alpha = (2 * depth) ** 0.25
    n_groups, n_state = s5_lambda_re.shape[1:]

    lb_soft = jax.nn.softmax(hg_lower_bounds.astype(F32), axis=0)
    lb_all = jnp.cumsum(lb_soft, axis=0) - lb_soft[0]

    xp = x_prompt.reshape(bt * t, d_model)
    xs = x_sample.reshape(bs, d_model)
    outs = [[] for _ in range(6)]
    for i in range(depth):
        lw = _layer_weights(i, lb_all, w_in, s5_lambda_re, s5_lambda_im, s5_log_dt, s5_b_re, s5_b_im,
                            s5_c_re, s5_c_im, s5_d, s5_w_glu, hg_norm_gain, w_branch_a, w_branch_b,
                            w_out, ln1_gain, ln1_bias, w_group_router, b_group_router, w_expert_router,
                            b_expert_router, w_exp_gate, w_exp_up, w_exp_down, w_ple_proj, w_ple_gate,
                            ln2_gain, ln2_bias)
        xp, a_re, a_im, a_hg = _layer_prompt(xp, p_prompt[i].reshape(bt * t, -1), lw, alpha, bt, t)
        xs, b_re, b_im, b_hg = _layer_sample(
            xs, p_sample[i].reshape(bs, -1),
            state_s5_re[i].reshape(bs, n_groups * n_state), state_s5_im[i].reshape(bs, n_groups * n_state),
            state_hgrn[i], lw, alpha)
        for lst, val in zip(outs, (a_re.reshape(bt, n_groups, n_state), a_im.reshape(bt, n_groups, n_state), a_hg,
                                   b_re.reshape(bs, n_groups, n_state), b_im.reshape(bs, n_groups, n_state), b_hg)):
            lst.append(val)

    return (xp.reshape(bt, t, d_model), xs.reshape(bs, 1, d_model), *[jnp.stack(o) for o in outs])
```

```python
import functools
import math

import jax
import jax.numpy as jnp
from jax import lax
from jax.experimental import pallas as pl
from jax.experimental.pallas import tpu as pltpu

F32 = jnp.float32
BF16 = jnp.bfloat16
HIGHEST = lax.Precision.HIGHEST

LANES = 128
SUBLANES = 8
VMEM_LIMIT_BYTES = 56 * 1024 * 1024

S5_GROUP = 16
S5_STATE = 64
S5_CHUNK = 16
HG_HEAD_DIM = 128
HG_CHUNK = 64
HG_SUB = 16
N_GROUPS = 4
EXPERTS_PER_GROUP = 8
N_EXPERTS = N_GROUPS * EXPERTS_PER_GROUP
LN_EPS = 1e-5
RMS_EPS = 1e-6


def _params(semantics):
    return pltpu.CompilerParams(dimension_semantics=semantics, vmem_limit_bytes=VMEM_LIMIT_BYTES)


def _dot(a, b):
    return jnp.dot(a, b, preferred_element_type=F32)


def _mm(a, w):
    if w.dtype == F32:
        return jnp.dot(a, w, precision=HIGHEST, preferred_element_type=F32)
    return jnp.dot(a.astype(BF16), w, preferred_element_type=F32)


def _dot_nt(a, b):
    return lax.dot_general(a, b, (((1,), (1,)), ((), ())), preferred_element_type=F32)


def _dot_tn(a, b):
    return lax.dot_general(a, b, (((0,), (0,)), ((), ())), preferred_element_type=F32)


def _sigmoid(x):
    return jax.nn.sigmoid(x)


def _silu(x):
    return x * jax.nn.sigmoid(x)


def _layer_norm_rows(x, gain, bias):
    mu = jnp.mean(x, axis=-1, keepdims=True)
    xc = x - mu
    var = jnp.mean(xc * xc, axis=-1, keepdims=True)
    return xc * lax.rsqrt(var + LN_EPS) * gain + bias


def _inproj_kernel(x_ref, w_ref, o_ref, xb_ref):
    if w_ref.dtype == F32:
        o_ref[...] = _mm(x_ref[...], w_ref[...])
        return

    @pl.when(pl.program_id(1) == 0)
    def _():
        xb_ref[...] = x_ref[...].astype(BF16)

    o_ref[...] = _dot(xb_ref[...], w_ref[...])


def _inproj(x, w_bf16, tm, tn=1024):
    m, k = x.shape
    n = w_bf16.shape[1]
    return pl.pallas_call(
        _inproj_kernel,
        grid=(m // tm, n // tn),
        in_specs=[pl.BlockSpec((tm, k), lambda i, j: (i, 0)),
                  pl.BlockSpec((k, tn), lambda i, j: (0, j))],
        out_specs=pl.BlockSpec((tm, tn), lambda i, j: (i, j)),
        out_shape=jax.ShapeDtypeStruct((m, n), F32),
        scratch_shapes=[pltpu.VMEM((tm, k), BF16)],
        compiler_params=_params(("parallel", "arbitrary")),
        name="inproj",
    )(x, w_bf16)


def _glu_kernel(a_ref, w_ref, o_ref):
    a = a_ref[...]
    o_ref[...] = a * _sigmoid(_mm(a, w_ref[...]))


def _glu(a, w_bf16, tm):
    m, k = a.shape
    return pl.pallas_call(
        _glu_kernel,
        grid=(m // tm,),
        in_specs=[pl.BlockSpec((tm, k), lambda i: (i, 0)),
                  pl.BlockSpec((k, k), lambda i: (0, 0))],
        out_specs=pl.BlockSpec((tm, k), lambda i: (i, 0)),
        out_shape=jax.ShapeDtypeStruct((m, k), F32),
        compiler_params=_params(("parallel",)),
        name="s5_glu",
    )(a, w_bf16)


def _merge_kernel(ya_ref, yb_ref, ga_ref, gb_ref, wa_ref, wb_ref, o_ref):
    pa = _mm(ya_ref[...], wa_ref[...])
    pb = _mm(yb_ref[...], wb_ref[...])
    o_ref[...] = _sigmoid(ga_ref[...]) * pa + _sigmoid(gb_ref[...]) * pb


def _merge(ya, yb, z, wa_bf16, wb_bf16, tm, tn=1024):
    m, k = ya.shape
    n = wa_bf16.shape[1]
    nj = n // tn
    return pl.pallas_call(
        _merge_kernel,
        grid=(m // tm, nj),
        in_specs=[pl.BlockSpec((tm, k), lambda i, j: (i, 0)),
                  pl.BlockSpec((tm, k), lambda i, j: (i, 0)),
                  pl.BlockSpec((tm, tn), lambda i, j: (i, 5 + j)),
                  pl.BlockSpec((tm, tn), lambda i, j: (i, 5 + nj + j)),
                  pl.BlockSpec((k, tn), lambda i, j: (0, j)),
                  pl.BlockSpec((k, tn), lambda i, j: (0, j))],
        out_specs=pl.BlockSpec((tm, tn), lambda i, j: (i, j)),
        out_shape=jax.ShapeDtypeStruct((m, n), F32),
        compiler_params=_params(("parallel", "arbitrary")),
        name="branch_merge",
    )(ya, yb, z, z, wa_bf16, wb_bf16)


def _ln1_kernel(mg_ref, x_ref, w_ref, g_ref, b_ref, o_ref, *, alpha):
    r = alpha * x_ref[...] + _mm(mg_ref[...], w_ref[...])
    o_ref[...] = _layer_norm_rows(r, g_ref[...], b_ref[...])


def _ln1(merged, x, w_bf16, gain, bias, alpha, tm):
    m, d = x.shape
    row = pl.BlockSpec((tm, d), lambda i: (i, 0))
    vec = pl.BlockSpec((1, d), lambda i: (0, 0))
    return pl.pallas_call(
        functools.partial(_ln1_kernel, alpha=alpha),
        grid=(m // tm,),
        in_specs=[row, row, pl.BlockSpec((d, d), lambda i: (0, 0)), vec, vec],
        out_specs=row,
        out_shape=jax.ShapeDtypeStruct((m, d), F32),
        compiler_params=_params(("parallel",)),
        name="outproj_ln1",
    )(merged, x, w_bf16, gain.reshape(1, d), bias.reshape(1, d))


def _final_kernel(h_ref, moe_ref, p_ref, wg_ref, wp_ref, g_ref, b_ref, o_ref, *, alpha):
    h = h_ref[...]
    gate = _sigmoid(_mm(h, wg_ref[...]))
    proj = _mm(p_ref[...], wp_ref[...])
    r = alpha * h + moe_ref[...] + gate * proj
    o_ref[...] = _layer_norm_rows(r, g_ref[...], b_ref[...])


def _final(h, moe, p, wg_bf16, wp_bf16, gain, bias, alpha, tm):
    m, d = h.shape
    dp = p.shape[1]
    row = pl.BlockSpec((tm, d), lambda i: (i, 0))
    vec = pl.BlockSpec((1, d), lambda i: (0, 0))
    return pl.pallas_call(
        functools.partial(_final_kernel, alpha=alpha),
        grid=(m // tm,),
        in_specs=[row, row, pl.BlockSpec((tm, dp), lambda i: (i, 0)),
                  pl.BlockSpec((d, d), lambda i: (0, 0)),
                  pl.BlockSpec((dp, d), lambda i: (0, 0)), vec, vec],
        out_specs=row,
        out_shape=jax.ShapeDtypeStruct((m, d), F32),
        compiler_params=_params(("parallel",)),
        name="ple_ln2",
    )(h, moe, p, wg_bf16, wp_bf16, gain.reshape(1, d), bias.reshape(1, d))


def _route(h, w, b):
    logits = jnp.dot(h, w, precision=HIGHEST, preferred_element_type=F32) + b
    lane = lax.broadcasted_iota(jnp.int32, logits.shape, 1)
    lane_f = lane.astype(F32)
    neg = -jnp.inf
    gl = jnp.where(lane >= N_EXPERTS, jnp.where(lane < N_EXPERTS + N_GROUPS, logits, neg), neg)
    gmax = jnp.max(gl, axis=-1, keepdims=True)
    denom = jnp.sum(jnp.exp(gl - gmax), axis=-1, keepdims=True)
    grp_p = 1.0 / denom
    gidx = jnp.min(jnp.where(gl == gmax, lane_f, 1e9), axis=-1, keepdims=True) - N_EXPERTS
    lane_grp = (lane // EXPERTS_PER_GROUP).astype(F32)
    el = jnp.where(lane < N_EXPERTS, jnp.where(lane_grp == gidx, logits, neg), neg)
    t1 = jnp.max(el, axis=-1, keepdims=True)
    i1 = jnp.min(jnp.where(el == t1, lane_f, 1e9), axis=-1, keepdims=True)
    el2 = jnp.where(lane_f == i1, neg, el)
    t2 = jnp.max(el2, axis=-1, keepdims=True)
    i2 = jnp.min(jnp.where(el2 == t2, lane_f, 1e9), axis=-1, keepdims=True)
    e2 = jnp.exp(t2 - t1)
    w1 = 1.0 / (1.0 + e2)
    w2 = e2 * w1
    return lane, lane_f, i1, i2, grp_p * w1, grp_p * w2


def _router_dense_kernel(h_ref, w_ref, b_ref, o_ref):
    _, lane_f, i1, i2, c1, c2 = _route(h_ref[...], w_ref[...], b_ref[...])
    o_ref[...] = jnp.where(lane_f == i1, c1, jnp.where(lane_f == i2, c2, 0.0))


def _router_dense(h, w_r, b_r, tm):
    m, d = h.shape
    return pl.pallas_call(
        _router_dense_kernel,
        grid=(m // tm,),
        in_specs=[pl.BlockSpec((tm, d), lambda i: (i, 0)),
                  pl.BlockSpec((d, LANES), lambda i: (0, 0)),
                  pl.BlockSpec((1, LANES), lambda i: (0, 0))],
        out_specs=pl.BlockSpec((tm, LANES), lambda i: (i, 0)),
        out_shape=jax.ShapeDtypeStruct((m, LANES), F32),
        compiler_params=_params(("parallel",)),
        name="router_dense",
    )(h, w_r, b_r)


def _moe_dense_kernel(h_ref, c_ref, wg_ref, wu_ref, wd_ref, o_ref):
    e = pl.program_id(1)

    @pl.when(e == 0)
    def _():
        o_ref[...] = jnp.zeros_like(o_ref)

    comb = c_ref[...]
    lane = lax.broadcasted_iota(jnp.int32, comb.shape, 1)
    col = jnp.sum(jnp.where(lane == e, comb, 0.0), axis=-1, keepdims=True)
    h = h_ref[...]
    hid = _silu(_mm(h, wg_ref[0])) * _mm(h, wu_ref[0]) * col
    o_ref[...] += _mm(hid, wd_ref[0])


def _moe_dense(h, comb, wg, wu, wd, tm):
    m, d = h.shape
    n_e, _, f = wg.shape
    return pl.pallas_call(
        _moe_dense_kernel,
        grid=(m // tm, n_e),
        in_specs=[pl.BlockSpec((tm, d), lambda i, e: (i, 0)),
                  pl.BlockSpec((tm, LANES), lambda i, e: (i, 0)),
                  pl.BlockSpec((1, d, f), lambda i, e: (e, 0, 0)),
                  pl.BlockSpec((1, d, f), lambda i, e: (e, 0, 0)),
                  pl.BlockSpec((1, f, d), lambda i, e: (e, 0, 0))],
        out_specs=pl.BlockSpec((tm, d), lambda i, e: (i, 0)),
        out_shape=jax.ShapeDtypeStruct((m, d), F32),
        compiler_params=_params(("parallel", "arbitrary")),
        name="moe_dense",
    )(h, comb, wg, wu, wd)


MOE_TILE = 256


def _moe_num_tiles(m):
    return 2 * m // MOE_TILE + N_EXPERTS


def _router_sorted_kernel(h_ref, w_ref, b_ref, s1_ref, s2_ref, cw_ref, cnt_ref, carry_sc):
    @pl.when(pl.program_id(0) == 0)
    def _():
        carry_sc[...] = jnp.zeros_like(carry_sc)

    lane, lane_f, i1, i2, c1, c2 = _route(h_ref[...], w_ref[...], b_ref[...])
    sel1 = jnp.where(lane_f == i1, 1.0, 0.0)
    sel2 = jnp.where(lane_f == i2, 1.0, 0.0)
    cnt = sel1 + sel2
    tm = cnt.shape[0]
    strict_lower = (lax.broadcasted_iota(jnp.int32, (tm, tm), 0) > lax.broadcasted_iota(jnp.int32, (tm, tm), 1))
    prefix = _dot(strict_lower.astype(BF16), cnt.astype(BF16)) + carry_sc[0:1, :]
    s1_ref[...] = sel1 * (prefix + 1.0)
    s2_ref[...] = sel2 * (prefix + 1.0)
    cw_ref[...] = jnp.where(lane == 0, c1, jnp.where(lane == 1, c2, 0.0))
    carry_sc[...] = carry_sc[...] + jnp.sum(cnt, axis=0, keepdims=True)
    cnt_ref[...] = carry_sc[...]


def _router_sorted(h, w_r, b_r, tm):
    m, d = h.shape
    tok = pl.BlockSpec((tm, LANES), lambda i: (i, 0))
    return pl.pallas_call(
        _router_sorted_kernel,
        grid=(m // tm,),
        in_specs=[pl.BlockSpec((tm, d), lambda i: (i, 0)),
                  pl.BlockSpec((d, LANES), lambda i: (0, 0)),
                  pl.BlockSpec((1, LANES), lambda i: (0, 0))],
        out_specs=[tok, tok, tok, pl.BlockSpec((SUBLANES, LANES), lambda i: (0, 0))],
        out_shape=[jax.ShapeDtypeStruct((m, LANES), F32)] * 3 + [jax.ShapeDtypeStruct((SUBLANES, LANES), F32)],
        scratch_shapes=[pltpu.VMEM((SUBLANES, LANES), F32)],
        compiler_params=_params(("arbitrary",)),
        name="router_sorted",
    )(h, w_r, b_r)


def _plan_kernel(s1_ref, s2_ref, cnt_ref, pos_ref, tab_ref):
    te = float(MOE_TILE)
    lane = lax.broadcasted_iota(jnp.int32, (1, LANES), 1)
    cnt = jnp.where(lane < N_EXPERTS, cnt_ref[0:1, :], 0.0)
    padded = jnp.floor((cnt + (te - 1.0)) * (1.0 / te)) * te
    r128 = lax.broadcasted_iota(jnp.int32, (LANES, LANES), 0)
    c128 = lax.broadcasted_iota(jnp.int32, (LANES, LANES), 1)
    before = jnp.where(r128 < c128, 1.0, 0.0)
    off = jnp.dot(jnp.broadcast_to(padded, (SUBLANES, LANES)), before, precision=HIGHEST,
                  preferred_element_type=F32)[0:1, :]
    s1 = s1_ref[...]
    s2 = s2_ref[...]
    v1 = jnp.where(s1 > 0.0, s1 - 1.0 + off, 0.0)
    v2 = jnp.where(s2 > 0.0, s2 - 1.0 + off, 0.0)
    ones8 = jnp.ones((SUBLANES, LANES), F32)
    p1 = lax.dot_general(ones8, v1, (((1,), (1,)), ((), ())), precision=HIGHEST, preferred_element_type=F32)
    p2 = lax.dot_general(ones8, v2, (((1,), (1,)), ((), ())), precision=HIGHEST, preferred_element_type=F32)
    row8 = lax.broadcasted_iota(jnp.int32, p1.shape, 0)
    pos_ref[...] = jnp.where(row8 == 0, p1, p2).astype(jnp.int32)

    def per_expert(row):
        return jnp.broadcast_to(row, (LANES, LANES)).T
    off_e = per_expert(off)
    end_e = per_expert(off + padded)
    cnt_e = per_expert(cnt)
    start = c128.astype(F32) * te
    is_e = r128 < N_EXPERTS
    tile_e = jnp.sum(jnp.where(is_e, jnp.where(end_e <= start, 1.0, 0.0), 0.0), axis=0, keepdims=True)
    rows = jnp.clip(cnt_e - (start - off_e), 0.0, te)
    owns = jnp.where(is_e, jnp.where(off_e <= start, jnp.where(start < end_e, rows, 0.0), 0.0), 0.0)
    n_rows = jnp.sum(owns, axis=0, keepdims=True)
    tile_e = jnp.minimum(tile_e, float(N_EXPERTS - 1))
    row_t = lax.broadcasted_iota(jnp.int32, (SUBLANES, LANES), 0)
    tab_ref[...] = jnp.where(row_t == 0, tile_e, n_rows).astype(jnp.int32)


def _plan(s1, s2, counts, tm):
    m = s1.shape[0]
    tok = pl.BlockSpec((tm, LANES), lambda i: (i, 0))
    return pl.pallas_call(
        _plan_kernel,
        grid=(m // tm,),
        in_specs=[tok, tok, pl.BlockSpec((SUBLANES, LANES), lambda i: (0, 0))],
        out_specs=[pl.BlockSpec((SUBLANES, tm), lambda i: (0, i)),
                   pl.BlockSpec((SUBLANES, LANES), lambda i: (0, 0))],
        out_shape=[jax.ShapeDtypeStruct((SUBLANES, m), jnp.int32),
                   jax.ShapeDtypeStruct((SUBLANES, LANES), jnp.int32)],
        compiler_params=_params(("arbitrary",)),
        name="moe_plan",
    )(s1, s2, counts)


def _invert_kernel(pos_ref, src_ref, *, m):
    def clear(i, c):
        src_ref[i] = 0
        return c

    lax.fori_loop(0, src_ref.shape[0], clear, 0)

    def place(a, c):
        src_ref[pos_ref[a]] = a % m
        return c

    lax.fori_loop(0, 2 * m, place, 0)


def _invert(pos, m):
    n_rows = _moe_num_tiles(m) * MOE_TILE
    smem = pl.BlockSpec(memory_space=pltpu.SMEM)
    return pl.pallas_call(
        functools.partial(_invert_kernel, m=m),
        in_specs=[smem],
        out_specs=smem,
        out_shape=jax.ShapeDtypeStruct((n_rows,), jnp.int32),
        name="moe_invert",
    )(pos)


def _row_copy(src_hbm, row, dst_buf, slot, dst_row, sem):
    return pltpu.make_async_copy(src_hbm.at[pl.ds(row, 1)], dst_buf.at[slot, pl.ds(dst_row, 1)], sem.at[slot])


def _experts_kernel(te_ref, nr_ref, src_ref, h_hbm, wg_ref, wu_ref, wd_ref, y_ref,
                    xbuf, sem, wgb, wub, wdb):
    j = pl.program_id(0)
    slot = j % 2

    def gather(tile, dst_slot):
        def body(r, c):
            _row_copy(h_hbm, src_ref[tile * MOE_TILE + r], xbuf, dst_slot, r, sem).start()
            return c
        lax.fori_loop(0, nr_ref[tile], body, 0)

    @pl.when(j == 0)
    def _():
        xbuf[...] = jnp.zeros_like(xbuf)
        gather(0, 0)

    @pl.when(j + 1 < pl.num_programs(0))
    def _():
        gather(j + 1, 1 - slot)

    def wait_one(r, c):
        _row_copy(h_hbm, 0, xbuf, slot, 0, sem).wait()
        return c
    lax.fori_loop(0, nr_ref[j], wait_one, 0)

    new_expert = jnp.logical_or(j == 0, te_ref[j] != te_ref[jnp.maximum(j - 1, 0)])

    @pl.when(new_expert)
    def _():
        wgb[...] = wg_ref[0].astype(BF16)
        wub[...] = wu_ref[0].astype(BF16)
        wdb[...] = wd_ref[0].astype(BF16)

    @pl.when(nr_ref[j] > 0)
    def _():
        x = xbuf[slot].astype(BF16)
        hid = _silu(_dot(x, wgb[...])) * _dot(x, wub[...])
        y_ref[...] = _dot(hid.astype(BF16), wdb[...])

    @pl.when(nr_ref[j] == 0)
    def _():
        y_ref[...] = jnp.zeros_like(y_ref)


def _experts_sorted(h, tile_expert, tile_rows, src, wg, wu, wd):
    m, d = h.shape
    n_e, _, f = wg.shape
    n_tiles = _moe_num_tiles(m)
    grid_spec = pltpu.PrefetchScalarGridSpec(
        num_scalar_prefetch=3,
        grid=(n_tiles,),
        in_specs=[pl.BlockSpec(memory_space=pl.ANY),
                  pl.BlockSpec((1, d, f), lambda j, te, nr, src: (te[j], 0, 0)),
                  pl.BlockSpec((1, d, f), lambda j, te, nr, src: (te[j], 0, 0)),
                  pl.BlockSpec((1, f, d), lambda j, te, nr, src: (te[j], 0, 0))],
        out_specs=pl.BlockSpec((MOE_TILE, d), lambda j, te, nr, src: (j, 0)),
        scratch_shapes=[pltpu.VMEM((2, MOE_TILE, d), F32),
                        pltpu.SemaphoreType.DMA((2,)),
                        pltpu.VMEM((d, f), BF16), pltpu.VMEM((d, f), BF16), pltpu.VMEM((f, d), BF16)],
    )
    return pl.pallas_call(
        _experts_kernel,
        grid_spec=grid_spec,
        out_shape=jax.ShapeDtypeStruct((n_tiles * MOE_TILE, d), F32),
        compiler_params=_params(("arbitrary",)),
        name="moe_experts_sorted",
    )(tile_expert, tile_rows, src, h, wg, wu, wd)


def _final_gather_kernel(pos_ref, h_ref, cw_ref, p_ref, wg_ref, wp_ref, g_ref, b_ref, ys_hbm, o_ref,
                         ybuf, sem, *, alpha, m):
    i = pl.program_id(0)
    slot = i % 2
    tm = h_ref.shape[0]

    def gather(tile, dst_slot):
        def body(r, c):
            t = tile * tm + r
            _row_copy(ys_hbm, pos_ref[t], ybuf, 2 * dst_slot, r, sem).start()
            _row_copy(ys_hbm, pos_ref[m + t], ybuf, 2 * dst_slot + 1, r, sem).start()
            return c
        lax.fori_loop(0, tm, body, 0)

    @pl.when(i == 0)
    def _():
        gather(0, 0)

    @pl.when(i + 1 < pl.num_programs(0))
    def _():
        gather(i + 1, 1 - slot)

    h = h_ref[...]
    gate = _sigmoid(_dot(h.astype(BF16), wg_ref[...]))
    proj = _dot(p_ref[...].astype(BF16), wp_ref[...])

    def wait_pair(r, c):
        _row_copy(ys_hbm, 0, ybuf, 2 * slot, 0, sem).wait()
        _row_copy(ys_hbm, 0, ybuf, 2 * slot + 1, 0, sem).wait()
        return c
    lax.fori_loop(0, tm, wait_pair, 0)

    cw = cw_ref[...]
    moe = cw[:, 0:1] * ybuf[2 * slot] + cw[:, 1:2] * ybuf[2 * slot + 1]
    r = alpha * h + moe + gate * proj
    o_ref[...] = _layer_norm_rows(r, g_ref[...], b_ref[...])


def _final_gather(h, cw, p, pos, ys, wg_bf16, wp_bf16, gain, bias, alpha, tm):
    m, d = h.shape
    dp = p.shape[1]
    row = pl.BlockSpec((tm, d), lambda i, pos: (i, 0))
    vec = pl.BlockSpec((1, d), lambda i, pos: (0, 0))
    grid_spec = pltpu.PrefetchScalarGridSpec(
        num_scalar_prefetch=1,
        grid=(m // tm,),
        in_specs=[row, pl.BlockSpec((tm, LANES), lambda i, pos: (i, 0)),
                  pl.BlockSpec((tm, dp), lambda i, pos: (i, 0)),
                  pl.BlockSpec((d, d), lambda i, pos: (0, 0)),
                  pl.BlockSpec((dp, d), lambda i, pos: (0, 0)), vec, vec,
                  pl.BlockSpec(memory_space=pl.ANY)],
        out_specs=row,
        scratch_shapes=[pltpu.VMEM((4, tm, d), F32), pltpu.SemaphoreType.DMA((4,))],
    )
    return pl.pallas_call(
        functools.partial(_final_gather_kernel, alpha=alpha, m=m),
        grid_spec=grid_spec,
        out_shape=jax.ShapeDtypeStruct((m, d), F32),
        compiler_params=_params(("arbitrary",)),
        name="ple_ln2_gather",
    )(pos, h, cw, p, wg_bf16, wp_bf16, gain.reshape(1, d), bias.reshape(1, d), ys)


def _s5_discretise(lam_re, lam_im, log_dt, b_re, b_im):
    dt = jnp.exp(log_dt)[:, None]
    mag = jnp.exp(lam_re * dt)
    ab_re = mag * jnp.cos(lam_im * dt)
    ab_im = mag * jnp.sin(lam_im * dt)
    den = lam_re * lam_re + lam_im * lam_im
    nr = ab_re - 1.0
    zr = (nr * lam_re + ab_im * lam_im) / den
    zi = (ab_im * lam_re - nr * lam_im) / den
    bb_re = zr[..., None] * b_re - zi[..., None] * b_im
    bb_im = zr[..., None] * b_im + zi[..., None] * b_re
    return ab_re, ab_im, bb_re, bb_im


def _s5_prompt_operators(ab_re, ab_im, bb_re, bb_im, c_re, c_im, d_skip):
    g, n = ab_re.shape
    cch = bb_re.shape[-1]
    L = S5_CHUNK
    pw_re = [jnp.ones_like(ab_re)]
    pw_im = [jnp.zeros_like(ab_im)]
    for _ in range(L):
        pr, pi = pw_re[-1], pw_im[-1]
        pw_re.append(pr * ab_re - pi * ab_im)
        pw_im.append(pr * ab_im + pi * ab_re)
    a_re = jnp.stack(pw_re)
    a_im = jnp.stack(pw_im)
    w_re = a_re[:L, :, :, None] * bb_re - a_im[:L, :, :, None] * bb_im
    w_im = a_re[:L, :, :, None] * bb_im + a_im[:L, :, :, None] * bb_re
    kern = (jnp.einsum('gcn,kgnd->kgcd', c_re, w_re, precision=HIGHEST)
            - jnp.einsum('gcn,kgnd->kgcd', c_im, w_im, precision=HIGHEST))
    zero_lags = jnp.zeros_like(kern)
    toe = jnp.stack([jnp.concatenate([zero_lags[:ti], kern[:L - ti]], axis=0) for ti in range(L)])
    toe = toe.transpose(2, 0, 4, 1, 3).reshape(g, L * cch, L * cch)

    p_re = w_re[::-1].transpose(1, 0, 3, 2).reshape(g, L * cch, n)
    p_im = w_im[::-1].transpose(1, 0, 3, 2).reshape(g, L * cch, n)
    q_re = (jnp.einsum('gcn,tgn->gntc', c_re, a_re[1:]) - jnp.einsum('gcn,tgn->gntc', c_im, a_im[1:]))
    q_im = -(jnp.einsum('gcn,tgn->gntc', c_re, a_im[1:]) + jnp.einsum('gcn,tgn->gntc', c_im, a_re[1:]))
    q_re = q_re.reshape(g, n, L * cch)
    q_im = q_im.reshape(g, n, L * cch)

    hp = g // 2
    w = L * cch
    eye2 = jnp.eye(2, dtype=F32)
    p_ri = jnp.stack([p_re, p_im], axis=2).reshape(hp, 2, w, 2, n)
    q_ri = jnp.stack([q_re, q_im], axis=1).reshape(hp, 2, 2, n, w)
    p2 = (p_ri[:, :, :, :, None, :] * eye2[None, :, None, None, :, None]).reshape(hp, 2 * w, 4 * n)
    q2 = (q_ri.transpose(0, 2, 1, 3, 4)[:, :, :, :, None, :]
          * eye2[None, None, :, None, :, None]).reshape(hp, 4 * n, 2 * w)
    a_chunk = jnp.stack([a_re[L].reshape(hp, 2 * n), a_im[L].reshape(hp, 2 * n)], axis=1)
    d2 = jnp.broadcast_to(d_skip.reshape(hp, 2, 1, cch), (hp, 2, L, cch)).reshape(hp, 1, 2 * w)
    return toe.astype(BF16), p2.astype(BF16), q2.astype(BF16), a_chunk, d2


def _s5_prompt_kernel(u_ref, t_ref, p_ref, q_ref, a_ref, d_ref, y_ref, fre_ref, fim_ref,
                      s_sc, xin_sc, *, n_chunks, bt):
    half = a_ref.shape[-1]
    u0 = u_ref[0]
    u1 = u_ref[1]
    ub0 = u0.astype(BF16)
    ub1 = u1.astype(BF16)
    y_intra = jnp.concatenate([_dot(ub0, t_ref[0]), _dot(ub1, t_ref[1])], axis=1)
    s_sc[...] = _dot(jnp.concatenate([ub0, ub1], axis=1), p_ref[0])
    ar = a_ref[0, 0:1, :]
    ai = a_ref[0, 1:2, :]
    xr = jnp.zeros((bt, half), F32)
    xi = jnp.zeros((bt, half), F32)
    for j in range(n_chunks):
        rows = slice(j * bt, (j + 1) * bt)
        xin_sc[rows, 0:half] = xr
        xin_sc[rows, half:2 * half] = xi
        sr = s_sc[rows, 0:half]
        si = s_sc[rows, half:2 * half]
        xr, xi = ar * xr - ai * xi + sr, ar * xi + ai * xr + si
    fre_ref[...] = xr
    fim_ref[...] = xi
    y_carry = _dot(xin_sc[...].astype(BF16), q_ref[0])
    y = y_intra + y_carry + d_ref[0] * jnp.concatenate([u0, u1], axis=1)
    y = jax.nn.gelu(y)
    w = u0.shape[1]
    y_ref[0] = y[:, :w]
    y_ref[1] = y[:, w:]


def _s5_prompt(u, ops, bt, t):
    toe, p2, q2, a_chunk, d2 = ops
    g = toe.shape[0]
    hp = g // 2
    L = S5_CHUNK
    cch = S5_GROUP
    n_chunks = t // L
    r = n_chunks * bt
    w = L * cch
    half = a_chunk.shape[-1]
    u_t = u.reshape(bt, n_chunks, L, g, cch).transpose(3, 1, 0, 2, 4).reshape(g, r, w)
    y_t, f_re, f_im = pl.pallas_call(
        functools.partial(_s5_prompt_kernel, n_chunks=n_chunks, bt=bt),
        grid=(hp,),
        in_specs=[pl.BlockSpec((2, r, w), lambda i: (i, 0, 0)),
                  pl.BlockSpec((2, w, w), lambda i: (i, 0, 0)),
                  pl.BlockSpec((1, 2 * w, 2 * half), lambda i: (i, 0, 0)),
                  pl.BlockSpec((1, 2 * half, 2 * w), lambda i: (i, 0, 0)),
                  pl.BlockSpec((1, 2, half), lambda i: (i, 0, 0)),
                  pl.BlockSpec((1, 1, 2 * w), lambda i: (i, 0, 0))],
        out_specs=[pl.BlockSpec((2, r, w), lambda i: (i, 0, 0)),
                   pl.BlockSpec((bt, half), lambda i: (0, i)),
                   pl.BlockSpec((bt, half), lambda i: (0, i))],
        out_shape=[jax.ShapeDtypeStruct((g, r, w), F32),
                   jax.ShapeDtypeStruct((bt, hp * half), F32),
                   jax.ShapeDtypeStruct((bt, hp * half), F32)],
        scratch_shapes=[pltpu.VMEM((r, 2 * half), F32), pltpu.VMEM((r, 2 * half), F32)],
        compiler_params=_params(("parallel",)),
        name="s5_prompt",
    )(u_t, toe, p2, q2, a_chunk, d2)
    y = y_t.reshape(g, n_chunks, bt, L, cch).transpose(2, 1, 3, 0, 4).reshape(bt * t, g * cch)
    return y, f_re, f_im


S5_SAMPLE_GROUPS = 8


def _s5_sample_operators(ab_re, ab_im, bb_re, bb_im, c_re, c_im, d_skip):
    g, n = ab_re.shape
    cch = bb_re.shape[-1]
    gb = S5_SAMPLE_GROUPS
    nb = g // gb
    eye = jnp.eye(gb, dtype=F32)
    b_ri = jnp.stack([bb_re, bb_im], axis=1).reshape(nb, gb, 2, n, cch)
    c_ri = jnp.stack([c_re, -c_im], axis=1).reshape(nb, gb, 2, cch, n)
    b8 = (b_ri.transpose(0, 1, 4, 2, 3)[:, :, :, :, None, :]
          * eye[None, :, None, None, :, None]).reshape(nb, gb * cch, 2 * gb * n)
    c8 = (c_ri.transpose(0, 2, 1, 4, 3)[:, :, :, :, None, :]
          * eye[None, None, :, None, :, None]).reshape(nb, 2 * gb * n, gb * cch)
    a8 = jnp.stack([ab_re.reshape(nb, gb * n), ab_im.reshape(nb, gb * n)], axis=1)
    d8 = d_skip.reshape(nb, 1, gb * cch)
    return b8, c8, a8, d8


def _s5_sample_kernel(u_ref, sr_ref, si_ref, b_ref, c_ref, a_ref, d_ref, y_ref, nr_ref, ni_ref):
    u = u_ref[...]
    half = sr_ref.shape[-1]
    bu = jnp.dot(u, b_ref[0], precision=HIGHEST, preferred_element_type=F32)
    ar = a_ref[0, 0:1, :]
    ai = a_ref[0, 1:2, :]
    sr = sr_ref[...]
    si = si_ref[...]
    xr = ar * sr - ai * si + bu[:, :half]
    xi = ar * si + ai * sr + bu[:, half:]
    nr_ref[...] = xr
    ni_ref[...] = xi
    y = jnp.dot(jnp.concatenate([xr, xi], axis=1), c_ref[0], precision=HIGHEST,
                preferred_element_type=F32) + d_ref[0] * u
    y_ref[...] = jax.nn.gelu(y)


def _s5_sample(z, s_re, s_im, ops):
    b8, c8, a8, d8 = ops
    nb = b8.shape[0]
    b = z.shape[0]
    wu = b8.shape[1]
    ws = a8.shape[-1]
    return pl.pallas_call(
        _s5_sample_kernel,
        grid=(nb,),
        in_specs=[pl.BlockSpec((b, wu), lambda i: (0, i)),
                  pl.BlockSpec((b, ws), lambda i: (0, i)),
                  pl.BlockSpec((b, ws), lambda i: (0, i)),
                  pl.BlockSpec((1, wu, 2 * ws), lambda i: (i, 0, 0)),
                  pl.BlockSpec((1, 2 * ws, wu), lambda i: (i, 0, 0)),
                  pl.BlockSpec((1, 2, ws), lambda i: (i, 0, 0)),
                  pl.BlockSpec((1, 1, wu), lambda i: (i, 0, 0))],
        out_specs=[pl.BlockSpec((b, wu), lambda i: (0, i)),
                   pl.BlockSpec((b, ws), lambda i: (0, i)),
                   pl.BlockSpec((b, ws), lambda i: (0, i))],
        out_shape=[jax.ShapeDtypeStruct((b, nb * wu), F32),
                   jax.ShapeDtypeStruct((b, nb * ws), F32),
                   jax.ShapeDtypeStruct((b, nb * ws), F32)],
        compiler_params=_params(("parallel",)),
        name="s5_sample",
    )(z, s_re, s_im, b8, c8, a8, d8)


def _hgrn_prompt_kernel(q_ref, f_ref, i_ref, g_ref, lb_ref, gain_ref, y_ref, sfin_ref, st_sc,
                        *, n_chunks):
    c = HG_CHUNK
    sub = HG_SUB
    t = pl.program_id(2)

    @pl.when(t == 0)
    def _():
        st_sc[...] = jnp.zeros_like(st_sc)

    lb = lb_ref[...]
    gain = gain_ref[...]
    row = lax.broadcasted_iota(jnp.int32, (c, 1), 0)
    row_in_sub = row % sub
    tri = (lax.broadcasted_iota(jnp.int32, (c, c), 0) >= lax.broadcasted_iota(jnp.int32, (c, c), 1)).astype(F32)
    neg = -jnp.inf

    def chunk(ci, carry):
        r0 = pl.multiple_of(ci * c, c)
        q = _silu(q_ref[pl.ds(r0, c), :])
        f = lb + (1.0 - lb) * _sigmoid(f_ref[pl.ds(r0, c), :])
        kk = 1.0 - f
        v = i_ref[pl.ds(r0, c), :]
        g = jnp.dot(tri, jnp.log(f), precision=HIGHEST, preferred_element_type=F32)
        g_last = g[c - 1:c, :]
        st = st_sc[...]
        o = _dot_nt((q * jnp.exp(g)).astype(BF16), st.astype(BF16))

        blocks = [jnp.zeros((sub, c), F32)]
        for i in range(1, c // sub):
            g_ref_i = g[i * sub - 1:i * sub, :]
            qi = q[i * sub:(i + 1) * sub, :] * jnp.exp(g[i * sub:(i + 1) * sub, :] - g_ref_i)
            earlier = row < i * sub
            ki = kk * jnp.exp(jnp.where(earlier, g_ref_i - g, neg))
            blocks.append(_dot_nt(qi.astype(BF16), ki.astype(BF16)))
        scores = jnp.concatenate(blocks, axis=0)
        o = o + _dot(scores.astype(BF16), v.astype(BF16))

        o = o + jnp.sum(q * kk, axis=-1, keepdims=True) * v
        for d in range(1, sub):
            gs = pltpu.roll(g, d, 0)
            ks = pltpu.roll(kk, d, 0)
            vs = pltpu.roll(v, d, 0)
            dec = jnp.exp(jnp.where(row_in_sub >= d, g - gs, neg))
            o = o + jnp.sum(q * ks * dec, axis=-1, keepdims=True) * vs

        k_dec = kk * jnp.exp(g_last - g)
        st_sc[...] = st * jnp.exp(g_last) + _dot_tn(v.astype(BF16), k_dec.astype(BF16))

        o = o * lax.rsqrt(jnp.mean(o * o, axis=-1, keepdims=True) + RMS_EPS) * gain
        y_ref[pl.ds(r0, c), :] = o * _silu(g_ref[pl.ds(r0, c), :])
        return carry

    lax.fori_loop(0, n_chunks, chunk, 0)

    @pl.when(t == pl.num_programs(2) - 1)
    def _():
        sfin_ref[0, 0] = st_sc[...].T


def _hgrn_prompt(z, lb, gain, bt, t):
    hd = HG_HEAD_DIM
    n_heads = lb.shape[0] // hd
    tb = min(512, t)
    nt = t // tb
    m = bt * t

    def col(block):
        return pl.BlockSpec((tb, hd), lambda b, h, s: (b * nt + s, block * n_heads + h))

    return pl.pallas_call(
        functools.partial(_hgrn_prompt_kernel, n_chunks=tb // HG_CHUNK),
        grid=(bt, n_heads, nt),
        in_specs=[col(1), col(2), col(3), col(4),
                  pl.BlockSpec((1, hd), lambda b, h, s: (0, h)),
                  pl.BlockSpec((1, hd), lambda b, h, s: (0, 0))],
        out_specs=[pl.BlockSpec((tb, hd), lambda b, h, s: (b * nt + s, h)),
                   pl.BlockSpec((1, 1, hd, hd), lambda b, h, s: (b, h, 0, 0))],
        out_shape=[jax.ShapeDtypeStruct((m, n_heads * hd), F32),
                   jax.ShapeDtypeStruct((bt, n_heads, hd, hd), F32)],
        scratch_shapes=[pltpu.VMEM((hd, hd), F32)],
        compiler_params=_params(("parallel", "parallel", "arbitrary")),
        name="hgrn_prompt",
    )(z, z, z, z, lb.reshape(1, -1), gain.reshape(1, hd))


HG_SAMPLE_KEYS = 32


def _hgrn_sample_kernel(q_ref, f_ref, i_ref, g_ref, lb_ref, gain_ref, s_ref, y_ref, snew_ref,
                        qt_sc, ft_sc, kt_sc, vt_sc, o_sc):
    kb = pl.program_id(1)

    @pl.when(kb == 0)
    def _():
        lb = lb_ref[...]
        f = lb + (1.0 - lb) * _sigmoid(f_ref[...])
        qt_sc[...] = _silu(q_ref[...]).T
        ft_sc[...] = f.T
        kt_sc[...] = (1.0 - f).T
        vt_sc[...] = i_ref[...].T
        o_sc[...] = jnp.zeros_like(o_sc)

    vt = vt_sc[...]
    acc = o_sc[...]
    for kl in range(HG_SAMPLE_KEYS):
        k = kb * HG_SAMPLE_KEYS + kl
        st = s_ref[:, 0, kl, :].T
        s_new = st * ft_sc[pl.ds(k, 1), :] + vt * kt_sc[pl.ds(k, 1), :]
        acc = acc + s_new * qt_sc[pl.ds(k, 1), :]
        snew_ref[:, 0, kl, :] = s_new.T
    o_sc[...] = acc

    @pl.when(kb == pl.num_programs(1) - 1)
    def _():
        o = acc.T
        o = o * lax.rsqrt(jnp.mean(o * o, axis=-1, keepdims=True) + RMS_EPS) * gain_ref[...]
        y_ref[...] = o * _silu(g_ref[...])


def _hgrn_sample(z, states, layer, lb, gain):
    _, b, n_heads, hd, _ = states.shape
    kbs = HG_SAMPLE_KEYS

    def col(block):
        return pl.BlockSpec((b, hd), lambda h, k: (0, block * n_heads + h))

    sblk = pl.BlockSpec((b, 1, kbs, hd), lambda h, k: (0, h, k, 0))
    return pl.pallas_call(
        _hgrn_sample_kernel,
        grid=(n_heads, hd // kbs),
        in_specs=[col(1), col(2), col(3), col(4),
                  pl.BlockSpec((1, hd), lambda h, k: (0, h)),
                  pl.BlockSpec((1, hd), lambda h, k: (0, 0)),
                  pl.BlockSpec((None, b, 1, kbs, hd), lambda h, k: (layer, 0, h, k, 0))],
        out_specs=[pl.BlockSpec((b, hd), lambda h, k: (0, h)), sblk],
        out_shape=[jax.ShapeDtypeStruct((b, n_heads * hd), F32),
                   jax.ShapeDtypeStruct(states.shape[1:], F32)],
        scratch_shapes=[pltpu.VMEM((hd, b), F32)] * 5,
        compiler_params=_params(("parallel", "arbitrary")),
        name="hgrn_sample",
    )(z, z, z, z, lb.reshape(1, -1), gain.reshape(1, hd), states)


def _mix_to_hidden(x, z, y_a_pre, y_b, lw, dense, alpha, tm):
    y_a = _glu(y_a_pre, dense["w_glu"], tm)
    merged = _merge(y_a, y_b, z, dense["w_ba"], dense["w_bb"], tm)
    return _ln1(merged, x, dense["w_o"], lw["ln1_g"], lw["ln1_b"], alpha, min(tm, 256))


def _layer_prompt(x, p, lw, alpha, bt, t):
    m = bt * t
    tm = min(512, m)
    dense = lw["dense_bf16"]
    z = _inproj(x, dense["w_in"], tm)
    y_a_pre, f_re, f_im = _s5_prompt(z[:, :lw["s5_width"]], lw["s5_prompt_ops"], bt, t)
    y_b, hg_new = _hgrn_prompt(z, lw["lb"], lw["gn_gain"], bt, t)
    h = _mix_to_hidden(x, z, y_a_pre, y_b, lw, dense, alpha, tm)
    s1, s2, cw, counts = _router_sorted(h, lw["w_router"], lw["b_router"], tm)
    pos8, tab = _plan(s1, s2, counts, tm)
    pos = pos8[:2].reshape(2 * m)
    n_tiles = _moe_num_tiles(m)
    src = _invert(pos, m)
    ys = _experts_sorted(h, tab[0, :n_tiles], tab[1, :n_tiles], src, lw["w_g"], lw["w_u"], lw["w_d"])
    x_new = _final_gather(h, cw, p, pos, ys, dense["w_pg"], dense["w_pp"], lw["ln2_g"], lw["ln2_b"], alpha,
                          min(tm, 256))
    return x_new, f_re, f_im, hg_new


def _layer_sample(x, p, s_re, s_im, hg_states, layer, lw, alpha):
    m = x.shape[0]
    dense = lw["dense_f32"]
    z = _inproj(x, dense["w_in"], m)
    y_a_pre, n_re, n_im = _s5_sample(z, s_re, s_im, lw["s5_sample_ops"])
    y_b, hg_new = _hgrn_sample(z, hg_states, layer, lw["lb"], lw["gn_gain"])
    h = _mix_to_hidden(x, z, y_a_pre, y_b, lw, dense, alpha, m)
    comb = _router_dense(h, lw["w_router"], lw["b_router"], m)
    moe = _moe_dense(h, comb, lw["w_g"], lw["w_u"], lw["w_d"], m)
    x_new = _final(h, moe, p, dense["w_pg"], dense["w_pp"], lw["ln2_g"], lw["ln2_b"], alpha, m)
    return x_new, n_re, n_im, hg_new


def _layer_weights(i, lb_all, w_in, s5_lambda_re, s5_lambda_im, s5_log_dt, s5_b_re, s5_b_im, s5_c_re,
                   s5_c_im, s5_d, s5_w_glu, hg_norm_gain, w_branch_a, w_branch_b, w_out, ln1_gain,
                   ln1_bias, w_group_router, b_group_router, w_expert_router, b_expert_router,
                   w_exp_gate, w_exp_up, w_exp_down, w_ple_proj, w_ple_gate, ln2_gain, ln2_bias):
    disc = _s5_discretise(s5_lambda_re[i], s5_lambda_im[i], s5_log_dt[i], s5_b_re[i], s5_b_im[i])
    d_model = w_in.shape[1]
    pad = LANES - N_EXPERTS - N_GROUPS
    w_router = jnp.concatenate([w_expert_router[i], w_group_router[i], jnp.zeros((d_model, pad), F32)], axis=1)
    b_router = jnp.concatenate([b_expert_router[i], b_group_router[i], jnp.zeros((pad,), F32)]).reshape(1, LANES)
    dense_f32 = {"w_in": w_in[i], "w_glu": s5_w_glu[i], "w_ba": w_branch_a[i], "w_bb": w_branch_b[i],
                 "w_o": w_out[i], "w_pp": w_ple_proj[i], "w_pg": w_ple_gate[i]}
    return {
        "dense_f32": dense_f32,
        "dense_bf16": {k: v.astype(BF16) for k, v in dense_f32.items()},
        "s5_width": s5_d.shape[1],
        "s5_prompt_ops": _s5_prompt_operators(*disc, s5_c_re[i], s5_c_im[i], s5_d[i]),
        "s5_sample_ops": _s5_sample_operators(*disc, s5_c_re[i], s5_c_im[i], s5_d[i]),
        "lb": lb_all[i],
        "gn_gain": hg_norm_gain[i],
        "ln1_g": ln1_gain[i], "ln1_b": ln1_bias[i],
        "w_router": w_router, "b_router": b_router,
        "w_g": w_exp_gate[i],
        "w_u": w_exp_up[i],
        "w_d": w_exp_down[i],
        "ln2_g": ln2_gain[i], "ln2_b": ln2_bias[i],
    }


def kernel(x_prompt, x_sample, p_prompt, p_sample, state_s5_re, state_s5_im, state_hgrn, w_in, s5_lambda_re, s5_lambda_im, s5_log_dt, s5_b_re, s5_b_im, s5_c_re, s5_c_im, s5_d, s5_w_glu, hg_lower_bounds, hg_norm_gain, w_branch_a, w_branch_b, w_out, ln1_gain, ln1_bias, w_group_router, b_group_router, w_expert_router, b_expert_router, w_exp_gate, w_exp_up, w_exp_down, w_ple_proj, w_ple_gate, ln2_gain, ln2_bias):
    depth = w_in.shape[0]
    bt, t, d_model = x_prompt.shape
    bs = x_sample.shape[0]
    alpha = (2 * depth) ** 0.25
    n_groups, n_state = s5_lambda_re.shape[1:]

    lb_soft = jax.nn.softmax(hg_lower_bounds.astype(F32), axis=0)
    lb_all = jnp.cumsum(lb_soft, axis=0) - lb_soft[0]

    xp = x_prompt.reshape(bt * t, d_model)
    xs = x_sample.reshape(bs, d_model)
    outs = [[] for _ in range(6)]
    for i in range(depth):
        lw = _layer_weights(i, lb_all, w_in, s5_lambda_re, s5_lambda_im, s5_log_dt, s5_b_re, s5_b_im,
                            s5_c_re, s5_c_im, s5_d, s5_w_glu, hg_norm_gain, w_branch_a, w_branch_b,
                            w_out, ln1_gain, ln1_bias, w_group_router, b_group_router, w_expert_router,
                            b_expert_router, w_exp_gate, w_exp_up, w_exp_down, w_ple_proj, w_ple_gate,
                            ln2_gain, ln2_bias)
        xp, a_re, a_im, a_hg = _layer_prompt(xp, p_prompt[i].reshape(bt * t, -1), lw, alpha, bt, t)
        xs, b_re, b_im, b_hg = _layer_sample(
            xs, p_sample[i].reshape(bs, -1),
            state_s5_re[i].reshape(bs, n_groups * n_state), state_s5_im[i].reshape(bs, n_groups * n_state),
            state_hgrn, i, lw, alpha)
        for lst, val in zip(outs, (a_re.reshape(bt, n_groups, n_state), a_im.reshape(bt, n_groups, n_state), a_hg,
                                   b_re.reshape(bs, n_groups, n_state), b_im.reshape(bs, n_groups, n_state), b_hg)):
            lst.append(val)

    return (xp.reshape(bt, t, d_model), xs.reshape(bs, 1, d_model), *[jnp.stack(o) for o in outs])
```

```python
import functools
import math

import jax
import jax.numpy as jnp
from jax import lax
from jax.experimental import pallas as pl
from jax.experimental.pallas import tpu as pltpu

F32 = jnp.float32
BF16 = jnp.bfloat16
HIGHEST = lax.Precision.HIGHEST

LANES = 128
SUBLANES = 8
VMEM_LIMIT_BYTES = 56 * 1024 * 1024

S5_GROUP = 16
S5_STATE = 64
S5_CHUNK = 16
HG_HEAD_DIM = 128
HG_CHUNK = 64
HG_SUB = 16
N_GROUPS = 4
EXPERTS_PER_GROUP = 8
N_EXPERTS = N_GROUPS * EXPERTS_PER_GROUP
LN_EPS = 1e-5
RMS_EPS = 1e-6


def _params(semantics):
    return pltpu.CompilerParams(dimension_semantics=semantics, vmem_limit_bytes=VMEM_LIMIT_BYTES)


def _dot(a, b):
    return jnp.dot(a, b, preferred_element_type=F32)


def _mm(a, w):
    if w.dtype == F32:
        return jnp.dot(a, w, precision=HIGHEST, preferred_element_type=F32)
    return jnp.dot(a.astype(BF16), w, preferred_element_type=F32)


def _dot_nt(a, b):
    return lax.dot_general(a, b, (((1,), (1,)), ((), ())), preferred_element_type=F32)


def _dot_tn(a, b):
    return lax.dot_general(a, b, (((0,), (0,)), ((), ())), preferred_element_type=F32)


def _sigmoid(x):
    return jax.nn.sigmoid(x)


def _silu(x):
    return x * jax.nn.sigmoid(x)


def _layer_norm_rows(x, gain, bias):
    mu = jnp.mean(x, axis=-1, keepdims=True)
    xc = x - mu
    var = jnp.mean(xc * xc, axis=-1, keepdims=True)
    return xc * lax.rsqrt(var + LN_EPS) * gain + bias


def _inproj_kernel(x_ref, w_ref, o_ref, xb_ref):
    if w_ref.dtype == F32:
        o_ref[...] = _mm(x_ref[...], w_ref[...])
        return

    @pl.when(pl.program_id(1) == 0)
    def _():
        xb_ref[...] = x_ref[...].astype(BF16)

    o_ref[...] = _dot(xb_ref[...], w_ref[...])


def _layer_spec(block, layer, index_map):
    return pl.BlockSpec((None,) + block, lambda *idx: (layer,) + index_map(*idx))


def _inproj(x, w, layer, tm, tn=1024):
    m, k = x.shape
    n = w.shape[-1]
    return pl.pallas_call(
        _inproj_kernel,
        grid=(m // tm, n // tn),
        in_specs=[pl.BlockSpec((tm, k), lambda i, j: (i, 0)),
                  _layer_spec((k, tn), layer, lambda i, j: (0, j))],
        out_specs=pl.BlockSpec((tm, tn), lambda i, j: (i, j)),
        out_shape=jax.ShapeDtypeStruct((m, n), F32),
        scratch_shapes=[pltpu.VMEM((tm, k), BF16)],
        compiler_params=_params(("parallel", "arbitrary")),
        name="inproj",
    )(x, w)


def _glu_kernel(a_ref, w_ref, o_ref):
    a = a_ref[...]
    o_ref[...] = a * _sigmoid(_mm(a, w_ref[...]))


def _glu(a, w, layer, tm):
    m, k = a.shape
    return pl.pallas_call(
        _glu_kernel,
        grid=(m // tm,),
        in_specs=[pl.BlockSpec((tm, k), lambda i: (i, 0)),
                  _layer_spec((k, k), layer, lambda i: (0, 0))],
        out_specs=pl.BlockSpec((tm, k), lambda i: (i, 0)),
        out_shape=jax.ShapeDtypeStruct((m, k), F32),
        compiler_params=_params(("parallel",)),
        name="s5_glu",
    )(a, w)


def _merge_kernel(ya_ref, yb_ref, ga_ref, gb_ref, wa_ref, wb_ref, o_ref):
    pa = _mm(ya_ref[...], wa_ref[...])
    pb = _mm(yb_ref[...], wb_ref[...])
    o_ref[...] = _sigmoid(ga_ref[...]) * pa + _sigmoid(gb_ref[...]) * pb


def _merge(ya, yb, z, wa, wb, layer, tm, tn=1024):
    m, k = ya.shape
    n = wa.shape[-1]
    nj = n // tn
    return pl.pallas_call(
        _merge_kernel,
        grid=(m // tm, nj),
        in_specs=[pl.BlockSpec((tm, k), lambda i, j: (i, 0)),
                  pl.BlockSpec((tm, k), lambda i, j: (i, 0)),
                  pl.BlockSpec((tm, tn), lambda i, j: (i, 5 + j)),
                  pl.BlockSpec((tm, tn), lambda i, j: (i, 5 + nj + j)),
                  _layer_spec((k, tn), layer, lambda i, j: (0, j)),
                  _layer_spec((k, tn), layer, lambda i, j: (0, j))],
        out_specs=pl.BlockSpec((tm, tn), lambda i, j: (i, j)),
        out_shape=jax.ShapeDtypeStruct((m, n), F32),
        compiler_params=_params(("parallel", "arbitrary")),
        name="branch_merge",
    )(ya, yb, z, z, wa, wb)


def _ln1_kernel(mg_ref, x_ref, w_ref, g_ref, b_ref, o_ref, *, alpha):
    r = alpha * x_ref[...] + _mm(mg_ref[...], w_ref[...])
    o_ref[...] = _layer_norm_rows(r, g_ref[...], b_ref[...])


def _ln1(merged, x, w, gain, bias, layer, alpha, tm):
    m, d = x.shape
    row = pl.BlockSpec((tm, d), lambda i: (i, 0))
    vec = _layer_spec((1, d), layer, lambda i: (0, 0))
    return pl.pallas_call(
        functools.partial(_ln1_kernel, alpha=alpha),
        grid=(m // tm,),
        in_specs=[row, row, _layer_spec((d, d), layer, lambda i: (0, 0)), vec, vec],
        out_specs=row,
        out_shape=jax.ShapeDtypeStruct((m, d), F32),
        compiler_params=_params(("parallel",)),
        name="outproj_ln1",
    )(merged, x, w, gain[:, None, :], bias[:, None, :])


def _final_kernel(h_ref, moe_ref, p_ref, wg_ref, wp_ref, g_ref, b_ref, o_ref, *, alpha):
    h = h_ref[...]
    gate = _sigmoid(_mm(h, wg_ref[...]))
    proj = _mm(p_ref[...], wp_ref[...])
    r = alpha * h + moe_ref[...] + gate * proj
    o_ref[...] = _layer_norm_rows(r, g_ref[...], b_ref[...])


def _final(h, moe, p, wg, wp, gain, bias, layer, alpha, tm):
    m, d = h.shape
    dp = p.shape[-1]
    row = pl.BlockSpec((tm, d), lambda i: (i, 0))
    vec = _layer_spec((1, d), layer, lambda i: (0, 0))
    return pl.pallas_call(
        functools.partial(_final_kernel, alpha=alpha),
        grid=(m // tm,),
        in_specs=[row, row, _layer_spec((tm, dp), layer, lambda i: (i, 0)),
                  _layer_spec((d, d), layer, lambda i: (0, 0)),
                  _layer_spec((dp, d), layer, lambda i: (0, 0)), vec, vec],
        out_specs=row,
        out_shape=jax.ShapeDtypeStruct((m, d), F32),
        compiler_params=_params(("parallel",)),
        name="ple_ln2",
    )(h, moe, p, wg, wp, gain[:, None, :], bias[:, None, :])


def _route(h, w, b):
    logits = jnp.dot(h, w, precision=HIGHEST, preferred_element_type=F32) + b
    lane = lax.broadcasted_iota(jnp.int32, logits.shape, 1)
    lane_f = lane.astype(F32)
    neg = -jnp.inf
    gl = jnp.where(lane >= N_EXPERTS, jnp.where(lane < N_EXPERTS + N_GROUPS, logits, neg), neg)
    gmax = jnp.max(gl, axis=-1, keepdims=True)
    denom = jnp.sum(jnp.exp(gl - gmax), axis=-1, keepdims=True)
    grp_p = 1.0 / denom
    gidx = jnp.min(jnp.where(gl == gmax, lane_f, 1e9), axis=-1, keepdims=True) - N_EXPERTS
    lane_grp = (lane // EXPERTS_PER_GROUP).astype(F32)
    el = jnp.where(lane < N_EXPERTS, jnp.where(lane_grp == gidx, logits, neg), neg)
    t1 = jnp.max(el, axis=-1, keepdims=True)
    i1 = jnp.min(jnp.where(el == t1, lane_f, 1e9), axis=-1, keepdims=True)
    el2 = jnp.where(lane_f == i1, neg, el)
    t2 = jnp.max(el2, axis=-1, keepdims=True)
    i2 = jnp.min(jnp.where(el2 == t2, lane_f, 1e9), axis=-1, keepdims=True)
    e2 = jnp.exp(t2 - t1)
    w1 = 1.0 / (1.0 + e2)
    w2 = e2 * w1
    return lane, lane_f, i1, i2, grp_p * w1, grp_p * w2


def _router_dense_kernel(h_ref, w_ref, b_ref, o_ref):
    _, lane_f, i1, i2, c1, c2 = _route(h_ref[...], w_ref[...], b_ref[...])
    o_ref[...] = jnp.where(lane_f == i1, c1, jnp.where(lane_f == i2, c2, 0.0))


def _router_dense(h, w_r, b_r, tm):
    m, d = h.shape
    return pl.pallas_call(
        _router_dense_kernel,
        grid=(m // tm,),
        in_specs=[pl.BlockSpec((tm, d), lambda i: (i, 0)),
                  pl.BlockSpec((d, LANES), lambda i: (0, 0)),
                  pl.BlockSpec((1, LANES), lambda i: (0, 0))],
        out_specs=pl.BlockSpec((tm, LANES), lambda i: (i, 0)),
        out_shape=jax.ShapeDtypeStruct((m, LANES), F32),
        compiler_params=_params(("parallel",)),
        name="router_dense",
    )(h, w_r, b_r)


def _moe_dense_kernel(h_ref, c_ref, wg_ref, wu_ref, wd_ref, o_ref):
    e = pl.program_id(1)

    @pl.when(e == 0)
    def _():
        o_ref[...] = jnp.zeros_like(o_ref)

    comb = c_ref[...]
    lane = lax.broadcasted_iota(jnp.int32, comb.shape, 1)
    col = jnp.sum(jnp.where(lane == e, comb, 0.0), axis=-1, keepdims=True)
    h = h_ref[...]
    hid = _silu(_mm(h, wg_ref[0])) * _mm(h, wu_ref[0]) * col
    o_ref[...] += _mm(hid, wd_ref[0])


def _moe_dense(h, comb, wg, wu, wd, layer, tm):
    m, d = h.shape
    _, n_e, _, f = wg.shape
    return pl.pallas_call(
        _moe_dense_kernel,
        grid=(m // tm, n_e),
        in_specs=[pl.BlockSpec((tm, d), lambda i, e: (i, 0)),
                  pl.BlockSpec((tm, LANES), lambda i, e: (i, 0)),
                  _layer_spec((1, d, f), layer, lambda i, e: (e, 0, 0)),
                  _layer_spec((1, d, f), layer, lambda i, e: (e, 0, 0)),
                  _layer_spec((1, f, d), layer, lambda i, e: (e, 0, 0))],
        out_specs=pl.BlockSpec((tm, d), lambda i, e: (i, 0)),
        out_shape=jax.ShapeDtypeStruct((m, d), F32),
        compiler_params=_params(("parallel", "arbitrary")),
        name="moe_dense",
    )(h, comb, wg, wu, wd)


MOE_TILE = 256


def _moe_num_tiles(m):
    return 2 * m // MOE_TILE + N_EXPERTS


def _router_sorted_kernel(h_ref, w_ref, b_ref, s1_ref, s2_ref, cw_ref, cnt_ref, carry_sc):
    @pl.when(pl.program_id(0) == 0)
    def _():
        carry_sc[...] = jnp.zeros_like(carry_sc)

    lane, lane_f, i1, i2, c1, c2 = _route(h_ref[...], w_ref[...], b_ref[...])
    sel1 = jnp.where(lane_f == i1, 1.0, 0.0)
    sel2 = jnp.where(lane_f == i2, 1.0, 0.0)
    cnt = sel1 + sel2
    tm = cnt.shape[0]
    strict_lower = (lax.broadcasted_iota(jnp.int32, (tm, tm), 0) > lax.broadcasted_iota(jnp.int32, (tm, tm), 1))
    prefix = _dot(strict_lower.astype(BF16), cnt.astype(BF16)) + carry_sc[0:1, :]
    s1_ref[...] = sel1 * (prefix + 1.0)
    s2_ref[...] = sel2 * (prefix + 1.0)
    cw_ref[...] = jnp.where(lane == 0, c1, jnp.where(lane == 1, c2, 0.0))
    carry_sc[...] = carry_sc[...] + jnp.sum(cnt, axis=0, keepdims=True)
    cnt_ref[...] = carry_sc[...]


def _router_sorted(h, w_r, b_r, tm):
    m, d = h.shape
    tok = pl.BlockSpec((tm, LANES), lambda i: (i, 0))
    return pl.pallas_call(
        _router_sorted_kernel,
        grid=(m // tm,),
        in_specs=[pl.BlockSpec((tm, d), lambda i: (i, 0)),
                  pl.BlockSpec((d, LANES), lambda i: (0, 0)),
                  pl.BlockSpec((1, LANES), lambda i: (0, 0))],
        out_specs=[tok, tok, tok, pl.BlockSpec((SUBLANES, LANES), lambda i: (0, 0))],
        out_shape=[jax.ShapeDtypeStruct((m, LANES), F32)] * 3 + [jax.ShapeDtypeStruct((SUBLANES, LANES), F32)],
        scratch_shapes=[pltpu.VMEM((SUBLANES, LANES), F32)],
        compiler_params=_params(("arbitrary",)),
        name="router_sorted",
    )(h, w_r, b_r)


def _plan_kernel(s1_ref, s2_ref, cnt_ref, pos_ref, tab_ref):
    te = float(MOE_TILE)
    lane = lax.broadcasted_iota(jnp.int32, (1, LANES), 1)
    cnt = jnp.where(lane < N_EXPERTS, cnt_ref[0:1, :], 0.0)
    padded = jnp.floor((cnt + (te - 1.0)) * (1.0 / te)) * te
    r128 = lax.broadcasted_iota(jnp.int32, (LANES, LANES), 0)
    c128 = lax.broadcasted_iota(jnp.int32, (LANES, LANES), 1)
    before = jnp.where(r128 < c128, 1.0, 0.0)
    off = jnp.dot(jnp.broadcast_to(padded, (SUBLANES, LANES)), before, precision=HIGHEST,
                  preferred_element_type=F32)[0:1, :]
    s1 = s1_ref[...]
    s2 = s2_ref[...]
    v1 = jnp.where(s1 > 0.0, s1 - 1.0 + off, 0.0)
    v2 = jnp.where(s2 > 0.0, s2 - 1.0 + off, 0.0)
    ones8 = jnp.ones((SUBLANES, LANES), F32)
    p1 = lax.dot_general(ones8, v1, (((1,), (1,)), ((), ())), precision=HIGHEST, preferred_element_type=F32)
    p2 = lax.dot_general(ones8, v2, (((1,), (1,)), ((), ())), precision=HIGHEST, preferred_element_type=F32)
    row8 = lax.broadcasted_iota(jnp.int32, p1.shape, 0)
    pos_ref[...] = jnp.where(row8 == 0, p1, p2).astype(jnp.int32)

    def per_expert(row):
        return jnp.broadcast_to(row, (LANES, LANES)).T
    off_e = per_expert(off)
    end_e = per_expert(off + padded)
    cnt_e = per_expert(cnt)
    start = c128.astype(F32) * te
    is_e = r128 < N_EXPERTS
    tile_e = jnp.sum(jnp.where(is_e, jnp.where(end_e <= start, 1.0, 0.0), 0.0), axis=0, keepdims=True)
    rows = jnp.clip(cnt_e - (start - off_e), 0.0, te)
    owns = jnp.where(is_e, jnp.where(off_e <= start, jnp.where(start < end_e, rows, 0.0), 0.0), 0.0)
    n_rows = jnp.sum(owns, axis=0, keepdims=True)
    tile_e = jnp.minimum(tile_e, float(N_EXPERTS - 1))
    n_used = jnp.sum(jnp.where(n_rows > 0.0, 1.0, 0.0), axis=-1, keepdims=True)
    row_t = lax.broadcasted_iota(jnp.int32, (SUBLANES, LANES), 0)
    tab_ref[...] = jnp.where(row_t == 0, tile_e, jnp.where(row_t == 1, n_rows, n_used)).astype(jnp.int32)


def _plan(s1, s2, counts, tm):
    m = s1.shape[0]
    tok = pl.BlockSpec((tm, LANES), lambda i: (i, 0))
    return pl.pallas_call(
        _plan_kernel,
        grid=(m // tm,),
        in_specs=[tok, tok, pl.BlockSpec((SUBLANES, LANES), lambda i: (0, 0))],
        out_specs=[pl.BlockSpec((SUBLANES, tm), lambda i: (0, i)),
                   pl.BlockSpec((SUBLANES, LANES), lambda i: (0, 0))],
        out_shape=[jax.ShapeDtypeStruct((SUBLANES, m), jnp.int32),
                   jax.ShapeDtypeStruct((SUBLANES, LANES), jnp.int32)],
        compiler_params=_params(("arbitrary",)),
        name="moe_plan",
    )(s1, s2, counts)


DISPATCH_CHUNK = 256
DMA_UNROLL = 8


def _dispatch_kernel(pos_ref, rows_ref, h_hbm, xs_hbm, zero_buf, zero_sem, sem, *, m, n_tiles):
    zero_buf[...] = jnp.zeros_like(zero_buf)

    def zero_copy(j):
        return pltpu.make_async_copy(zero_buf, xs_hbm.at[pl.ds(j * MOE_TILE, MOE_TILE)], zero_sem)

    def partial_tile(j):
        return rows_ref[j] < MOE_TILE

    def zero_start(j, c):
        @pl.when(partial_tile(j))
        def _():
            zero_copy(j).start()
        return c

    def zero_wait(j, c):
        @pl.when(partial_tile(j))
        def _():
            zero_copy(j).wait()
        return c

    lax.fori_loop(0, n_tiles, zero_start, 0)
    lax.fori_loop(0, n_tiles, zero_wait, 0)

    n_chunks = m // DISPATCH_CHUNK

    def chunk_wait(c):
        pltpu.make_async_copy(h_hbm.at[pl.ds(0, 2 * DISPATCH_CHUNK)], xs_hbm.at[pl.ds(0, 2 * DISPATCH_CHUNK)],
                              sem.at[c % 2]).wait()

    def chunk(c, carry):
        def body(r, cc):
            t = c * DISPATCH_CHUNK + r
            src = h_hbm.at[pl.ds(t, 1)]
            pltpu.make_async_copy(src, xs_hbm.at[pl.ds(pos_ref[t], 1)], sem.at[c % 2]).start()
            pltpu.make_async_copy(src, xs_hbm.at[pl.ds(pos_ref[m + t], 1)], sem.at[c % 2]).start()
            return cc
        lax.fori_loop(0, DISPATCH_CHUNK, body, 0, unroll=DMA_UNROLL)

        @pl.when(c > 0)
        def _():
            chunk_wait(c - 1)
        return carry

    lax.fori_loop(0, n_chunks, chunk, 0)
    chunk_wait(n_chunks - 1)


def _dispatch(h, pos, tile_rows):
    m, d = h.shape
    n_tiles = _moe_num_tiles(m)
    grid_spec = pltpu.PrefetchScalarGridSpec(
        num_scalar_prefetch=2,
        grid=(1,),
        in_specs=[pl.BlockSpec(memory_space=pl.ANY)],
        out_specs=pl.BlockSpec(memory_space=pl.ANY),
        scratch_shapes=[pltpu.VMEM((MOE_TILE, d), F32), pltpu.SemaphoreType.DMA(()),
                        pltpu.SemaphoreType.DMA((2,))],
    )
    return pl.pallas_call(
        functools.partial(_dispatch_kernel, m=m, n_tiles=n_tiles),
        grid_spec=grid_spec,
        out_shape=jax.ShapeDtypeStruct((n_tiles * MOE_TILE, d), F32),
        compiler_params=_params(("arbitrary",)),
        name="moe_dispatch",
    )(pos, tile_rows, h)


def _experts_kernel(te_ref, nr_ref, nu_ref, x_ref, wg_ref, wu_ref, wd_ref, y_ref, wgb, wub, wdb):
    j = pl.program_id(0)
    new_expert = jnp.logical_or(j == 0, te_ref[j] != te_ref[jnp.maximum(j - 1, 0)])

    @pl.when(jnp.logical_and(new_expert, nr_ref[j] > 0))
    def _():
        wgb[...] = wg_ref[0].astype(BF16)
        wub[...] = wu_ref[0].astype(BF16)
        wdb[...] = wd_ref[0].astype(BF16)

    @pl.when(nr_ref[j] > 0)
    def _():
        x = x_ref[...].astype(BF16)
        hid = _silu(_dot(x, wgb[...])) * _dot(x, wub[...])
        y_ref[...] = _dot(hid.astype(BF16), wdb[...])

    @pl.when(nr_ref[j] == 0)
    def _():
        y_ref[...] = jnp.zeros_like(y_ref)


def _experts_sorted(xs, tile_expert, tile_rows, n_used, wg, wu, wd, layer):
    n_rows, d = xs.shape
    _, n_e, _, f = wg.shape
    n_tiles = n_rows // MOE_TILE

    def used_tile(j, te, nr, nu):
        return (jnp.minimum(j, nu[0] - 1), 0)

    def expert_block(j, te, nr, nu):
        return (layer, te[j], 0, 0)

    grid_spec = pltpu.PrefetchScalarGridSpec(
        num_scalar_prefetch=3,
        grid=(n_tiles,),
        in_specs=[pl.BlockSpec((MOE_TILE, d), used_tile),
                  pl.BlockSpec((None, 1, d, f), expert_block),
                  pl.BlockSpec((None, 1, d, f), expert_block),
                  pl.BlockSpec((None, 1, f, d), expert_block)],
        out_specs=pl.BlockSpec((MOE_TILE, d), lambda j, te, nr, nu: (j, 0)),
        scratch_shapes=[pltpu.VMEM((d, f), BF16), pltpu.VMEM((d, f), BF16), pltpu.VMEM((f, d), BF16)],
    )
    return pl.pallas_call(
        _experts_kernel,
        grid_spec=grid_spec,
        out_shape=jax.ShapeDtypeStruct((n_rows, d), F32),
        compiler_params=_params(("arbitrary",)),
        name="moe_experts_sorted",
    )(tile_expert, tile_rows, n_used, xs, wg, wu, wd)


def _final_gather_kernel(pos_ref, h_ref, cw_ref, p_ref, wg_ref, wp_ref, g_ref, b_ref, ys_hbm, o_ref,
                         ybuf, sem, *, alpha, m):
    i = pl.program_id(0)
    slot = i % 2
    tm = h_ref.shape[0]

    def gather(tile, dst_slot):
        def body(r, c):
            t = tile * tm + r
            for pick in range(2):
                pltpu.make_async_copy(ys_hbm.at[pl.ds(pos_ref[pick * m + t], 1)],
                                      ybuf.at[dst_slot, pl.ds(pick * tm + r, 1)], sem.at[dst_slot]).start()
            return c
        lax.fori_loop(0, tm, body, 0, unroll=DMA_UNROLL)

    @pl.when(i == 0)
    def _():
        gather(0, 0)

    @pl.when(i + 1 < pl.num_programs(0))
    def _():
        gather(i + 1, 1 - slot)

    h = h_ref[...]
    gate = _sigmoid(_mm(h, wg_ref[...]))
    proj = _mm(p_ref[...], wp_ref[...])

    pltpu.make_async_copy(ys_hbm.at[pl.ds(0, 2 * tm)], ybuf.at[slot], sem.at[slot]).wait()

    cw = cw_ref[...]
    moe = cw[:, 0:1] * ybuf[slot, 0:tm] + cw[:, 1:2] * ybuf[slot, tm:2 * tm]
    r = alpha * h + moe + gate * proj
    o_ref[...] = _layer_norm_rows(r, g_ref[...], b_ref[...])


def _final_gather(h, cw, p, pos, ys, wg, wp, gain, bias, layer, alpha, tm):
    m, d = h.shape
    dp = p.shape[-1]
    row = pl.BlockSpec((tm, d), lambda i, pos: (i, 0))
    vec = _layer_spec((1, d), layer, lambda i, pos: (0, 0))
    grid_spec = pltpu.PrefetchScalarGridSpec(
        num_scalar_prefetch=1,
        grid=(m // tm,),
        in_specs=[row, pl.BlockSpec((tm, LANES), lambda i, pos: (i, 0)),
                  _layer_spec((tm, dp), layer, lambda i, pos: (i, 0)),
                  _layer_spec((d, d), layer, lambda i, pos: (0, 0)),
                  _layer_spec((dp, d), layer, lambda i, pos: (0, 0)), vec, vec,
                  pl.BlockSpec(memory_space=pl.ANY)],
        out_specs=row,
        scratch_shapes=[pltpu.VMEM((2, 2 * tm, d), F32), pltpu.SemaphoreType.DMA((2,))],
    )
    return pl.pallas_call(
        functools.partial(_final_gather_kernel, alpha=alpha, m=m),
        grid_spec=grid_spec,
        out_shape=jax.ShapeDtypeStruct((m, d), F32),
        compiler_params=_params(("arbitrary",)),
        name="ple_ln2_gather",
    )(pos, h, cw, p, wg, wp, gain[:, None, :], bias[:, None, :], ys)


def _s5_discretise(lam_re, lam_im, log_dt, b_re, b_im):
    dt = jnp.exp(log_dt)[:, None]
    mag = jnp.exp(lam_re * dt)
    ab_re = mag * jnp.cos(lam_im * dt)
    ab_im = mag * jnp.sin(lam_im * dt)
    den = lam_re * lam_re + lam_im * lam_im
    nr = ab_re - 1.0
    zr = (nr * lam_re + ab_im * lam_im) / den
    zi = (ab_im * lam_re - nr * lam_im) / den
    bb_re = zr[..., None] * b_re - zi[..., None] * b_im
    bb_im = zr[..., None] * b_im + zi[..., None] * b_re
    return ab_re, ab_im, bb_re, bb_im


def _s5_prompt_operators(ab_re, ab_im, bb_re, bb_im, c_re, c_im, d_skip):
    g, n = ab_re.shape
    cch = bb_re.shape[-1]
    L = S5_CHUNK
    pw_re = [jnp.ones_like(ab_re)]
    pw_im = [jnp.zeros_like(ab_im)]
    for _ in range(L):
        pr, pi = pw_re[-1], pw_im[-1]
        pw_re.append(pr * ab_re - pi * ab_im)
        pw_im.append(pr * ab_im + pi * ab_re)
    a_re = jnp.stack(pw_re)
    a_im = jnp.stack(pw_im)
    w_re = a_re[:L, :, :, None] * bb_re - a_im[:L, :, :, None] * bb_im
    w_im = a_re[:L, :, :, None] * bb_im + a_im[:L, :, :, None] * bb_re
    kern = (jnp.einsum('gcn,kgnd->kgcd', c_re, w_re, precision=HIGHEST)
            - jnp.einsum('gcn,kgnd->kgcd', c_im, w_im, precision=HIGHEST))
    zero_lags = jnp.zeros_like(kern)
    toe = jnp.stack([jnp.concatenate([zero_lags[:ti], kern[:L - ti]], axis=0) for ti in range(L)])
    toe = toe.transpose(2, 0, 4, 1, 3).reshape(g, L * cch, L * cch)

    p_re = w_re[::-1].transpose(1, 0, 3, 2).reshape(g, L * cch, n)
    p_im = w_im[::-1].transpose(1, 0, 3, 2).reshape(g, L * cch, n)
    q_re = (jnp.einsum('gcn,tgn->gntc', c_re, a_re[1:]) - jnp.einsum('gcn,tgn->gntc', c_im, a_im[1:]))
    q_im = -(jnp.einsum('gcn,tgn->gntc', c_re, a_im[1:]) + jnp.einsum('gcn,tgn->gntc', c_im, a_re[1:]))
    q_re = q_re.reshape(g, n, L * cch)
    q_im = q_im.reshape(g, n, L * cch)

    hp = g // 2
    w = L * cch
    eye2 = jnp.eye(2, dtype=F32)
    p_ri = jnp.stack([p_re, p_im], axis=2).reshape(hp, 2, w, 2, n)
    q_ri = jnp.stack([q_re, q_im], axis=1).reshape(hp, 2, 2, n, w)
    p2 = (p_ri[:, :, :, :, None, :] * eye2[None, :, None, None, :, None]).reshape(hp, 2 * w, 4 * n)
    q2 = (q_ri.transpose(0, 2, 1, 3, 4)[:, :, :, :, None, :]
          * eye2[None, None, :, None, :, None]).reshape(hp, 4 * n, 2 * w)
    a_chunk = jnp.stack([a_re[L].reshape(hp, 2 * n), a_im[L].reshape(hp, 2 * n)], axis=1)
    d2 = jnp.broadcast_to(d_skip.reshape(hp, 2, 1, cch), (hp, 2, L, cch)).reshape(hp, 1, 2 * w)
    return toe.astype(BF16), p2.astype(BF16), q2.astype(BF16), a_chunk, d2


def _s5_prompt_kernel(u_ref, t_ref, p_ref, q_ref, a_ref, d_ref, y_ref, fre_ref, fim_ref,
                      s_sc, xin_sc, *, n_chunks, bt):
    half = a_ref.shape[-1]
    u0 = u_ref[0]
    u1 = u_ref[1]
    ub0 = u0.astype(BF16)
    ub1 = u1.astype(BF16)
    y_intra = jnp.concatenate([_dot(ub0, t_ref[0]), _dot(ub1, t_ref[1])], axis=1)
    s_sc[...] = _dot(jnp.concatenate([ub0, ub1], axis=1), p_ref[0])
    ar = a_ref[0, 0:1, :]
    ai = a_ref[0, 1:2, :]
    xr = jnp.zeros((bt, half), F32)
    xi = jnp.zeros((bt, half), F32)
    for j in range(n_chunks):
        rows = slice(j * bt, (j + 1) * bt)
        xin_sc[rows, 0:half] = xr
        xin_sc[rows, half:2 * half] = xi
        sr = s_sc[rows, 0:half]
        si = s_sc[rows, half:2 * half]
        xr, xi = ar * xr - ai * xi + sr, ar * xi + ai * xr + si
    fre_ref[...] = xr
    fim_ref[...] = xi
    y_carry = _dot(xin_sc[...].astype(BF16), q_ref[0])
    y = y_intra + y_carry + d_ref[0] * jnp.concatenate([u0, u1], axis=1)
    y = jax.nn.gelu(y)
    w = u0.shape[1]
    y_ref[0] = y[:, :w]
    y_ref[1] = y[:, w:]


def _s5_prompt(u, ops, bt, t):
    toe, p2, q2, a_chunk, d2 = ops
    g = toe.shape[0]
    hp = g // 2
    L = S5_CHUNK
    cch = S5_GROUP
    n_chunks = t // L
    r = n_chunks * bt
    w = L * cch
    half = a_chunk.shape[-1]
    u_t = u.reshape(bt, n_chunks, L, g, cch).transpose(3, 1, 0, 2, 4).reshape(g, r, w)
    y_t, f_re, f_im = pl.pallas_call(
        functools.partial(_s5_prompt_kernel, n_chunks=n_chunks, bt=bt),
        grid=(hp,),
        in_specs=[pl.BlockSpec((2, r, w), lambda i: (i, 0, 0)),
                  pl.BlockSpec((2, w, w), lambda i: (i, 0, 0)),
                  pl.BlockSpec((1, 2 * w, 2 * half), lambda i: (i, 0, 0)),
                  pl.BlockSpec((1, 2 * half, 2 * w), lambda i: (i, 0, 0)),
                  pl.BlockSpec((1, 2, half), lambda i: (i, 0, 0)),
                  pl.BlockSpec((1, 1, 2 * w), lambda i: (i, 0, 0))],
        out_specs=[pl.BlockSpec((2, r, w), lambda i: (i, 0, 0)),
                   pl.BlockSpec((bt, half), lambda i: (0, i)),
                   pl.BlockSpec((bt, half), lambda i: (0, i))],
        out_shape=[jax.ShapeDtypeStruct((g, r, w), F32),
                   jax.ShapeDtypeStruct((bt, hp * half), F32),
                   jax.ShapeDtypeStruct((bt, hp * half), F32)],
        scratch_shapes=[pltpu.VMEM((r, 2 * half), F32), pltpu.VMEM((r, 2 * half), F32)],
        compiler_params=_params(("parallel",)),
        name="s5_prompt",
    )(u_t, toe, p2, q2, a_chunk, d2)
    y = y_t.reshape(g, n_chunks, bt, L, cch).transpose(2, 1, 3, 0, 4).reshape(bt * t, g * cch)
    return y, f_re, f_im


S5_SAMPLE_GROUPS = 8


def _s5_sample_operators(ab_re, ab_im, bb_re, bb_im, c_re, c_im, d_skip):
    g, n = ab_re.shape
    cch = bb_re.shape[-1]
    gb = S5_SAMPLE_GROUPS
    nb = g // gb
    eye = jnp.eye(gb, dtype=F32)
    b_ri = jnp.stack([bb_re, bb_im], axis=1).reshape(nb, gb, 2, n, cch)
    c_ri = jnp.stack([c_re, -c_im], axis=1).reshape(nb, gb, 2, cch, n)
    b8 = (b_ri.transpose(0, 1, 4, 2, 3)[:, :, :, :, None, :]
          * eye[None, :, None, None, :, None]).reshape(nb, gb * cch, 2 * gb * n)
    c8 = (c_ri.transpose(0, 2, 1, 4, 3)[:, :, :, :, None, :]
          * eye[None, None, :, None, :, None]).reshape(nb, 2 * gb * n, gb * cch)
    a8 = jnp.stack([ab_re.reshape(nb, gb * n), ab_im.reshape(nb, gb * n)], axis=1)
    d8 = d_skip.reshape(nb, 1, gb * cch)
    return b8, c8, a8, d8


def _s5_sample_kernel(u_ref, sr_ref, si_ref, b_ref, c_ref, a_ref, d_ref, y_ref, nr_ref, ni_ref):
    u = u_ref[...]
    half = sr_ref.shape[-1]
    bu = jnp.dot(u, b_ref[0], precision=HIGHEST, preferred_element_type=F32)
    ar = a_ref[0, 0:1, :]
    ai = a_ref[0, 1:2, :]
    sr = sr_ref[...]
    si = si_ref[...]
    xr = ar * sr - ai * si + bu[:, :half]
    xi = ar * si + ai * sr + bu[:, half:]
    nr_ref[...] = xr
    ni_ref[...] = xi
    y = jnp.dot(jnp.concatenate([xr, xi], axis=1), c_ref[0], precision=HIGHEST,
                preferred_element_type=F32) + d_ref[0] * u
    y_ref[...] = jax.nn.gelu(y)


def _s5_sample(z, s_re, s_im, ops):
    b8, c8, a8, d8 = ops
    nb = b8.shape[0]
    b = z.shape[0]
    wu = b8.shape[1]
    ws = a8.shape[-1]
    return pl.pallas_call(
        _s5_sample_kernel,
        grid=(nb,),
        in_specs=[pl.BlockSpec((b, wu), lambda i: (0, i)),
                  pl.BlockSpec((b, ws), lambda i: (0, i)),
                  pl.BlockSpec((b, ws), lambda i: (0, i)),
                  pl.BlockSpec((1, wu, 2 * ws), lambda i: (i, 0, 0)),
                  pl.BlockSpec((1, 2 * ws, wu), lambda i: (i, 0, 0)),
                  pl.BlockSpec((1, 2, ws), lambda i: (i, 0, 0)),
                  pl.BlockSpec((1, 1, wu), lambda i: (i, 0, 0))],
        out_specs=[pl.BlockSpec((b, wu), lambda i: (0, i)),
                   pl.BlockSpec((b, ws), lambda i: (0, i)),
                   pl.BlockSpec((b, ws), lambda i: (0, i))],
        out_shape=[jax.ShapeDtypeStruct((b, nb * wu), F32),
                   jax.ShapeDtypeStruct((b, nb * ws), F32),
                   jax.ShapeDtypeStruct((b, nb * ws), F32)],
        compiler_params=_params(("parallel",)),
        name="s5_sample",
    )(z, s_re, s_im, b8, c8, a8, d8)


def _hgrn_prompt_kernel(q_ref, f_ref, i_ref, g_ref, lb_ref, gain_ref, y_ref, sfin_ref, st_sc,
                        *, n_chunks):
    c = HG_CHUNK
    sub = HG_SUB
    t = pl.program_id(2)

    @pl.when(t == 0)
    def _():
        st_sc[...] = jnp.zeros_like(st_sc)

    lb = lb_ref[...]
    gain = gain_ref[...]
    row = lax.broadcasted_iota(jnp.int32, (c, 1), 0)
    row_in_sub = row % sub
    tri = (lax.broadcasted_iota(jnp.int32, (c, c), 0) >= lax.broadcasted_iota(jnp.int32, (c, c), 1)).astype(F32)
    neg = -jnp.inf

    def chunk(ci, carry):
        r0 = pl.multiple_of(ci * c, c)
        q = _silu(q_ref[pl.ds(r0, c), :])
        f = lb + (1.0 - lb) * _sigmoid(f_ref[pl.ds(r0, c), :])
        kk = 1.0 - f
        v = i_ref[pl.ds(r0, c), :]
        g = jnp.dot(tri, jnp.log(f), precision=HIGHEST, preferred_element_type=F32)
        g_last = g[c - 1:c, :]
        st = st_sc[...]
        o = _dot_nt((q * jnp.exp(g)).astype(BF16), st.astype(BF16))

        blocks = [jnp.zeros((sub, c), F32)]
        for i in range(1, c // sub):
            g_ref_i = g[i * sub - 1:i * sub, :]
            qi = q[i * sub:(i + 1) * sub, :] * jnp.exp(g[i * sub:(i + 1) * sub, :] - g_ref_i)
            earlier = row < i * sub
            ki = kk * jnp.exp(jnp.where(earlier, g_ref_i - g, neg))
            blocks.append(_dot_nt(qi.astype(BF16), ki.astype(BF16)))
        scores = jnp.concatenate(blocks, axis=0)
        o = o + _dot(scores.astype(BF16), v.astype(BF16))

        o = o + jnp.sum(q * kk, axis=-1, keepdims=True) * v
        for d in range(1, sub):
            gs = pltpu.roll(g, d, 0)
            ks = pltpu.roll(kk, d, 0)
            vs = pltpu.roll(v, d, 0)
            dec = jnp.exp(jnp.where(row_in_sub >= d, g - gs, neg))
            o = o + jnp.sum(q * ks * dec, axis=-1, keepdims=True) * vs

        k_dec = kk * jnp.exp(g_last - g)
        st_sc[...] = st * jnp.exp(g_last) + _dot_tn(v.astype(BF16), k_dec.astype(BF16))

        o = o * lax.rsqrt(jnp.mean(o * o, axis=-1, keepdims=True) + RMS_EPS) * gain
        y_ref[pl.ds(r0, c), :] = o * _silu(g_ref[pl.ds(r0, c), :])
        return carry

    lax.fori_loop(0, n_chunks, chunk, 0)

    @pl.when(t == pl.num_programs(2) - 1)
    def _():
        sfin_ref[0, 0] = st_sc[...].T


def _hgrn_prompt(z, lb, gain, bt, t):
    hd = HG_HEAD_DIM
    n_heads = lb.shape[0] // hd
    tb = min(512, t)
    nt = t // tb
    m = bt * t

    def col(block):
        return pl.BlockSpec((tb, hd), lambda b, h, s: (b * nt + s, block * n_heads + h))

    return pl.pallas_call(
        functools.partial(_hgrn_prompt_kernel, n_chunks=tb // HG_CHUNK),
        grid=(bt, n_heads, nt),
        in_specs=[col(1), col(2), col(3), col(4),
                  pl.BlockSpec((1, hd), lambda b, h, s: (0, h)),
                  pl.BlockSpec((1, hd), lambda b, h, s: (0, 0))],
        out_specs=[pl.BlockSpec((tb, hd), lambda b, h, s: (b * nt + s, h)),
                   pl.BlockSpec((1, 1, hd, hd), lambda b, h, s: (b, h, 0, 0))],
        out_shape=[jax.ShapeDtypeStruct((m, n_heads * hd), F32),
                   jax.ShapeDtypeStruct((bt, n_heads, hd, hd), F32)],
        scratch_shapes=[pltpu.VMEM((hd, hd), F32)],
        compiler_params=_params(("parallel", "parallel", "arbitrary")),
        name="hgrn_prompt",
    )(z, z, z, z, lb.reshape(1, -1), gain.reshape(1, hd))


HG_SAMPLE_KEYS = 32


def _hgrn_sample_kernel(q_ref, f_ref, i_ref, g_ref, lb_ref, gain_ref, s_ref, y_ref, snew_ref,
                        qt_sc, ft_sc, kt_sc, vt_sc, o_sc):
    kb = pl.program_id(1)

    @pl.when(kb == 0)
    def _():
        lb = lb_ref[...]
        f = lb + (1.0 - lb) * _sigmoid(f_ref[...])
        qt_sc[...] = _silu(q_ref[...]).T
        ft_sc[...] = f.T
        kt_sc[...] = (1.0 - f).T
        vt_sc[...] = i_ref[...].T
        o_sc[...] = jnp.zeros_like(o_sc)

    vt = vt_sc[...]
    acc = o_sc[...]
    for kl in range(HG_SAMPLE_KEYS):
        k = kb * HG_SAMPLE_KEYS + kl
        st = s_ref[:, 0, kl, :].T
        s_new = st * ft_sc[pl.ds(k, 1), :] + vt * kt_sc[pl.ds(k, 1), :]
        acc = acc + s_new * qt_sc[pl.ds(k, 1), :]
        snew_ref[:, 0, kl, :] = s_new.T
    o_sc[...] = acc

    @pl.when(kb == pl.num_programs(1) - 1)
    def _():
        o = acc.T
        o = o * lax.rsqrt(jnp.mean(o * o, axis=-1, keepdims=True) + RMS_EPS) * gain_ref[...]
        y_ref[...] = o * _silu(g_ref[...])


def _hgrn_sample(z, states, layer, lb, gain):
    _, b, n_heads, hd, _ = states.shape
    kbs = HG_SAMPLE_KEYS

    def col(block):
        return pl.BlockSpec((b, hd), lambda h, k: (0, block * n_heads + h))

    sblk = pl.BlockSpec((b, 1, kbs, hd), lambda h, k: (0, h, k, 0))
    return pl.pallas_call(
        _hgrn_sample_kernel,
        grid=(n_heads, hd // kbs),
        in_specs=[col(1), col(2), col(3), col(4),
                  pl.BlockSpec((1, hd), lambda h, k: (0, h)),
                  pl.BlockSpec((1, hd), lambda h, k: (0, 0)),
                  pl.BlockSpec((None, b, 1, kbs, hd), lambda h, k: (layer, 0, h, k, 0))],
        out_specs=[pl.BlockSpec((b, hd), lambda h, k: (0, h)), sblk],
        out_shape=[jax.ShapeDtypeStruct((b, n_heads * hd), F32),
                   jax.ShapeDtypeStruct(states.shape[1:], F32)],
        scratch_shapes=[pltpu.VMEM((hd, b), F32)] * 5,
        compiler_params=_params(("parallel", "arbitrary")),
        name="hgrn_sample",
    )(z, z, z, z, lb.reshape(1, -1), gain.reshape(1, hd), states)


def _mix_to_hidden(x, z, y_a_pre, y_b, mw, dense, layer, alpha, tm):
    y_a = _glu(y_a_pre, dense["w_glu"], layer, tm)
    merged = _merge(y_a, y_b, z, dense["w_ba"], dense["w_bb"], layer, tm)
    return _ln1(merged, x, dense["w_o"], mw["ln1_g"], mw["ln1_b"], layer, alpha, min(tm, 256))


def _layer_prompt(x, p, mw, lw, layer, alpha, bt, t):
    m = bt * t
    tm = min(512, m)
    dense = mw["dense_bf16"]
    z = _inproj(x, dense["w_in"], layer, tm)
    y_a_pre, f_re, f_im = _s5_prompt(z[:, :lw["s5_width"]], lw["s5_prompt_ops"], bt, t)
    y_b, hg_new = _hgrn_prompt(z, lw["lb"], lw["gn_gain"], bt, t)
    h = _mix_to_hidden(x, z, y_a_pre, y_b, mw, dense, layer, alpha, tm)
    s1, s2, cw, counts = _router_sorted(h, lw["w_router"], lw["b_router"], tm)
    pos8, tab = _plan(s1, s2, counts, tm)
    pos = pos8[:2].reshape(2 * m)
    n_tiles = _moe_num_tiles(m)
    tile_expert, tile_rows, n_used = tab[0, :n_tiles], tab[1, :n_tiles], tab[2, :1]
    xs = _dispatch(h, pos, tile_rows)
    ys = _experts_sorted(xs, tile_expert, tile_rows, n_used, mw["w_g"], mw["w_u"], mw["w_d"], layer)
    x_new = _final_gather(h, cw, p, pos, ys, dense["w_pg"], dense["w_pp"], mw["ln2_g"], mw["ln2_b"], layer,
                          alpha, min(tm, 256))
    return x_new, f_re, f_im, hg_new


def _layer_sample(x, p, s_re, s_im, hg_states, mw, lw, layer, alpha):
    m = x.shape[0]
    dense = mw["dense_f32"]
    z = _inproj(x, dense["w_in"], layer, m)
    y_a_pre, n_re, n_im = _s5_sample(z, s_re, s_im, lw["s5_sample_ops"])
    y_b, hg_new = _hgrn_sample(z, hg_states, layer, lw["lb"], lw["gn_gain"])
    h = _mix_to_hidden(x, z, y_a_pre, y_b, mw, dense, layer, alpha, m)
    comb = _router_dense(h, lw["w_router"], lw["b_router"], m)
    moe = _moe_dense(h, comb, mw["w_g"], mw["w_u"], mw["w_d"], layer, m)
    x_new = _final(h, moe, p, dense["w_pg"], dense["w_pp"], mw["ln2_g"], mw["ln2_b"], layer, alpha, m)
    return x_new, n_re, n_im, hg_new


def _layer_operands(i, lb_all, s5_lambda_re, s5_lambda_im, s5_log_dt, s5_b_re, s5_b_im, s5_c_re, s5_c_im, s5_d,
                    hg_norm_gain, w_group_router, b_group_router, w_expert_router, b_expert_router):
    disc = _s5_discretise(s5_lambda_re[i], s5_lambda_im[i], s5_log_dt[i], s5_b_re[i], s5_b_im[i])
    d_model = w_group_router.shape[1]
    pad = LANES - N_EXPERTS - N_GROUPS
    w_router = jnp.concatenate([w_expert_router[i], w_group_router[i], jnp.zeros((d_model, pad), F32)], axis=1)
    b_router = jnp.concatenate([b_expert_router[i], b_group_router[i], jnp.zeros((pad,), F32)]).reshape(1, LANES)
    return {
        "s5_width": s5_d.shape[1],
        "s5_prompt_ops": _s5_prompt_operators(*disc, s5_c_re[i], s5_c_im[i], s5_d[i]),
        "s5_sample_ops": _s5_sample_operators(*disc, s5_c_re[i], s5_c_im[i], s5_d[i]),
        "lb": lb_all[i],
        "gn_gain": hg_norm_gain[i],
        "w_router": w_router, "b_router": b_router,
    }


def kernel(x_prompt, x_sample, p_prompt, p_sample, state_s5_re, state_s5_im, state_hgrn, w_in, s5_lambda_re, s5_lambda_im, s5_log_dt, s5_b_re, s5_b_im, s5_c_re, s5_c_im, s5_d, s5_w_glu, hg_lower_bounds, hg_norm_gain, w_branch_a, w_branch_b, w_out, ln1_gain, ln1_bias, w_group_router, b_group_router, w_expert_router, b_expert_router, w_exp_gate, w_exp_up, w_exp_down, w_ple_proj, w_ple_gate, ln2_gain, ln2_bias):
    depth = w_in.shape[0]
    bt, t, d_model = x_prompt.shape
    bs = x_sample.shape[0]
    alpha = (2 * depth) ** 0.25
    n_groups, n_state = s5_lambda_re.shape[1:]

    lb_soft = jax.nn.softmax(hg_lower_bounds.astype(F32), axis=0)
    lb_all = jnp.cumsum(lb_soft, axis=0) - lb_soft[0]

    dense_f32 = {"w_in": w_in, "w_glu": s5_w_glu, "w_ba": w_branch_a, "w_bb": w_branch_b,
                 "w_o": w_out, "w_pp": w_ple_proj, "w_pg": w_ple_gate}
    mw = {"dense_f32": dense_f32,
          "dense_bf16": {k: v.astype(BF16) for k, v in dense_f32.items()},
          "w_g": w_exp_gate, "w_u": w_exp_up, "w_d": w_exp_down,
          "ln1_g": ln1_gain, "ln1_b": ln1_bias, "ln2_g": ln2_gain, "ln2_b": ln2_bias}
    pp = p_prompt.reshape(depth, bt * t, -1)
    ps = p_sample.reshape(depth, bs, -1)

    xp = x_prompt.reshape(bt * t, d_model)
    xs = x_sample.reshape(bs, d_model)
    outs = [[] for _ in range(6)]
    for i in range(depth):
        lw = _layer_operands(i, lb_all, s5_lambda_re, s5_lambda_im, s5_log_dt, s5_b_re, s5_b_im, s5_c_re,
                             s5_c_im, s5_d, hg_norm_gain, w_group_router, b_group_router, w_expert_router,
                             b_expert_router)
        xp, a_re, a_im, a_hg = _layer_prompt(xp, pp, mw, lw, i, alpha, bt, t)
        xs, b_re, b_im, b_hg = _layer_sample(
            xs, ps, state_s5_re[i].reshape(bs, n_groups * n_state), state_s5_im[i].reshape(bs, n_groups * n_state),
            state_hgrn, mw, lw, i, alpha)
        for lst, val in zip(outs, (a_re.reshape(bt, n_groups, n_state), a_im.reshape(bt, n_groups, n_state), a_hg,
                                   b_re.reshape(bs, n_groups, n_state), b_im.reshape(bs, n_groups, n_state), b_hg)):
            lst.append(val)

    return (xp.reshape(bt, t, d_model), xs.reshape(bs, 1, d_model), *[jnp.stack(o) for o in outs])
```

```python
import functools
import math

import jax
import jax.numpy as jnp
from jax import lax
from jax.experimental import pallas as pl
from jax.experimental.pallas import tpu as pltpu

F32 = jnp.float32
BF16 = jnp.bfloat16
HIGHEST = lax.Precision.HIGHEST

LANES = 128
SUBLANES = 8
VMEM_LIMIT_BYTES = 56 * 1024 * 1024

S5_GROUP = 16
S5_STATE = 64
S5_CHUNK = 16
HG_HEAD_DIM = 128
HG_CHUNK = 64
HG_SUB = 16
N_GROUPS = 4
EXPERTS_PER_GROUP = 8
N_EXPERTS = N_GROUPS * EXPERTS_PER_GROUP
LN_EPS = 1e-5
RMS_EPS = 1e-6


def _params(semantics):
    return pltpu.CompilerParams(dimension_semantics=semantics, vmem_limit_bytes=VMEM_LIMIT_BYTES)


def _dot(a, b):
    return jnp.dot(a, b, preferred_element_type=F32)


def _mm(a, w):
    if w.dtype == F32:
        return jnp.dot(a, w, precision=HIGHEST, preferred_element_type=F32)
    return jnp.dot(a.astype(BF16), w, preferred_element_type=F32)


def _dot_nt(a, b):
    return lax.dot_general(a, b, (((1,), (1,)), ((), ())), preferred_element_type=F32)


def _dot_tn(a, b):
    return lax.dot_general(a, b, (((0,), (0,)), ((), ())), preferred_element_type=F32)


def _sigmoid(x):
    return jax.nn.sigmoid(x)


def _silu(x):
    return x * jax.nn.sigmoid(x)


def _layer_norm_rows(x, gain, bias):
    mu = jnp.mean(x, axis=-1, keepdims=True)
    xc = x - mu
    var = jnp.mean(xc * xc, axis=-1, keepdims=True)
    return xc * lax.rsqrt(var + LN_EPS) * gain + bias


def _inproj_kernel(x_ref, w_ref, o_ref, xb_ref):
    if w_ref.dtype == F32:
        o_ref[...] = _mm(x_ref[...], w_ref[...])
        return

    @pl.when(pl.program_id(1) == 0)
    def _():
        xb_ref[...] = x_ref[...].astype(BF16)

    o_ref[...] = _dot(xb_ref[...], w_ref[...])


def _layer_spec(block, layer, index_map):
    return pl.BlockSpec((None,) + block, lambda *idx: (layer,) + index_map(*idx))


def _inproj(x, w, layer, tm, tn=1024):
    m, k = x.shape
    n = w.shape[-1]
    return pl.pallas_call(
        _inproj_kernel,
        grid=(m // tm, n // tn),
        in_specs=[pl.BlockSpec((tm, k), lambda i, j: (i, 0)),
                  _layer_spec((k, tn), layer, lambda i, j: (0, j))],
        out_specs=pl.BlockSpec((tm, tn), lambda i, j: (i, j)),
        out_shape=jax.ShapeDtypeStruct((m, n), F32),
        scratch_shapes=[pltpu.VMEM((tm, k), BF16)],
        compiler_params=_params(("parallel", "arbitrary")),
        name="inproj",
    )(x, w)


def _glu_kernel(a_ref, w_ref, o_ref):
    a = a_ref[...]
    o_ref[...] = a * _sigmoid(_mm(a, w_ref[...]))


def _glu(a, w, layer, tm):
    m, k = a.shape
    return pl.pallas_call(
        _glu_kernel,
        grid=(m // tm,),
        in_specs=[pl.BlockSpec((tm, k), lambda i: (i, 0)),
                  _layer_spec((k, k), layer, lambda i: (0, 0))],
        out_specs=pl.BlockSpec((tm, k), lambda i: (i, 0)),
        out_shape=jax.ShapeDtypeStruct((m, k), F32),
        compiler_params=_params(("parallel",)),
        name="s5_glu",
    )(a, w)


def _merge_kernel(ya_ref, yb_ref, ga_ref, gb_ref, wa_ref, wb_ref, o_ref):
    pa = _mm(ya_ref[...], wa_ref[...])
    pb = _mm(yb_ref[...], wb_ref[...])
    o_ref[...] = _sigmoid(ga_ref[...]) * pa + _sigmoid(gb_ref[...]) * pb


def _merge(ya, yb, z, wa, wb, layer, tm, tn=1024):
    m, k = ya.shape
    n = wa.shape[-1]
    nj = n // tn
    return pl.pallas_call(
        _merge_kernel,
        grid=(m // tm, nj),
        in_specs=[pl.BlockSpec((tm, k), lambda i, j: (i, 0)),
                  pl.BlockSpec((tm, k), lambda i, j: (i, 0)),
                  pl.BlockSpec((tm, tn), lambda i, j: (i, 5 + j)),
                  pl.BlockSpec((tm, tn), lambda i, j: (i, 5 + nj + j)),
                  _layer_spec((k, tn), layer, lambda i, j: (0, j)),
                  _layer_spec((k, tn), layer, lambda i, j: (0, j))],
        out_specs=pl.BlockSpec((tm, tn), lambda i, j: (i, j)),
        out_shape=jax.ShapeDtypeStruct((m, n), F32),
        compiler_params=_params(("parallel", "arbitrary")),
        name="branch_merge",
    )(ya, yb, z, z, wa, wb)


def _ln1_kernel(mg_ref, x_ref, w_ref, g_ref, b_ref, o_ref, *, alpha):
    r = alpha * x_ref[...] + _mm(mg_ref[...], w_ref[...])
    o_ref[...] = _layer_norm_rows(r, g_ref[...], b_ref[...])


def _ln1(merged, x, w, gain, bias, layer, alpha, tm):
    m, d = x.shape
    row = pl.BlockSpec((tm, d), lambda i: (i, 0))
    vec = _layer_spec((1, d), layer, lambda i: (0, 0))
    return pl.pallas_call(
        functools.partial(_ln1_kernel, alpha=alpha),
        grid=(m // tm,),
        in_specs=[row, row, _layer_spec((d, d), layer, lambda i: (0, 0)), vec, vec],
        out_specs=row,
        out_shape=jax.ShapeDtypeStruct((m, d), F32),
        compiler_params=_params(("parallel",)),
        name="outproj_ln1",
    )(merged, x, w, gain[:, None, :], bias[:, None, :])


def _final_kernel(h_ref, moe_ref, p_ref, wg_ref, wp_ref, g_ref, b_ref, o_ref, *, alpha):
    h = h_ref[...]
    gate = _sigmoid(_mm(h, wg_ref[...]))
    proj = _mm(p_ref[...], wp_ref[...])
    r = alpha * h + moe_ref[...] + gate * proj
    o_ref[...] = _layer_norm_rows(r, g_ref[...], b_ref[...])


def _final(h, moe, p, wg, wp, gain, bias, layer, alpha, tm):
    m, d = h.shape
    dp = p.shape[-1]
    row = pl.BlockSpec((tm, d), lambda i: (i, 0))
    vec = _layer_spec((1, d), layer, lambda i: (0, 0))
    return pl.pallas_call(
        functools.partial(_final_kernel, alpha=alpha),
        grid=(m // tm,),
        in_specs=[row, row, _layer_spec((tm, dp), layer, lambda i: (i, 0)),
                  _layer_spec((d, d), layer, lambda i: (0, 0)),
                  _layer_spec((dp, d), layer, lambda i: (0, 0)), vec, vec],
        out_specs=row,
        out_shape=jax.ShapeDtypeStruct((m, d), F32),
        compiler_params=_params(("parallel",)),
        name="ple_ln2",
    )(h, moe, p, wg, wp, gain[:, None, :], bias[:, None, :])


def _route(h, w, b):
    logits = jnp.dot(h, w, precision=HIGHEST, preferred_element_type=F32) + b
    lane = lax.broadcasted_iota(jnp.int32, logits.shape, 1)
    lane_f = lane.astype(F32)
    neg = -jnp.inf
    gl = jnp.where(lane >= N_EXPERTS, jnp.where(lane < N_EXPERTS + N_GROUPS, logits, neg), neg)
    gmax = jnp.max(gl, axis=-1, keepdims=True)
    denom = jnp.sum(jnp.exp(gl - gmax), axis=-1, keepdims=True)
    grp_p = 1.0 / denom
    gidx = jnp.min(jnp.where(gl == gmax, lane_f, 1e9), axis=-1, keepdims=True) - N_EXPERTS
    lane_grp = (lane // EXPERTS_PER_GROUP).astype(F32)
    el = jnp.where(lane < N_EXPERTS, jnp.where(lane_grp == gidx, logits, neg), neg)
    t1 = jnp.max(el, axis=-1, keepdims=True)
    i1 = jnp.min(jnp.where(el == t1, lane_f, 1e9), axis=-1, keepdims=True)
    el2 = jnp.where(lane_f == i1, neg, el)
    t2 = jnp.max(el2, axis=-1, keepdims=True)
    i2 = jnp.min(jnp.where(el2 == t2, lane_f, 1e9), axis=-1, keepdims=True)
    e2 = jnp.exp(t2 - t1)
    w1 = 1.0 / (1.0 + e2)
    w2 = e2 * w1
    return lane, lane_f, i1, i2, grp_p * w1, grp_p * w2


def _router_dense_kernel(h_ref, w_ref, b_ref, o_ref):
    _, lane_f, i1, i2, c1, c2 = _route(h_ref[...], w_ref[...], b_ref[...])
    o_ref[...] = jnp.where(lane_f == i1, c1, jnp.where(lane_f == i2, c2, 0.0))


def _router_dense(h, w_r, b_r, tm):
    m, d = h.shape
    return pl.pallas_call(
        _router_dense_kernel,
        grid=(m // tm,),
        in_specs=[pl.BlockSpec((tm, d), lambda i: (i, 0)),
                  pl.BlockSpec((d, LANES), lambda i: (0, 0)),
                  pl.BlockSpec((1, LANES), lambda i: (0, 0))],
        out_specs=pl.BlockSpec((tm, LANES), lambda i: (i, 0)),
        out_shape=jax.ShapeDtypeStruct((m, LANES), F32),
        compiler_params=_params(("parallel",)),
        name="router_dense",
    )(h, w_r, b_r)


def _moe_dense_kernel(h_ref, c_ref, wg_ref, wu_ref, wd_ref, o_ref):
    e = pl.program_id(1)

    @pl.when(e == 0)
    def _():
        o_ref[...] = jnp.zeros_like(o_ref)

    comb = c_ref[...]
    lane = lax.broadcasted_iota(jnp.int32, comb.shape, 1)
    col = jnp.sum(jnp.where(lane == e, comb, 0.0), axis=-1, keepdims=True)
    h = h_ref[...]
    hid = _silu(_mm(h, wg_ref[0])) * _mm(h, wu_ref[0]) * col
    o_ref[...] += _mm(hid, wd_ref[0])


def _moe_dense(h, comb, wg, wu, wd, layer, tm):
    m, d = h.shape
    _, n_e, _, f = wg.shape
    return pl.pallas_call(
        _moe_dense_kernel,
        grid=(m // tm, n_e),
        in_specs=[pl.BlockSpec((tm, d), lambda i, e: (i, 0)),
                  pl.BlockSpec((tm, LANES), lambda i, e: (i, 0)),
                  _layer_spec((1, d, f), layer, lambda i, e: (e, 0, 0)),
                  _layer_spec((1, d, f), layer, lambda i, e: (e, 0, 0)),
                  _layer_spec((1, f, d), layer, lambda i, e: (e, 0, 0))],
        out_specs=pl.BlockSpec((tm, d), lambda i, e: (i, 0)),
        out_shape=jax.ShapeDtypeStruct((m, d), F32),
        compiler_params=_params(("parallel", "arbitrary")),
        name="moe_dense",
    )(h, comb, wg, wu, wd)


MOE_TILE = 256


def _moe_num_tiles(m):
    return 2 * m // MOE_TILE + N_EXPERTS


def _router_sorted_kernel(h_ref, w_ref, b_ref, s1_ref, s2_ref, cw_ref, cnt_ref, carry_sc):
    @pl.when(pl.program_id(0) == 0)
    def _():
        carry_sc[...] = jnp.zeros_like(carry_sc)

    lane, lane_f, i1, i2, c1, c2 = _route(h_ref[...], w_ref[...], b_ref[...])
    sel1 = jnp.where(lane_f == i1, 1.0, 0.0)
    sel2 = jnp.where(lane_f == i2, 1.0, 0.0)
    cnt = sel1 + sel2
    tm = cnt.shape[0]
    strict_lower = (lax.broadcasted_iota(jnp.int32, (tm, tm), 0) > lax.broadcasted_iota(jnp.int32, (tm, tm), 1))
    prefix = _dot(strict_lower.astype(BF16), cnt.astype(BF16)) + carry_sc[0:1, :]
    s1_ref[...] = sel1 * (prefix + 1.0)
    s2_ref[...] = sel2 * (prefix + 1.0)
    cw_ref[...] = jnp.where(lane == 0, c1, jnp.where(lane == 1, c2, 0.0))
    carry_sc[...] = carry_sc[...] + jnp.sum(cnt, axis=0, keepdims=True)
    cnt_ref[...] = carry_sc[...]


def _router_sorted(h, w_r, b_r, tm):
    m, d = h.shape
    tok = pl.BlockSpec((tm, LANES), lambda i: (i, 0))
    return pl.pallas_call(
        _router_sorted_kernel,
        grid=(m // tm,),
        in_specs=[pl.BlockSpec((tm, d), lambda i: (i, 0)),
                  pl.BlockSpec((d, LANES), lambda i: (0, 0)),
                  pl.BlockSpec((1, LANES), lambda i: (0, 0))],
        out_specs=[tok, tok, tok, pl.BlockSpec((SUBLANES, LANES), lambda i: (0, 0))],
        out_shape=[jax.ShapeDtypeStruct((m, LANES), F32)] * 3 + [jax.ShapeDtypeStruct((SUBLANES, LANES), F32)],
        scratch_shapes=[pltpu.VMEM((SUBLANES, LANES), F32)],
        compiler_params=_params(("arbitrary",)),
        name="router_sorted",
    )(h, w_r, b_r)


def _plan_kernel(s1_ref, s2_ref, cnt_ref, pos_ref, tab_ref):
    te = float(MOE_TILE)
    lane = lax.broadcasted_iota(jnp.int32, (1, LANES), 1)
    cnt = jnp.where(lane < N_EXPERTS, cnt_ref[0:1, :], 0.0)
    padded = jnp.floor((cnt + (te - 1.0)) * (1.0 / te)) * te
    r128 = lax.broadcasted_iota(jnp.int32, (LANES, LANES), 0)
    c128 = lax.broadcasted_iota(jnp.int32, (LANES, LANES), 1)
    before = jnp.where(r128 < c128, 1.0, 0.0)
    off = jnp.dot(jnp.broadcast_to(padded, (SUBLANES, LANES)), before, precision=HIGHEST,
                  preferred_element_type=F32)[0:1, :]
    s1 = s1_ref[...]
    s2 = s2_ref[...]
    v1 = jnp.where(s1 > 0.0, s1 - 1.0 + off, 0.0)
    v2 = jnp.where(s2 > 0.0, s2 - 1.0 + off, 0.0)
    ones8 = jnp.ones((SUBLANES, LANES), F32)
    p1 = lax.dot_general(ones8, v1, (((1,), (1,)), ((), ())), precision=HIGHEST, preferred_element_type=F32)
    p2 = lax.dot_general(ones8, v2, (((1,), (1,)), ((), ())), precision=HIGHEST, preferred_element_type=F32)
    row8 = lax.broadcasted_iota(jnp.int32, p1.shape, 0)
    pos_ref[...] = jnp.where(row8 == 0, p1, p2).astype(jnp.int32)

    def per_expert(row):
        return jnp.broadcast_to(row, (LANES, LANES)).T
    off_e = per_expert(off)
    end_e = per_expert(off + padded)
    cnt_e = per_expert(cnt)
    start = c128.astype(F32) * te
    is_e = r128 < N_EXPERTS
    tile_e = jnp.sum(jnp.where(is_e, jnp.where(end_e <= start, 1.0, 0.0), 0.0), axis=0, keepdims=True)
    rows = jnp.clip(cnt_e - (start - off_e), 0.0, te)
    owns = jnp.where(is_e, jnp.where(off_e <= start, jnp.where(start < end_e, rows, 0.0), 0.0), 0.0)
    n_rows = jnp.sum(owns, axis=0, keepdims=True)
    tile_e = jnp.minimum(tile_e, float(N_EXPERTS - 1))
    n_used = jnp.sum(jnp.where(n_rows > 0.0, 1.0, 0.0), axis=-1, keepdims=True)
    row_t = lax.broadcasted_iota(jnp.int32, (SUBLANES, LANES), 0)
    tab_ref[...] = jnp.where(row_t == 0, tile_e, jnp.where(row_t == 1, n_rows, n_used)).astype(jnp.int32)


def _plan(s1, s2, counts, tm):
    m = s1.shape[0]
    tok = pl.BlockSpec((tm, LANES), lambda i: (i, 0))
    return pl.pallas_call(
        _plan_kernel,
        grid=(m // tm,),
        in_specs=[tok, tok, pl.BlockSpec((SUBLANES, LANES), lambda i: (0, 0))],
        out_specs=[pl.BlockSpec((SUBLANES, tm), lambda i: (0, i)),
                   pl.BlockSpec((SUBLANES, LANES), lambda i: (0, 0))],
        out_shape=[jax.ShapeDtypeStruct((SUBLANES, m), jnp.int32),
                   jax.ShapeDtypeStruct((SUBLANES, LANES), jnp.int32)],
        compiler_params=_params(("arbitrary",)),
        name="moe_plan",
    )(s1, s2, counts)


DMA_UNROLL = 8


def _dispatch_kernel(pos_ref, rows_ref, h_ref, xs_hbm, zero_buf, zero_sem, sem, *, m, n_tiles):
    i = pl.program_id(0)
    tm = h_ref.shape[0]

    @pl.when(i == 0)
    def _():
        zero_buf[...] = jnp.zeros_like(zero_buf)

        def zero_copy(j):
            return pltpu.make_async_copy(zero_buf, xs_hbm.at[pl.ds(j * MOE_TILE, MOE_TILE)], zero_sem)

        def zero_start(j, c):
            @pl.when(rows_ref[j] < MOE_TILE)
            def _():
                zero_copy(j).start()
            return c

        def zero_wait(j, c):
            @pl.when(rows_ref[j] < MOE_TILE)
            def _():
                zero_copy(j).wait()
            return c

        lax.fori_loop(0, n_tiles, zero_start, 0)
        lax.fori_loop(0, n_tiles, zero_wait, 0)

    def body(r, c):
        t = i * tm + r
        src = h_ref.at[pl.ds(r, 1)]
        pltpu.make_async_copy(src, xs_hbm.at[pl.ds(pos_ref[t], 1)], sem).start()
        pltpu.make_async_copy(src, xs_hbm.at[pl.ds(pos_ref[m + t], 1)], sem).start()
        return c
    lax.fori_loop(0, tm, body, 0, unroll=DMA_UNROLL)

    pltpu.make_async_copy(xs_hbm.at[pl.ds(0, 2 * tm)], xs_hbm.at[pl.ds(0, 2 * tm)], sem).wait()


def _dispatch(h, pos, tile_rows, tm):
    m, d = h.shape
    n_tiles = _moe_num_tiles(m)
    grid_spec = pltpu.PrefetchScalarGridSpec(
        num_scalar_prefetch=2,
        grid=(m // tm,),
        in_specs=[pl.BlockSpec((tm, d), lambda i, pos, rows: (i, 0))],
        out_specs=pl.BlockSpec(memory_space=pl.ANY),
        scratch_shapes=[pltpu.VMEM((MOE_TILE, d), F32), pltpu.SemaphoreType.DMA(()),
                        pltpu.SemaphoreType.DMA(())],
    )
    return pl.pallas_call(
        functools.partial(_dispatch_kernel, m=m, n_tiles=n_tiles),
        grid_spec=grid_spec,
        out_shape=jax.ShapeDtypeStruct((n_tiles * MOE_TILE, d), F32),
        compiler_params=_params(("arbitrary",)),
        name="moe_dispatch",
    )(pos, tile_rows, h)


def _experts_kernel(te_ref, nr_ref, nu_ref, x_ref, wg_ref, wu_ref, wd_ref, y_ref, wgb, wub, wdb):
    j = pl.program_id(0)
    new_expert = jnp.logical_or(j == 0, te_ref[j] != te_ref[jnp.maximum(j - 1, 0)])

    @pl.when(jnp.logical_and(new_expert, nr_ref[j] > 0))
    def _():
        wgb[...] = wg_ref[0].astype(BF16)
        wub[...] = wu_ref[0].astype(BF16)
        wdb[...] = wd_ref[0].astype(BF16)

    @pl.when(nr_ref[j] > 0)
    def _():
        x = x_ref[...].astype(BF16)
        hid = _silu(_dot(x, wgb[...])) * _dot(x, wub[...])
        y_ref[...] = _dot(hid.astype(BF16), wdb[...])

    @pl.when(nr_ref[j] == 0)
    def _():
        y_ref[...] = jnp.zeros_like(y_ref)


def _experts_sorted(xs, tile_expert, tile_rows, n_used, wg, wu, wd, layer):
    n_rows, d = xs.shape
    _, n_e, _, f = wg.shape
    n_tiles = n_rows // MOE_TILE

    def used_tile(j, te, nr, nu):
        return (jnp.minimum(j, nu[0] - 1), 0)

    def expert_block(j, te, nr, nu):
        return (layer, te[j], 0, 0)

    grid_spec = pltpu.PrefetchScalarGridSpec(
        num_scalar_prefetch=3,
        grid=(n_tiles,),
        in_specs=[pl.BlockSpec((MOE_TILE, d), used_tile),
                  pl.BlockSpec((None, 1, d, f), expert_block),
                  pl.BlockSpec((None, 1, d, f), expert_block),
                  pl.BlockSpec((None, 1, f, d), expert_block)],
        out_specs=pl.BlockSpec((MOE_TILE, d), lambda j, te, nr, nu: (j, 0)),
        scratch_shapes=[pltpu.VMEM((d, f), BF16), pltpu.VMEM((d, f), BF16), pltpu.VMEM((f, d), BF16)],
    )
    return pl.pallas_call(
        _experts_kernel,
        grid_spec=grid_spec,
        out_shape=jax.ShapeDtypeStruct((n_rows, d), F32),
        compiler_params=_params(("arbitrary",)),
        name="moe_experts_sorted",
    )(tile_expert, tile_rows, n_used, xs, wg, wu, wd)


def _final_gather_kernel(pos_ref, h_ref, cw_ref, p_ref, wg_ref, wp_ref, g_ref, b_ref, ys_hbm, o_ref,
                         ybuf, sem, *, alpha, m):
    i = pl.program_id(0)
    slot = i % 2
    tm = h_ref.shape[0]

    def gather(tile, dst_slot):
        def body(r, c):
            t = tile * tm + r
            for pick in range(2):
                pltpu.make_async_copy(ys_hbm.at[pl.ds(pos_ref[pick * m + t], 1)],
                                      ybuf.at[dst_slot, pl.ds(pick * tm + r, 1)], sem.at[dst_slot]).start()
            return c
        lax.fori_loop(0, tm, body, 0, unroll=DMA_UNROLL)

    @pl.when(i == 0)
    def _():
        gather(0, 0)

    @pl.when(i + 1 < pl.num_programs(0))
    def _():
        gather(i + 1, 1 - slot)

    h = h_ref[...]
    gate = _sigmoid(_mm(h, wg_ref[...]))
    proj = _mm(p_ref[...], wp_ref[...])

    pltpu.make_async_copy(ys_hbm.at[pl.ds(0, 2 * tm)], ybuf.at[slot], sem.at[slot]).wait()

    cw = cw_ref[...]
    moe = cw[:, 0:1] * ybuf[slot, 0:tm] + cw[:, 1:2] * ybuf[slot, tm:2 * tm]
    r = alpha * h + moe + gate * proj
    o_ref[...] = _layer_norm_rows(r, g_ref[...], b_ref[...])


def _final_gather(h, cw, p, pos, ys, wg, wp, gain, bias, layer, alpha, tm):
    m, d = h.shape
    dp = p.shape[-1]
    row = pl.BlockSpec((tm, d), lambda i, pos: (i, 0))
    vec = _layer_spec((1, d), layer, lambda i, pos: (0, 0))
    grid_spec = pltpu.PrefetchScalarGridSpec(
        num_scalar_prefetch=1,
        grid=(m // tm,),
        in_specs=[row, pl.BlockSpec((tm, LANES), lambda i, pos: (i, 0)),
                  _layer_spec((tm, dp), layer, lambda i, pos: (i, 0)),
                  _layer_spec((d, d), layer, lambda i, pos: (0, 0)),
                  _layer_spec((dp, d), layer, lambda i, pos: (0, 0)), vec, vec,
                  pl.BlockSpec(memory_space=pl.ANY)],
        out_specs=row,
        scratch_shapes=[pltpu.VMEM((2, 2 * tm, d), F32), pltpu.SemaphoreType.DMA((2,))],
    )
    return pl.pallas_call(
        functools.partial(_final_gather_kernel, alpha=alpha, m=m),
        grid_spec=grid_spec,
        out_shape=jax.ShapeDtypeStruct((m, d), F32),
        compiler_params=_params(("arbitrary",)),
        name="ple_ln2_gather",
    )(pos, h, cw, p, wg, wp, gain[:, None, :], bias[:, None, :], ys)


def _s5_discretise(lam_re, lam_im, log_dt, b_re, b_im):
    dt = jnp.exp(log_dt)[:, None]
    mag = jnp.exp(lam_re * dt)
    ab_re = mag * jnp.cos(lam_im * dt)
    ab_im = mag * jnp.sin(lam_im * dt)
    den = lam_re * lam_re + lam_im * lam_im
    nr = ab_re - 1.0
    zr = (nr * lam_re + ab_im * lam_im) / den
    zi = (ab_im * lam_re - nr * lam_im) / den
    bb_re = zr[..., None] * b_re - zi[..., None] * b_im
    bb_im = zr[..., None] * b_im + zi[..., None] * b_re
    return ab_re, ab_im, bb_re, bb_im


def _s5_prompt_operators(ab_re, ab_im, bb_re, bb_im, c_re, c_im, d_skip):
    g, n = ab_re.shape
    cch = bb_re.shape[-1]
    L = S5_CHUNK
    pw_re = [jnp.ones_like(ab_re)]
    pw_im = [jnp.zeros_like(ab_im)]
    for _ in range(L):
        pr, pi = pw_re[-1], pw_im[-1]
        pw_re.append(pr * ab_re - pi * ab_im)
        pw_im.append(pr * ab_im + pi * ab_re)
    a_re = jnp.stack(pw_re)
    a_im = jnp.stack(pw_im)
    w_re = a_re[:L, :, :, None] * bb_re - a_im[:L, :, :, None] * bb_im
    w_im = a_re[:L, :, :, None] * bb_im + a_im[:L, :, :, None] * bb_re
    kern = (jnp.einsum('gcn,kgnd->kgcd', c_re, w_re, precision=HIGHEST)
            - jnp.einsum('gcn,kgnd->kgcd', c_im, w_im, precision=HIGHEST))
    zero_lags = jnp.zeros_like(kern)
    toe = jnp.stack([jnp.concatenate([zero_lags[:ti], kern[:L - ti]], axis=0) for ti in range(L)])
    toe = toe.transpose(2, 0, 4, 1, 3).reshape(g, L * cch, L * cch)

    p_re = w_re[::-1].transpose(1, 0, 3, 2).reshape(g, L * cch, n)
    p_im = w_im[::-1].transpose(1, 0, 3, 2).reshape(g, L * cch, n)
    q_re = (jnp.einsum('gcn,tgn->gntc', c_re, a_re[1:]) - jnp.einsum('gcn,tgn->gntc', c_im, a_im[1:]))
    q_im = -(jnp.einsum('gcn,tgn->gntc', c_re, a_im[1:]) + jnp.einsum('gcn,tgn->gntc', c_im, a_re[1:]))
    q_re = q_re.reshape(g, n, L * cch)
    q_im = q_im.reshape(g, n, L * cch)

    hp = g // 2
    w = L * cch
    eye2 = jnp.eye(2, dtype=F32)
    p_ri = jnp.stack([p_re, p_im], axis=2).reshape(hp, 2, w, 2, n)
    q_ri = jnp.stack([q_re, q_im], axis=1).reshape(hp, 2, 2, n, w)
    p2 = (p_ri[:, :, :, :, None, :] * eye2[None, :, None, None, :, None]).reshape(hp, 2 * w, 4 * n)
    q2 = (q_ri.transpose(0, 2, 1, 3, 4)[:, :, :, :, None, :]
          * eye2[None, None, :, None, :, None]).reshape(hp, 4 * n, 2 * w)
    a_chunk = jnp.stack([a_re[L].reshape(hp, 2 * n), a_im[L].reshape(hp, 2 * n)], axis=1)
    d2 = jnp.broadcast_to(d_skip.reshape(hp, 2, 1, cch), (hp, 2, L, cch)).reshape(hp, 1, 2 * w)
    return toe.astype(BF16), p2.astype(BF16), q2.astype(BF16), a_chunk, d2


def _s5_prompt_kernel(u_ref, t_ref, p_ref, q_ref, a_ref, d_ref, y_ref, fre_ref, fim_ref,
                      s_sc, xin_sc, *, n_chunks, bt):
    half = a_ref.shape[-1]
    u0 = u_ref[0]
    u1 = u_ref[1]
    ub0 = u0.astype(BF16)
    ub1 = u1.astype(BF16)
    y_intra = jnp.concatenate([_dot(ub0, t_ref[0]), _dot(ub1, t_ref[1])], axis=1)
    s_sc[...] = _dot(jnp.concatenate([ub0, ub1], axis=1), p_ref[0])
    ar = a_ref[0, 0:1, :]
    ai = a_ref[0, 1:2, :]
    xr = jnp.zeros((bt, half), F32)
    xi = jnp.zeros((bt, half), F32)
    for j in range(n_chunks):
        rows = slice(j * bt, (j + 1) * bt)
        xin_sc[rows, 0:half] = xr
        xin_sc[rows, half:2 * half] = xi
        sr = s_sc[rows, 0:half]
        si = s_sc[rows, half:2 * half]
        xr, xi = ar * xr - ai * xi + sr, ar * xi + ai * xr + si
    fre_ref[...] = xr
    fim_ref[...] = xi
    y_carry = _dot(xin_sc[...].astype(BF16), q_ref[0])
    y = y_intra + y_carry + d_ref[0] * jnp.concatenate([u0, u1], axis=1)
    y = jax.nn.gelu(y)
    w = u0.shape[1]
    y_ref[0] = y[:, :w]
    y_ref[1] = y[:, w:]


def _s5_prompt(u, ops, bt, t):
    toe, p2, q2, a_chunk, d2 = ops
    g = toe.shape[0]
    hp = g // 2
    L = S5_CHUNK
    cch = S5_GROUP
    n_chunks = t // L
    r = n_chunks * bt
    w = L * cch
    half = a_chunk.shape[-1]
    u_t = u.reshape(bt, n_chunks, L, g, cch).transpose(3, 1, 0, 2, 4).reshape(g, r, w)
    y_t, f_re, f_im = pl.pallas_call(
        functools.partial(_s5_prompt_kernel, n_chunks=n_chunks, bt=bt),
        grid=(hp,),
        in_specs=[pl.BlockSpec((2, r, w), lambda i: (i, 0, 0)),
                  pl.BlockSpec((2, w, w), lambda i: (i, 0, 0)),
                  pl.BlockSpec((1, 2 * w, 2 * half), lambda i: (i, 0, 0)),
                  pl.BlockSpec((1, 2 * half, 2 * w), lambda i: (i, 0, 0)),
                  pl.BlockSpec((1, 2, half), lambda i: (i, 0, 0)),
                  pl.BlockSpec((1, 1, 2 * w), lambda i: (i, 0, 0))],
        out_specs=[pl.BlockSpec((2, r, w), lambda i: (i, 0, 0)),
                   pl.BlockSpec((bt, half), lambda i: (0, i)),
                   pl.BlockSpec((bt, half), lambda i: (0, i))],
        out_shape=[jax.ShapeDtypeStruct((g, r, w), F32),
                   jax.ShapeDtypeStruct((bt, hp * half), F32),
                   jax.ShapeDtypeStruct((bt, hp * half), F32)],
        scratch_shapes=[pltpu.VMEM((r, 2 * half), F32), pltpu.VMEM((r, 2 * half), F32)],
        compiler_params=_params(("parallel",)),
        name="s5_prompt",
    )(u_t, toe, p2, q2, a_chunk, d2)
    y = y_t.reshape(g, n_chunks, bt, L, cch).transpose(2, 1, 3, 0, 4).reshape(bt * t, g * cch)
    return y, f_re, f_im


S5_SAMPLE_GROUPS = 8


def _s5_sample_operators(ab_re, ab_im, bb_re, bb_im, c_re, c_im, d_skip):
    g, n = ab_re.shape
    cch = bb_re.shape[-1]
    gb = S5_SAMPLE_GROUPS
    nb = g // gb
    eye = jnp.eye(gb, dtype=F32)
    b_ri = jnp.stack([bb_re, bb_im], axis=1).reshape(nb, gb, 2, n, cch)
    c_ri = jnp.stack([c_re, -c_im], axis=1).reshape(nb, gb, 2, cch, n)
    b8 = (b_ri.transpose(0, 1, 4, 2, 3)[:, :, :, :, None, :]
          * eye[None, :, None, None, :, None]).reshape(nb, gb * cch, 2 * gb * n)
    c8 = (c_ri.transpose(0, 2, 1, 4, 3)[:, :, :, :, None, :]
          * eye[None, None, :, None, :, None]).reshape(nb, 2 * gb * n, gb * cch)
    a8 = jnp.stack([ab_re.reshape(nb, gb * n), ab_im.reshape(nb, gb * n)], axis=1)
    d8 = d_skip.reshape(nb, 1, gb * cch)
    return b8, c8, a8, d8


def _s5_sample_kernel(u_ref, sr_ref, si_ref, b_ref, c_ref, a_ref, d_ref, y_ref, nr_ref, ni_ref):
    u = u_ref[...]
    half = sr_ref.shape[-1]
    bu = jnp.dot(u, b_ref[0], precision=HIGHEST, preferred_element_type=F32)
    ar = a_ref[0, 0:1, :]
    ai = a_ref[0, 1:2, :]
    sr = sr_ref[...]
    si = si_ref[...]
    xr = ar * sr - ai * si + bu[:, :half]
    xi = ar * si + ai * sr + bu[:, half:]
    nr_ref[...] = xr
    ni_ref[...] = xi
    y = jnp.dot(jnp.concatenate([xr, xi], axis=1), c_ref[0], precision=HIGHEST,
                preferred_element_type=F32) + d_ref[0] * u
    y_ref[...] = jax.nn.gelu(y)


def _s5_sample(z, s_re, s_im, ops):
    b8, c8, a8, d8 = ops
    nb = b8.shape[0]
    b = z.shape[0]
    wu = b8.shape[1]
    ws = a8.shape[-1]
    return pl.pallas_call(
        _s5_sample_kernel,
        grid=(nb,),
        in_specs=[pl.BlockSpec((b, wu), lambda i: (0, i)),
                  pl.BlockSpec((b, ws), lambda i: (0, i)),
                  pl.BlockSpec((b, ws), lambda i: (0, i)),
                  pl.BlockSpec((1, wu, 2 * ws), lambda i: (i, 0, 0)),
                  pl.BlockSpec((1, 2 * ws, wu), lambda i: (i, 0, 0)),
                  pl.BlockSpec((1, 2, ws), lambda i: (i, 0, 0)),
                  pl.BlockSpec((1, 1, wu), lambda i: (i, 0, 0))],
        out_specs=[pl.BlockSpec((b, wu), lambda i: (0, i)),
                   pl.BlockSpec((b, ws), lambda i: (0, i)),
                   pl.BlockSpec((b, ws), lambda i: (0, i))],
        out_shape=[jax.ShapeDtypeStruct((b, nb * wu), F32),
                   jax.ShapeDtypeStruct((b, nb * ws), F32),
                   jax.ShapeDtypeStruct((b, nb * ws), F32)],
        compiler_params=_params(("parallel",)),
        name="s5_sample",
    )(z, s_re, s_im, b8, c8, a8, d8)


HG_HEADS_PER_STEP = 2
HG_SAFE_SPAN = 60.0


def _hgrn_intra_factored(q, kk, v, g):
    c, sub = HG_CHUNK, HG_SUB
    n_sub = c // sub
    row = lax.broadcasted_iota(jnp.int32, (c, 1), 0)
    starts = [g[i * sub - 1:i * sub, :] if i else jnp.zeros_like(g[0:1, :]) for i in range(n_sub)]
    g_start_rows = jnp.concatenate([jnp.broadcast_to(s, (sub, s.shape[1])) for s in starts], axis=0)
    q_fac = (q * jnp.exp(g - g_start_rows)).astype(BF16)
    pad = jnp.zeros((LANES - c, kk.shape[1]), BF16)
    k_fac = []
    for i in range(n_sub):
        ki = kk * jnp.exp(jnp.where(row < (i + 1) * sub, starts[i] - g, -jnp.inf))
        k_fac += [ki.astype(BF16), pad]
    wide = _dot_nt(q_fac, jnp.concatenate(k_fac, axis=0))
    scores = jnp.concatenate([wide[i * sub:(i + 1) * sub, i * LANES:i * LANES + c] for i in range(n_sub)], axis=0)
    causal = lax.broadcasted_iota(jnp.int32, (c, c), 0) >= lax.broadcasted_iota(jnp.int32, (c, c), 1)
    return _dot(jnp.where(causal, scores, 0.0).astype(BF16), v.astype(BF16))


def _hgrn_intra_exact(q, kk, v, g):
    c, sub = HG_CHUNK, HG_SUB
    row = lax.broadcasted_iota(jnp.int32, (c, 1), 0)
    row_in_sub = row % sub
    neg = -jnp.inf
    blocks = [jnp.zeros((sub, c), F32)]
    for i in range(1, c // sub):
        g_start = g[i * sub - 1:i * sub, :]
        qi = q[i * sub:(i + 1) * sub, :] * jnp.exp(g[i * sub:(i + 1) * sub, :] - g_start)
        ki = kk * jnp.exp(jnp.where(row < i * sub, g_start - g, neg))
        blocks.append(_dot_nt(qi.astype(BF16), ki.astype(BF16)))
    o = _dot(jnp.concatenate(blocks, axis=0).astype(BF16), v.astype(BF16))
    o = o + jnp.sum(q * kk, axis=-1, keepdims=True) * v
    for d in range(1, sub):
        gs = pltpu.roll(g, d, 0)
        ks = pltpu.roll(kk, d, 0)
        vs = pltpu.roll(v, d, 0)
        dec = jnp.exp(jnp.where(row_in_sub >= d, g - gs, neg))
        o = o + jnp.sum(q * ks * dec, axis=-1, keepdims=True) * vs
    return o


def _hgrn_prompt_kernel(q_ref, f_ref, i_ref, g_ref, lb_ref, gain_ref, y_ref, sfin_ref, st_sc, inter_sc,
                        *, n_chunks):
    c = HG_CHUNK
    sub = HG_SUB
    hd = HG_HEAD_DIM
    n_h = st_sc.shape[0]
    t = pl.program_id(2)

    @pl.when(t == 0)
    def _():
        st_sc[...] = jnp.zeros_like(st_sc)

    gain = gain_ref[...]
    lb_all = lb_ref[...]
    tri = (lax.broadcasted_iota(jnp.int32, (c, c), 0) >= lax.broadcasted_iota(jnp.int32, (c, c), 1)).astype(BF16)

    def chunk(ci, carry):
        r0 = pl.multiple_of(ci * c, c)

        def forget_and_decay():
            f = lb_all + (1.0 - lb_all) * _sigmoid(f_ref[pl.ds(r0, c), :])
            log_f = jnp.log(f)
            hi = log_f.astype(BF16)
            rest = log_f - hi.astype(F32)
            mid = rest.astype(BF16)
            lo = (rest - mid.astype(F32)).astype(BF16)
            sums = _dot(tri, jnp.concatenate([hi, mid, lo], axis=1))
            w = n_h * hd
            return f, (sums[:, 2 * w:] + sums[:, w:2 * w]) + sums[:, :w]

        def operands(hh, f, g):
            lanes = slice(hh * hd, (hh + 1) * hd)
            return _silu(q_ref[pl.ds(r0, c), lanes]), 1.0 - f[:, lanes], i_ref[pl.ds(r0, c), lanes], g[:, lanes]

        def finish(o, hh):
            lanes = slice(hh * hd, (hh + 1) * hd)
            o = o * lax.rsqrt(jnp.mean(o * o, axis=-1, keepdims=True) + RMS_EPS) * gain
            y_ref[pl.ds(r0, c), lanes] = o * _silu(g_ref[pl.ds(r0, c), lanes])

        f_all, g_all = forget_and_decay()
        span = jnp.zeros((1, hd), F32)
        for hh in range(n_h):
            q, kk, v, g = operands(hh, f_all, g_all)
            for i in range(c // sub):
                g_start = g[i * sub - 1:i * sub, :] if i else jnp.zeros_like(g[0:1, :])
                span = jnp.maximum(span, g_start - g[(i + 1) * sub - 1:(i + 1) * sub, :])
            g_last = g[c - 1:c, :]
            st = st_sc[hh]
            inter = _dot_nt((q * jnp.exp(g)).astype(BF16), st.astype(BF16))
            inter_sc[hh] = inter
            finish(inter + _hgrn_intra_factored(q, kk, v, g), hh)
            k_dec = kk * jnp.exp(g_last - g)
            st_sc[hh] = st * jnp.exp(g_last) + _dot_tn(v.astype(BF16), k_dec.astype(BF16))

        @pl.when(jnp.max(span) > HG_SAFE_SPAN)
        def _():
            f_again, g_again = forget_and_decay()
            for hh in range(n_h):
                q, kk, v, g = operands(hh, f_again, g_again)
                finish(inter_sc[hh] + _hgrn_intra_exact(q, kk, v, g), hh)

        return carry

    lax.fori_loop(0, n_chunks, chunk, 0)

    @pl.when(t == pl.num_programs(2) - 1)
    def _():
        for hh in range(n_h):
            sfin_ref[0, hh] = st_sc[hh].T


def _hgrn_prompt(z, lb, gain, bt, t):
    hd = HG_HEAD_DIM
    n_heads = lb.shape[0] // hd
    n_h = HG_HEADS_PER_STEP
    groups = n_heads // n_h
    w = n_h * hd
    tb = min(512, t)
    nt = t // tb
    m = bt * t

    def col(block):
        return pl.BlockSpec((tb, w), lambda b, h, s: (b * nt + s, block * groups + h))

    return pl.pallas_call(
        functools.partial(_hgrn_prompt_kernel, n_chunks=tb // HG_CHUNK),
        grid=(bt, groups, nt),
        in_specs=[col(1), col(2), col(3), col(4),
                  pl.BlockSpec((1, w), lambda b, h, s: (0, h)),
                  pl.BlockSpec((1, hd), lambda b, h, s: (0, 0))],
        out_specs=[pl.BlockSpec((tb, w), lambda b, h, s: (b * nt + s, h)),
                   pl.BlockSpec((1, n_h, hd, hd), lambda b, h, s: (b, h, 0, 0))],
        out_shape=[jax.ShapeDtypeStruct((m, n_heads * hd), F32),
                   jax.ShapeDtypeStruct((bt, n_heads, hd, hd), F32)],
        scratch_shapes=[pltpu.VMEM((n_h, hd, hd), F32), pltpu.VMEM((n_h, HG_CHUNK, hd), F32)],
        compiler_params=_params(("parallel", "parallel", "arbitrary")),
        name="hgrn_prompt",
    )(z, z, z, z, lb.reshape(1, -1), gain.reshape(1, hd))


HG_SAMPLE_KEYS = 32


def _hgrn_sample_kernel(q_ref, f_ref, i_ref, g_ref, lb_ref, gain_ref, s_ref, y_ref, snew_ref,
                        qt_sc, ft_sc, kt_sc, vt_sc, o_sc):
    kb = pl.program_id(1)

    @pl.when(kb == 0)
    def _():
        lb = lb_ref[...]
        f = lb + (1.0 - lb) * _sigmoid(f_ref[...])
        qt_sc[...] = _silu(q_ref[...]).T
        ft_sc[...] = f.T
        kt_sc[...] = (1.0 - f).T
        vt_sc[...] = i_ref[...].T
        o_sc[...] = jnp.zeros_like(o_sc)

    vt = vt_sc[...]
    acc = o_sc[...]
    for kl in range(HG_SAMPLE_KEYS):
        k = kb * HG_SAMPLE_KEYS + kl
        st = s_ref[:, 0, kl, :].T
        s_new = st * ft_sc[pl.ds(k, 1), :] + vt * kt_sc[pl.ds(k, 1), :]
        acc = acc + s_new * qt_sc[pl.ds(k, 1), :]
        snew_ref[:, 0, kl, :] = s_new.T
    o_sc[...] = acc

    @pl.when(kb == pl.num_programs(1) - 1)
    def _():
        o = acc.T
        o = o * lax.rsqrt(jnp.mean(o * o, axis=-1, keepdims=True) + RMS_EPS) * gain_ref[...]
        y_ref[...] = o * _silu(g_ref[...])


def _hgrn_sample(z, states, layer, lb, gain):
    _, b, n_heads, hd, _ = states.shape
    kbs = HG_SAMPLE_KEYS

    def col(block):
        return pl.BlockSpec((b, hd), lambda h, k: (0, block * n_heads + h))

    sblk = pl.BlockSpec((b, 1, kbs, hd), lambda h, k: (0, h, k, 0))
    return pl.pallas_call(
        _hgrn_sample_kernel,
        grid=(n_heads, hd // kbs),
        in_specs=[col(1), col(2), col(3), col(4),
                  pl.BlockSpec((1, hd), lambda h, k: (0, h)),
                  pl.BlockSpec((1, hd), lambda h, k: (0, 0)),
                  pl.BlockSpec((None, b, 1, kbs, hd), lambda h, k: (layer, 0, h, k, 0))],
        out_specs=[pl.BlockSpec((b, hd), lambda h, k: (0, h)), sblk],
        out_shape=[jax.ShapeDtypeStruct((b, n_heads * hd), F32),
                   jax.ShapeDtypeStruct(states.shape[1:], F32)],
        scratch_shapes=[pltpu.VMEM((hd, b), F32)] * 5,
        compiler_params=_params(("parallel", "arbitrary")),
        name="hgrn_sample",
    )(z, z, z, z, lb.reshape(1, -1), gain.reshape(1, hd), states)


def _mix_to_hidden(x, z, y_a_pre, y_b, mw, dense, layer, alpha, tm):
    y_a = _glu(y_a_pre, dense["w_glu"], layer, tm)
    merged = _merge(y_a, y_b, z, dense["w_ba"], dense["w_bb"], layer, tm)
    return _ln1(merged, x, dense["w_o"], mw["ln1_g"], mw["ln1_b"], layer, alpha, min(tm, 256))


def _layer_prompt(x, p, mw, lw, layer, alpha, bt, t):
    m = bt * t
    tm = min(512, m)
    dense = mw["dense_bf16"]
    z = _inproj(x, dense["w_in"], layer, tm)
    y_a_pre, f_re, f_im = _s5_prompt(z[:, :lw["s5_width"]], lw["s5_prompt_ops"], bt, t)
    y_b, hg_new = _hgrn_prompt(z, lw["lb"], lw["gn_gain"], bt, t)
    h = _mix_to_hidden(x, z, y_a_pre, y_b, mw, dense, layer, alpha, tm)
    s1, s2, cw, counts = _router_sorted(h, lw["w_router"], lw["b_router"], tm)
    pos8, tab = _plan(s1, s2, counts, tm)
    pos = pos8[:2].reshape(2 * m)
    n_tiles = _moe_num_tiles(m)
    tile_expert, tile_rows, n_used = tab[0, :n_tiles], tab[1, :n_tiles], tab[2, :1]
    xs = _dispatch(h, pos, tile_rows, tm)
    ys = _experts_sorted(xs, tile_expert, tile_rows, n_used, mw["w_g"], mw["w_u"], mw["w_d"], layer)
    x_new = _final_gather(h, cw, p, pos, ys, dense["w_pg"], dense["w_pp"], mw["ln2_g"], mw["ln2_b"], layer,
                          alpha, min(tm, 256))
    return x_new, f_re, f_im, hg_new


def _layer_sample(x, p, s_re, s_im, hg_states, mw, lw, layer, alpha):
    m = x.shape[0]
    dense = mw["dense_f32"]
    z = _inproj(x, dense["w_in"], layer, m)
    y_a_pre, n_re, n_im = _s5_sample(z, s_re, s_im, lw["s5_sample_ops"])
    y_b, hg_new = _hgrn_sample(z, hg_states, layer, lw["lb"], lw["gn_gain"])
    h = _mix_to_hidden(x, z, y_a_pre, y_b, mw, dense, layer, alpha, m)
    comb = _router_dense(h, lw["w_router"], lw["b_router"], m)
    moe = _moe_dense(h, comb, mw["w_g"], mw["w_u"], mw["w_d"], layer, m)
    x_new = _final(h, moe, p, dense["w_pg"], dense["w_pp"], mw["ln2_g"], mw["ln2_b"], layer, alpha, m)
    return x_new, n_re, n_im, hg_new


def _layer_operands(i, lb_all, s5_lambda_re, s5_lambda_im, s5_log_dt, s5_b_re, s5_b_im, s5_c_re, s5_c_im, s5_d,
                    hg_norm_gain, w_group_router, b_group_router, w_expert_router, b_expert_router):
    disc = _s5_discretise(s5_lambda_re[i], s5_lambda_im[i], s5_log_dt[i], s5_b_re[i], s5_b_im[i])
    d_model = w_group_router.shape[1]
    pad = LANES - N_EXPERTS - N_GROUPS
    w_router = jnp.concatenate([w_expert_router[i], w_group_router[i], jnp.zeros((d_model, pad), F32)], axis=1)
    b_router = jnp.concatenate([b_expert_router[i], b_group_router[i], jnp.zeros((pad,), F32)]).reshape(1, LANES)
    return {
        "s5_width": s5_d.shape[1],
        "s5_prompt_ops": _s5_prompt_operators(*disc, s5_c_re[i], s5_c_im[i], s5_d[i]),
        "s5_sample_ops": _s5_sample_operators(*disc, s5_c_re[i], s5_c_im[i], s5_d[i]),
        "lb": lb_all[i],
        "gn_gain": hg_norm_gain[i],
        "w_router": w_router, "b_router": b_router,
    }


def kernel(x_prompt, x_sample, p_prompt, p_sample, state_s5_re, state_s5_im, state_hgrn, w_in, s5_lambda_re, s5_lambda_im, s5_log_dt, s5_b_re, s5_b_im, s5_c_re, s5_c_im, s5_d, s5_w_glu, hg_lower_bounds, hg_norm_gain, w_branch_a, w_branch_b, w_out, ln1_gain, ln1_bias, w_group_router, b_group_router, w_expert_router, b_expert_router, w_exp_gate, w_exp_up, w_exp_down, w_ple_proj, w_ple_gate, ln2_gain, ln2_bias):
    depth = w_in.shape[0]
    bt, t, d_model = x_prompt.shape
    bs = x_sample.shape[0]
    alpha = (2 * depth) ** 0.25
    n_groups, n_state = s5_lambda_re.shape[1:]

    lb_soft = jax.nn.softmax(hg_lower_bounds.astype(F32), axis=0)
    lb_all = jnp.cumsum(lb_soft, axis=0) - lb_soft[0]

    dense_f32 = {"w_in": w_in, "w_glu": s5_w_glu, "w_ba": w_branch_a, "w_bb": w_branch_b,
                 "w_o": w_out, "w_pp": w_ple_proj, "w_pg": w_ple_gate}
    mw = {"dense_f32": dense_f32,
          "dense_bf16": {k: v.astype(BF16) for k, v in dense_f32.items()},
          "w_g": w_exp_gate, "w_u": w_exp_up, "w_d": w_exp_down,
          "ln1_g": ln1_gain, "ln1_b": ln1_bias, "ln2_g": ln2_gain, "ln2_b": ln2_bias}
    pp = p_prompt.reshape(depth, bt * t, -1)
    ps = p_sample.reshape(depth, bs, -1)

    xp = x_prompt.reshape(bt * t, d_model)
    xs = x_sample.reshape(bs, d_model)
    outs = [[] for _ in range(6)]
    for i in range(depth):
        lw = _layer_operands(i, lb_all, s5_lambda_re, s5_lambda_im, s5_log_dt, s5_b_re, s5_b_im, s5_c_re,
                             s5_c_im, s5_d, hg_norm_gain, w_group_router, b_group_router, w_expert_router,
                             b_expert_router)
        xp, a_re, a_im, a_hg = _layer_prompt(xp, pp, mw, lw, i, alpha, bt, t)
        xs, b_re, b_im, b_hg = _layer_sample(
            xs, ps, state_s5_re[i].reshape(bs, n_groups * n_state), state_s5_im[i].reshape(bs, n_groups * n_state),
            state_hgrn, mw, lw, i, alpha)
        for lst, val in zip(outs, (a_re.reshape(bt, n_groups, n_state), a_im.reshape(bt, n_groups, n_state), a_hg,
                                   b_re.reshape(bs, n_groups, n_state), b_im.reshape(bs, n_groups, n_state), b_hg)):
            lst.append(val)

    return (xp.reshape(bt, t, d_model), xs.reshape(bs, 1, d_model), *[jnp.stack(o) for o in outs])
```

```python
import functools
import math

import jax
import jax.numpy as jnp
from jax import lax
from jax.experimental import pallas as pl
from jax.experimental.pallas import tpu as pltpu

F32 = jnp.float32
BF16 = jnp.bfloat16
HIGHEST = lax.Precision.HIGHEST

LANES = 128
SUBLANES = 8
VMEM_LIMIT_BYTES = 56 * 1024 * 1024

S5_GROUP = 16
S5_STATE = 64
S5_CHUNK = 16
HG_HEAD_DIM = 128
HG_CHUNK = 64
HG_SUB = 16
N_GROUPS = 4
EXPERTS_PER_GROUP = 8
N_EXPERTS = N_GROUPS * EXPERTS_PER_GROUP
LN_EPS = 1e-5
RMS_EPS = 1e-6


def _params(semantics):
    return pltpu.CompilerParams(dimension_semantics=semantics, vmem_limit_bytes=VMEM_LIMIT_BYTES)


def _dot(a, b):
    return jnp.dot(a, b, preferred_element_type=F32)


def _mm(a, w):
    if w.dtype == F32:
        return jnp.dot(a, w, precision=HIGHEST, preferred_element_type=F32)
    return jnp.dot(a.astype(BF16), w, preferred_element_type=F32)


def _dot_nt(a, b):
    return lax.dot_general(a, b, (((1,), (1,)), ((), ())), preferred_element_type=F32)


def _dot_tn(a, b):
    return lax.dot_general(a, b, (((0,), (0,)), ((), ())), preferred_element_type=F32)


def _sigmoid(x):
    return jax.nn.sigmoid(x)


def _silu(x):
    return x * jax.nn.sigmoid(x)


def _layer_norm_rows(x, gain, bias):
    mu = jnp.mean(x, axis=-1, keepdims=True)
    xc = x - mu
    var = jnp.mean(xc * xc, axis=-1, keepdims=True)
    return xc * lax.rsqrt(var + LN_EPS) * gain + bias


INPROJ_ROWS = 1024


def _inproj_kernel(x_ref, w_ref, o_ref, xb_ref):
    if w_ref.dtype == F32:
        o_ref[...] = _mm(x_ref[...], w_ref[...])
        return

    @pl.when(pl.program_id(1) == 0)
    def _():
        xb_ref[...] = x_ref[...].astype(BF16)

    o_ref[...] = _dot(xb_ref[...], w_ref[...])


def _layer_spec(block, layer, index_map):
    return pl.BlockSpec((None,) + block, lambda *idx: (layer,) + index_map(*idx))


def _inproj(x, w, layer, tm, tn=1024):
    m, k = x.shape
    n = w.shape[-1]
    return pl.pallas_call(
        _inproj_kernel,
        grid=(m // tm, n // tn),
        in_specs=[pl.BlockSpec((tm, k), lambda i, j: (i, 0)),
                  _layer_spec((k, tn), layer, lambda i, j: (0, j))],
        out_specs=pl.BlockSpec((tm, tn), lambda i, j: (i, j)),
        out_shape=jax.ShapeDtypeStruct((m, n), F32),
        scratch_shapes=[pltpu.VMEM((tm, k), BF16)],
        compiler_params=_params(("parallel", "arbitrary")),
        name="inproj",
    )(x, w)


def _glu_kernel(a_ref, w_ref, o_ref):
    a = a_ref[...]
    o_ref[...] = a * _sigmoid(_mm(a, w_ref[...]))


def _glu(a, w, layer, tm):
    m, k = a.shape
    return pl.pallas_call(
        _glu_kernel,
        grid=(m // tm,),
        in_specs=[pl.BlockSpec((tm, k), lambda i: (i, 0)),
                  _layer_spec((k, k), layer, lambda i: (0, 0))],
        out_specs=pl.BlockSpec((tm, k), lambda i: (i, 0)),
        out_shape=jax.ShapeDtypeStruct((m, k), F32),
        compiler_params=_params(("parallel",)),
        name="s5_glu",
    )(a, w)


def _merge_kernel(ya_ref, yb_ref, ga_ref, gb_ref, wa_ref, wb_ref, o_ref):
    pa = _mm(ya_ref[...], wa_ref[...])
    pb = _mm(yb_ref[...], wb_ref[...])
    o_ref[...] = _sigmoid(ga_ref[...]) * pa + _sigmoid(gb_ref[...]) * pb


def _merge(ya, yb, z, wa, wb, layer, tm, tn=1024):
    m, k = ya.shape
    n = wa.shape[-1]
    nj = n // tn
    return pl.pallas_call(
        _merge_kernel,
        grid=(m // tm, nj),
        in_specs=[pl.BlockSpec((tm, k), lambda i, j: (i, 0)),
                  pl.BlockSpec((tm, k), lambda i, j: (i, 0)),
                  pl.BlockSpec((tm, tn), lambda i, j: (i, 5 + j)),
                  pl.BlockSpec((tm, tn), lambda i, j: (i, 5 + nj + j)),
                  _layer_spec((k, tn), layer, lambda i, j: (0, j)),
                  _layer_spec((k, tn), layer, lambda i, j: (0, j))],
        out_specs=pl.BlockSpec((tm, tn), lambda i, j: (i, j)),
        out_shape=jax.ShapeDtypeStruct((m, n), F32),
        compiler_params=_params(("parallel", "arbitrary")),
        name="branch_merge",
    )(ya, yb, z, z, wa, wb)


def _ln1_kernel(mg_ref, x_ref, w_ref, g_ref, b_ref, o_ref, *, alpha):
    r = alpha * x_ref[...] + _mm(mg_ref[...], w_ref[...])
    o_ref[...] = _layer_norm_rows(r, g_ref[...], b_ref[...])


def _ln1(merged, x, w, gain, bias, layer, alpha, tm):
    m, d = x.shape
    row = pl.BlockSpec((tm, d), lambda i: (i, 0))
    vec = _layer_spec((1, d), layer, lambda i: (0, 0))
    return pl.pallas_call(
        functools.partial(_ln1_kernel, alpha=alpha),
        grid=(m // tm,),
        in_specs=[row, row, _layer_spec((d, d), layer, lambda i: (0, 0)), vec, vec],
        out_specs=row,
        out_shape=jax.ShapeDtypeStruct((m, d), F32),
        compiler_params=_params(("parallel",)),
        name="outproj_ln1",
    )(merged, x, w, gain[:, None, :], bias[:, None, :])


def _final_kernel(h_ref, moe_ref, p_ref, wg_ref, wp_ref, g_ref, b_ref, o_ref, *, alpha):
    h = h_ref[...]
    gate = _sigmoid(_mm(h, wg_ref[...]))
    proj = _mm(p_ref[...], wp_ref[...])
    r = alpha * h + moe_ref[...] + gate * proj
    o_ref[...] = _layer_norm_rows(r, g_ref[...], b_ref[...])


def _final(h, moe, p, wg, wp, gain, bias, layer, alpha, tm):
    m, d = h.shape
    dp = p.shape[-1]
    row = pl.BlockSpec((tm, d), lambda i: (i, 0))
    vec = _layer_spec((1, d), layer, lambda i: (0, 0))
    return pl.pallas_call(
        functools.partial(_final_kernel, alpha=alpha),
        grid=(m // tm,),
        in_specs=[row, row, _layer_spec((tm, dp), layer, lambda i: (i, 0)),
                  _layer_spec((d, d), layer, lambda i: (0, 0)),
                  _layer_spec((dp, d), layer, lambda i: (0, 0)), vec, vec],
        out_specs=row,
        out_shape=jax.ShapeDtypeStruct((m, d), F32),
        compiler_params=_params(("parallel",)),
        name="ple_ln2",
    )(h, moe, p, wg, wp, gain[:, None, :], bias[:, None, :])


def _route(h, w, b):
    logits = jnp.dot(h, w, precision=HIGHEST, preferred_element_type=F32) + b
    lane = lax.broadcasted_iota(jnp.int32, logits.shape, 1)
    lane_f = lane.astype(F32)
    neg = -jnp.inf
    gl = jnp.where(lane >= N_EXPERTS, jnp.where(lane < N_EXPERTS + N_GROUPS, logits, neg), neg)
    gmax = jnp.max(gl, axis=-1, keepdims=True)
    denom = jnp.sum(jnp.exp(gl - gmax), axis=-1, keepdims=True)
    grp_p = 1.0 / denom
    gidx = jnp.min(jnp.where(gl == gmax, lane_f, 1e9), axis=-1, keepdims=True) - N_EXPERTS
    lane_grp = (lane // EXPERTS_PER_GROUP).astype(F32)
    el = jnp.where(lane < N_EXPERTS, jnp.where(lane_grp == gidx, logits, neg), neg)
    t1 = jnp.max(el, axis=-1, keepdims=True)
    i1 = jnp.min(jnp.where(el == t1, lane_f, 1e9), axis=-1, keepdims=True)
    el2 = jnp.where(lane_f == i1, neg, el)
    t2 = jnp.max(el2, axis=-1, keepdims=True)
    i2 = jnp.min(jnp.where(el2 == t2, lane_f, 1e9), axis=-1, keepdims=True)
    e2 = jnp.exp(t2 - t1)
    w1 = 1.0 / (1.0 + e2)
    w2 = e2 * w1
    return lane, lane_f, i1, i2, grp_p * w1, grp_p * w2


def _router_dense_kernel(h_ref, w_ref, b_ref, o_ref):
    _, lane_f, i1, i2, c1, c2 = _route(h_ref[...], w_ref[...], b_ref[...])
    o_ref[...] = jnp.where(lane_f == i1, c1, jnp.where(lane_f == i2, c2, 0.0))


def _router_dense(h, w_r, b_r, tm):
    m, d = h.shape
    return pl.pallas_call(
        _router_dense_kernel,
        grid=(m // tm,),
        in_specs=[pl.BlockSpec((tm, d), lambda i: (i, 0)),
                  pl.BlockSpec((d, LANES), lambda i: (0, 0)),
                  pl.BlockSpec((1, LANES), lambda i: (0, 0))],
        out_specs=pl.BlockSpec((tm, LANES), lambda i: (i, 0)),
        out_shape=jax.ShapeDtypeStruct((m, LANES), F32),
        compiler_params=_params(("parallel",)),
        name="router_dense",
    )(h, w_r, b_r)


def _moe_dense_kernel(h_ref, c_ref, wg_ref, wu_ref, wd_ref, o_ref):
    e = pl.program_id(1)

    @pl.when(e == 0)
    def _():
        o_ref[...] = jnp.zeros_like(o_ref)

    comb = c_ref[...]
    lane = lax.broadcasted_iota(jnp.int32, comb.shape, 1)
    col = jnp.sum(jnp.where(lane == e, comb, 0.0), axis=-1, keepdims=True)
    h = h_ref[...]
    hid = _silu(_mm(h, wg_ref[0])) * _mm(h, wu_ref[0]) * col
    o_ref[...] += _mm(hid, wd_ref[0])


def _moe_dense(h, comb, wg, wu, wd, layer, tm):
    m, d = h.shape
    _, n_e, _, f = wg.shape
    return pl.pallas_call(
        _moe_dense_kernel,
        grid=(m // tm, n_e),
        in_specs=[pl.BlockSpec((tm, d), lambda i, e: (i, 0)),
                  pl.BlockSpec((tm, LANES), lambda i, e: (i, 0)),
                  _layer_spec((1, d, f), layer, lambda i, e: (e, 0, 0)),
                  _layer_spec((1, d, f), layer, lambda i, e: (e, 0, 0)),
                  _layer_spec((1, f, d), layer, lambda i, e: (e, 0, 0))],
        out_specs=pl.BlockSpec((tm, d), lambda i, e: (i, 0)),
        out_shape=jax.ShapeDtypeStruct((m, d), F32),
        compiler_params=_params(("parallel", "arbitrary")),
        name="moe_dense",
    )(h, comb, wg, wu, wd)


MOE_TILE = 256


def _moe_num_tiles(m):
    return 2 * m // MOE_TILE + N_EXPERTS


def _router_sorted_kernel(h_ref, w_ref, b_ref, s1_ref, s2_ref, cw_ref, cnt_ref, carry_sc):
    @pl.when(pl.program_id(0) == 0)
    def _():
        carry_sc[...] = jnp.zeros_like(carry_sc)

    lane, lane_f, i1, i2, c1, c2 = _route(h_ref[...], w_ref[...], b_ref[...])
    sel1 = jnp.where(lane_f == i1, 1.0, 0.0)
    sel2 = jnp.where(lane_f == i2, 1.0, 0.0)
    cnt = sel1 + sel2
    tm = cnt.shape[0]
    strict_lower = (lax.broadcasted_iota(jnp.int32, (tm, tm), 0) > lax.broadcasted_iota(jnp.int32, (tm, tm), 1))
    prefix = _dot(strict_lower.astype(BF16), cnt.astype(BF16)) + carry_sc[0:1, :]
    s1_ref[...] = sel1 * (prefix + 1.0)
    s2_ref[...] = sel2 * (prefix + 1.0)
    cw_ref[...] = jnp.where(lane == 0, c1, jnp.where(lane == 1, c2, 0.0))
    carry_sc[...] = carry_sc[...] + jnp.sum(cnt, axis=0, keepdims=True)
    cnt_ref[...] = carry_sc[...]


def _router_sorted(h, w_r, b_r, tm):
    m, d = h.shape
    tok = pl.BlockSpec((tm, LANES), lambda i: (i, 0))
    return pl.pallas_call(
        _router_sorted_kernel,
        grid=(m // tm,),
        in_specs=[pl.BlockSpec((tm, d), lambda i: (i, 0)),
                  pl.BlockSpec((d, LANES), lambda i: (0, 0)),
                  pl.BlockSpec((1, LANES), lambda i: (0, 0))],
        out_specs=[tok, tok, tok, pl.BlockSpec((SUBLANES, LANES), lambda i: (0, 0))],
        out_shape=[jax.ShapeDtypeStruct((m, LANES), F32)] * 3 + [jax.ShapeDtypeStruct((SUBLANES, LANES), F32)],
        scratch_shapes=[pltpu.VMEM((SUBLANES, LANES), F32)],
        compiler_params=_params(("arbitrary",)),
        name="router_sorted",
    )(h, w_r, b_r)


def _plan_kernel(s1_ref, s2_ref, cnt_ref, pos_ref, tab_ref):
    te = float(MOE_TILE)
    lane = lax.broadcasted_iota(jnp.int32, (1, LANES), 1)
    cnt = jnp.where(lane < N_EXPERTS, cnt_ref[0:1, :], 0.0)
    padded = jnp.floor((cnt + (te - 1.0)) * (1.0 / te)) * te
    r128 = lax.broadcasted_iota(jnp.int32, (LANES, LANES), 0)
    c128 = lax.broadcasted_iota(jnp.int32, (LANES, LANES), 1)
    before = jnp.where(r128 < c128, 1.0, 0.0)
    off = jnp.dot(jnp.broadcast_to(padded, (SUBLANES, LANES)), before, precision=HIGHEST,
                  preferred_element_type=F32)[0:1, :]
    s1 = s1_ref[...]
    s2 = s2_ref[...]
    v1 = jnp.where(s1 > 0.0, s1 - 1.0 + off, 0.0)
    v2 = jnp.where(s2 > 0.0, s2 - 1.0 + off, 0.0)
    ones8 = jnp.ones((SUBLANES, LANES), F32)
    p1 = lax.dot_general(ones8, v1, (((1,), (1,)), ((), ())), precision=HIGHEST, preferred_element_type=F32)
    p2 = lax.dot_general(ones8, v2, (((1,), (1,)), ((), ())), precision=HIGHEST, preferred_element_type=F32)
    row8 = lax.broadcasted_iota(jnp.int32, p1.shape, 0)
    pos_ref[...] = jnp.where(row8 == 0, p1, p2).astype(jnp.int32)

    def per_expert(row):
        return jnp.broadcast_to(row, (LANES, LANES)).T
    off_e = per_expert(off)
    end_e = per_expert(off + padded)
    cnt_e = per_expert(cnt)
    start = c128.astype(F32) * te
    is_e = r128 < N_EXPERTS
    tile_e = jnp.sum(jnp.where(is_e, jnp.where(end_e <= start, 1.0, 0.0), 0.0), axis=0, keepdims=True)
    rows = jnp.clip(cnt_e - (start - off_e), 0.0, te)
    owns = jnp.where(is_e, jnp.where(off_e <= start, jnp.where(start < end_e, rows, 0.0), 0.0), 0.0)
    n_rows = jnp.sum(owns, axis=0, keepdims=True)
    tile_e = jnp.minimum(tile_e, float(N_EXPERTS - 1))
    n_used = jnp.sum(jnp.where(n_rows > 0.0, 1.0, 0.0), axis=-1, keepdims=True)
    row_t = lax.broadcasted_iota(jnp.int32, (SUBLANES, LANES), 0)
    tab_ref[...] = jnp.where(row_t == 0, tile_e, jnp.where(row_t == 1, n_rows, n_used)).astype(jnp.int32)


def _plan(s1, s2, counts, tm):
    m = s1.shape[0]
    tok = pl.BlockSpec((tm, LANES), lambda i: (i, 0))
    return pl.pallas_call(
        _plan_kernel,
        grid=(m // tm,),
        in_specs=[tok, tok, pl.BlockSpec((SUBLANES, LANES), lambda i: (0, 0))],
        out_specs=[pl.BlockSpec((SUBLANES, tm), lambda i: (0, i)),
                   pl.BlockSpec((SUBLANES, LANES), lambda i: (0, 0))],
        out_shape=[jax.ShapeDtypeStruct((SUBLANES, m), jnp.int32),
                   jax.ShapeDtypeStruct((SUBLANES, LANES), jnp.int32)],
        compiler_params=_params(("arbitrary",)),
        name="moe_plan",
    )(s1, s2, counts)


DMA_UNROLL = 8


def _dispatch_kernel(pos_ref, rows_ref, h_ref, xs_hbm, zero_buf, zero_sem, sem, *, m, n_tiles):
    i = pl.program_id(0)
    tm = h_ref.shape[0]

    @pl.when(i == 0)
    def _():
        zero_buf[...] = jnp.zeros_like(zero_buf)

        def zero_copy(j):
            return pltpu.make_async_copy(zero_buf, xs_hbm.at[pl.ds(j * MOE_TILE, MOE_TILE)], zero_sem)

        def zero_start(j, c):
            @pl.when(rows_ref[j] < MOE_TILE)
            def _():
                zero_copy(j).start()
            return c

        def zero_wait(j, c):
            @pl.when(rows_ref[j] < MOE_TILE)
            def _():
                zero_copy(j).wait()
            return c

        lax.fori_loop(0, n_tiles, zero_start, 0)
        lax.fori_loop(0, n_tiles, zero_wait, 0)

    def body(r, c):
        t = i * tm + r
        src = h_ref.at[pl.ds(r, 1)]
        pltpu.make_async_copy(src, xs_hbm.at[pl.ds(pos_ref[t], 1)], sem).start()
        pltpu.make_async_copy(src, xs_hbm.at[pl.ds(pos_ref[m + t], 1)], sem).start()
        return c
    lax.fori_loop(0, tm, body, 0, unroll=DMA_UNROLL)

    pltpu.make_async_copy(xs_hbm.at[pl.ds(0, 2 * tm)], xs_hbm.at[pl.ds(0, 2 * tm)], sem).wait()


def _dispatch(h, pos, tile_rows, tm):
    m, d = h.shape
    n_tiles = _moe_num_tiles(m)
    grid_spec = pltpu.PrefetchScalarGridSpec(
        num_scalar_prefetch=2,
        grid=(m // tm,),
        in_specs=[pl.BlockSpec((tm, d), lambda i, pos, rows: (i, 0))],
        out_specs=pl.BlockSpec(memory_space=pl.ANY),
        scratch_shapes=[pltpu.VMEM((MOE_TILE, d), F32), pltpu.SemaphoreType.DMA(()),
                        pltpu.SemaphoreType.DMA(())],
    )
    return pl.pallas_call(
        functools.partial(_dispatch_kernel, m=m, n_tiles=n_tiles),
        grid_spec=grid_spec,
        out_shape=jax.ShapeDtypeStruct((n_tiles * MOE_TILE, d), F32),
        compiler_params=_params(("arbitrary",)),
        name="moe_dispatch",
    )(pos, tile_rows, h)


def _experts_kernel(te_ref, nr_ref, nu_ref, x_ref, wg_ref, wu_ref, wd_ref, y_ref, wgb, wub, wdb):
    j = pl.program_id(0)
    new_expert = jnp.logical_or(j == 0, te_ref[j] != te_ref[jnp.maximum(j - 1, 0)])

    @pl.when(jnp.logical_and(new_expert, nr_ref[j] > 0))
    def _():
        wgb[...] = wg_ref[0].astype(BF16)
        wub[...] = wu_ref[0].astype(BF16)
        wdb[...] = wd_ref[0].astype(BF16)

    @pl.when(nr_ref[j] > 0)
    def _():
        x = x_ref[...].astype(BF16)
        hid = _silu(_dot(x, wgb[...])) * _dot(x, wub[...])
        y_ref[...] = _dot(hid.astype(BF16), wdb[...])

    @pl.when(nr_ref[j] == 0)
    def _():
        y_ref[...] = jnp.zeros_like(y_ref)


def _experts_sorted(xs, tile_expert, tile_rows, n_used, wg, wu, wd, layer):
    n_rows, d = xs.shape
    _, n_e, _, f = wg.shape
    n_tiles = n_rows // MOE_TILE

    def used_tile(j, te, nr, nu):
        return (jnp.minimum(j, nu[0] - 1), 0)

    def expert_block(j, te, nr, nu):
        return (layer, te[j], 0, 0)

    grid_spec = pltpu.PrefetchScalarGridSpec(
        num_scalar_prefetch=3,
        grid=(n_tiles,),
        in_specs=[pl.BlockSpec((MOE_TILE, d), used_tile),
                  pl.BlockSpec((None, 1, d, f), expert_block),
                  pl.BlockSpec((None, 1, d, f), expert_block),
                  pl.BlockSpec((None, 1, f, d), expert_block)],
        out_specs=pl.BlockSpec((MOE_TILE, d), lambda j, te, nr, nu: (j, 0)),
        scratch_shapes=[pltpu.VMEM((d, f), BF16), pltpu.VMEM((d, f), BF16), pltpu.VMEM((f, d), BF16)],
    )
    return pl.pallas_call(
        _experts_kernel,
        grid_spec=grid_spec,
        out_shape=jax.ShapeDtypeStruct((n_rows, d), F32),
        compiler_params=_params(("arbitrary",)),
        name="moe_experts_sorted",
    )(tile_expert, tile_rows, n_used, xs, wg, wu, wd)


def _final_gather_kernel(pos_ref, h_ref, cw_ref, p_ref, wg_ref, wp_ref, g_ref, b_ref, ys_hbm, o_ref,
                         ybuf, sem, *, alpha, m):
    i = pl.program_id(0)
    slot = i % 2
    tm = h_ref.shape[0]

    def gather(tile, dst_slot):
        def body(r, c):
            t = tile * tm + r
            for pick in range(2):
                pltpu.make_async_copy(ys_hbm.at[pl.ds(pos_ref[pick * m + t], 1)],
                                      ybuf.at[dst_slot, pl.ds(pick * tm + r, 1)], sem.at[dst_slot]).start()
            return c
        lax.fori_loop(0, tm, body, 0, unroll=DMA_UNROLL)

    @pl.when(i == 0)
    def _():
        gather(0, 0)

    @pl.when(i + 1 < pl.num_programs(0))
    def _():
        gather(i + 1, 1 - slot)

    h = h_ref[...]
    gate = _sigmoid(_mm(h, wg_ref[...]))
    proj = _mm(p_ref[...], wp_ref[...])

    pltpu.make_async_copy(ys_hbm.at[pl.ds(0, 2 * tm)], ybuf.at[slot], sem.at[slot]).wait()

    cw = cw_ref[...]
    moe = cw[:, 0:1] * ybuf[slot, 0:tm] + cw[:, 1:2] * ybuf[slot, tm:2 * tm]
    r = alpha * h + moe + gate * proj
    o_ref[...] = _layer_norm_rows(r, g_ref[...], b_ref[...])


def _final_gather(h, cw, p, pos, ys, wg, wp, gain, bias, layer, alpha, tm):
    m, d = h.shape
    dp = p.shape[-1]
    row = pl.BlockSpec((tm, d), lambda i, pos: (i, 0))
    vec = _layer_spec((1, d), layer, lambda i, pos: (0, 0))
    grid_spec = pltpu.PrefetchScalarGridSpec(
        num_scalar_prefetch=1,
        grid=(m // tm,),
        in_specs=[row, pl.BlockSpec((tm, LANES), lambda i, pos: (i, 0)),
                  _layer_spec((tm, dp), layer, lambda i, pos: (i, 0)),
                  _layer_spec((d, d), layer, lambda i, pos: (0, 0)),
                  _layer_spec((dp, d), layer, lambda i, pos: (0, 0)), vec, vec,
                  pl.BlockSpec(memory_space=pl.ANY)],
        out_specs=row,
        scratch_shapes=[pltpu.VMEM((2, 2 * tm, d), F32), pltpu.SemaphoreType.DMA((2,))],
    )
    return pl.pallas_call(
        functools.partial(_final_gather_kernel, alpha=alpha, m=m),
        grid_spec=grid_spec,
        out_shape=jax.ShapeDtypeStruct((m, d), F32),
        compiler_params=_params(("arbitrary",)),
        name="ple_ln2_gather",
    )(pos, h, cw, p, wg, wp, gain[:, None, :], bias[:, None, :], ys)


def _s5_discretise(lam_re, lam_im, log_dt, b_re, b_im):
    dt = jnp.exp(log_dt)[:, None]
    mag = jnp.exp(lam_re * dt)
    ab_re = mag * jnp.cos(lam_im * dt)
    ab_im = mag * jnp.sin(lam_im * dt)
    den = lam_re * lam_re + lam_im * lam_im
    nr = ab_re - 1.0
    zr = (nr * lam_re + ab_im * lam_im) / den
    zi = (ab_im * lam_re - nr * lam_im) / den
    bb_re = zr[..., None] * b_re - zi[..., None] * b_im
    bb_im = zr[..., None] * b_im + zi[..., None] * b_re
    return ab_re, ab_im, bb_re, bb_im


def _s5_prompt_operators(ab_re, ab_im, bb_re, bb_im, c_re, c_im, d_skip):
    g, n = ab_re.shape
    cch = bb_re.shape[-1]
    L = S5_CHUNK
    pw_re = [jnp.ones_like(ab_re)]
    pw_im = [jnp.zeros_like(ab_im)]
    for _ in range(L):
        pr, pi = pw_re[-1], pw_im[-1]
        pw_re.append(pr * ab_re - pi * ab_im)
        pw_im.append(pr * ab_im + pi * ab_re)
    a_re = jnp.stack(pw_re)
    a_im = jnp.stack(pw_im)
    w_re = a_re[:L, :, :, None] * bb_re - a_im[:L, :, :, None] * bb_im
    w_im = a_re[:L, :, :, None] * bb_im + a_im[:L, :, :, None] * bb_re
    kern = (jnp.einsum('gcn,kgnd->kgcd', c_re, w_re, precision=HIGHEST)
            - jnp.einsum('gcn,kgnd->kgcd', c_im, w_im, precision=HIGHEST))
    steps = jnp.arange(L)
    lag_is = (steps[None, None, :] - steps[None, :, None] == steps[:, None, None]).astype(F32)
    toe = jnp.einsum('kio,kgcd->gidoc', lag_is, kern, precision=HIGHEST)
    toe = toe.reshape(g, L * cch, L * cch)

    p_re = w_re[::-1].transpose(1, 0, 3, 2).reshape(g, L * cch, n)
    p_im = w_im[::-1].transpose(1, 0, 3, 2).reshape(g, L * cch, n)
    q_re = (jnp.einsum('gcn,tgn->gntc', c_re, a_re[1:]) - jnp.einsum('gcn,tgn->gntc', c_im, a_im[1:]))
    q_im = -(jnp.einsum('gcn,tgn->gntc', c_re, a_im[1:]) + jnp.einsum('gcn,tgn->gntc', c_im, a_re[1:]))
    q_re = q_re.reshape(g, n, L * cch)
    q_im = q_im.reshape(g, n, L * cch)

    hp = g // 2
    w = L * cch
    eye2 = jnp.eye(2, dtype=F32)
    p_ri = jnp.stack([p_re, p_im], axis=2).reshape(hp, 2, w, 2, n)
    q_ri = jnp.stack([q_re, q_im], axis=1).reshape(hp, 2, 2, n, w)
    p2 = (p_ri[:, :, :, :, None, :] * eye2[None, :, None, None, :, None]).reshape(hp, 2 * w, 4 * n)
    q2 = (q_ri.transpose(0, 2, 1, 3, 4)[:, :, :, :, None, :]
          * eye2[None, None, :, None, :, None]).reshape(hp, 4 * n, 2 * w)
    a_chunk = jnp.stack([a_re[L].reshape(hp, 2 * n), a_im[L].reshape(hp, 2 * n)], axis=1)
    d2 = jnp.broadcast_to(d_skip.reshape(hp, 2, 1, cch), (hp, 2, L, cch)).reshape(hp, 1, 2 * w)
    return toe.astype(BF16), p2.astype(BF16), q2.astype(BF16), a_chunk, d2


def _s5_prompt_kernel(u_ref, t_ref, p_ref, q_ref, a_ref, d_ref, y_ref, fre_ref, fim_ref,
                      s_sc, xin_sc, *, n_chunks, bt):
    half = a_ref.shape[-1]
    u0 = jnp.concatenate([u_ref[0, 0], u_ref[0, 1]], axis=1)
    u1 = jnp.concatenate([u_ref[1, 0], u_ref[1, 1]], axis=1)
    ub0 = u0.astype(BF16)
    ub1 = u1.astype(BF16)
    y_intra = jnp.concatenate([_dot(ub0, t_ref[0]), _dot(ub1, t_ref[1])], axis=1)
    s_sc[...] = _dot(jnp.concatenate([ub0, ub1], axis=1), p_ref[0])
    ar = a_ref[0, 0:1, :]
    ai = a_ref[0, 1:2, :]
    xr = jnp.zeros((bt, half), F32)
    xi = jnp.zeros((bt, half), F32)
    for j in range(n_chunks):
        rows = slice(j * bt, (j + 1) * bt)
        xin_sc[rows, 0:half] = xr
        xin_sc[rows, half:2 * half] = xi
        sr = s_sc[rows, 0:half]
        si = s_sc[rows, half:2 * half]
        xr, xi = ar * xr - ai * xi + sr, ar * xi + ai * xr + si
    fre_ref[...] = xr
    fim_ref[...] = xi
    y_carry = _dot(xin_sc[...].astype(BF16), q_ref[0])
    y = y_intra + y_carry + d_ref[0] * jnp.concatenate([u0, u1], axis=1)
    y = jax.nn.gelu(y)
    for k in range(4):
        y_ref[k // 2, k % 2] = y[:, k * LANES:(k + 1) * LANES]


S5_RELAYOUT_CHUNKS = 32
S5_RELAYOUT_GROUPS = LANES // S5_GROUP


def _to_chunks_kernel(x_ref, o_ref, *, bt):
    b = pl.program_id(2)
    L, cch = S5_CHUNK, S5_GROUP
    nj = x_ref.shape[0] // L
    per_tile = LANES // cch
    at_pos =[x_ref[pl.ds(p, nj, stride=L), :] for p in range(L)]
    for g in range(S5_RELAYOUT_GROUPS):
        for h in range(L // per_tile):
            piece = jnp.concatenate([at_pos[h * per_tile + p][:, g * cch:(g + 1) * cch] for p in range(per_tile)],
                                    axis=1)
            o_ref[g, h, pl.ds(b, nj, stride=bt), :] = piece


def _from_chunks_kernel(y_ref, o_ref, *, bt):
    b = pl.program_id(2)
    L, cch = S5_CHUNK, S5_GROUP
    nj = o_ref.shape[0] // L
    per_tile = LANES // cch
    tiles = [[y_ref[g, h, pl.ds(b, nj, stride=bt), :] for h in range(L // per_tile)]
             for g in range(S5_RELAYOUT_GROUPS)]
    for p in range(L):
        h, q = divmod(p, per_tile)
        o_ref[pl.ds(p, nj, stride=L), :] = jnp.concatenate(
            [tiles[g][h][:, q * cch:(q + 1) * cch] for g in range(S5_RELAYOUT_GROUPS)], axis=1)


def _to_chunks(z, g, bt, t):
    L, cch, gt = S5_CHUNK, S5_GROUP, S5_RELAYOUT_GROUPS
    nj = min(S5_RELAYOUT_CHUNKS, t // L)
    n_chunks = t // L
    nb = n_chunks // nj
    halves = L * cch // LANES
    return pl.pallas_call(
        functools.partial(_to_chunks_kernel, bt=bt),
        grid=(nb, g // gt, bt),
        in_specs=[pl.BlockSpec((nj * L, LANES), lambda j, lt, b: (b * nb + j, lt))],
        out_specs=pl.BlockSpec((gt, halves, nj * bt, LANES), lambda j, lt, b: (lt, 0, j, 0)),
        out_shape=jax.ShapeDtypeStruct((g, halves, n_chunks * bt, LANES), F32),
        compiler_params=_params(("parallel", "parallel", "arbitrary")),
        name="s5_to_chunks",
    )(z)


def _from_chunks(y_t, bt, t):
    L, cch, gt = S5_CHUNK, S5_GROUP, S5_RELAYOUT_GROUPS
    nj = min(S5_RELAYOUT_CHUNKS, t // L)
    g, halves = y_t.shape[:2]
    n_chunks = t // L
    nb = n_chunks // nj
    return pl.pallas_call(
        functools.partial(_from_chunks_kernel, bt=bt),
        grid=(nb, g // gt, bt),
        in_specs=[pl.BlockSpec((gt, halves, nj * bt, LANES), lambda j, lt, b: (lt, 0, j, 0))],
        out_specs=pl.BlockSpec((nj * L, LANES), lambda j, lt, b: (b * nb + j, lt)),
        out_shape=jax.ShapeDtypeStruct((bt * t, g * cch), F32),
        compiler_params=_params(("parallel", "parallel", "arbitrary")),
        name="s5_from_chunks",
    )(y_t)


def _s5_prompt(z, ops, bt, t):
    toe, p2, q2, a_chunk, d2 = ops
    g = toe.shape[0]
    hp = g // 2
    L = S5_CHUNK
    cch = S5_GROUP
    n_chunks = t // L
    r = n_chunks * bt
    w = L * cch
    half = a_chunk.shape[-1]
    u_t = _to_chunks(z, g, bt, t)
    y_t, f_re, f_im = pl.pallas_call(
        functools.partial(_s5_prompt_kernel, n_chunks=n_chunks, bt=bt),
        grid=(hp,),
        in_specs=[pl.BlockSpec((2, w // LANES, r, LANES), lambda i: (i, 0, 0, 0)),
                  pl.BlockSpec((2, w, w), lambda i: (i, 0, 0)),
                  pl.BlockSpec((1, 2 * w, 2 * half), lambda i: (i, 0, 0)),
                  pl.BlockSpec((1, 2 * half, 2 * w), lambda i: (i, 0, 0)),
                  pl.BlockSpec((1, 2, half), lambda i: (i, 0, 0)),
                  pl.BlockSpec((1, 1, 2 * w), lambda i: (i, 0, 0))],
        out_specs=[pl.BlockSpec((2, w // LANES, r, LANES), lambda i: (i, 0, 0, 0)),
                   pl.BlockSpec((bt, half), lambda i: (0, i)),
                   pl.BlockSpec((bt, half), lambda i: (0, i))],
        out_shape=[jax.ShapeDtypeStruct((g, w // LANES, r, LANES), F32),
                   jax.ShapeDtypeStruct((bt, hp * half), F32),
                   jax.ShapeDtypeStruct((bt, hp * half), F32)],
        scratch_shapes=[pltpu.VMEM((r, 2 * half), F32), pltpu.VMEM((r, 2 * half), F32)],
        compiler_params=_params(("parallel",)),
        name="s5_prompt",
    )(u_t, toe, p2, q2, a_chunk, d2)
    return _from_chunks(y_t, bt, t), f_re, f_im


S5_SAMPLE_GROUPS = 8


def _s5_sample_operators(ab_re, ab_im, bb_re, bb_im, c_re, c_im, d_skip):
    g, n = ab_re.shape
    cch = bb_re.shape[-1]
    gb = S5_SAMPLE_GROUPS
    nb = g // gb
    eye = jnp.eye(gb, dtype=F32)
    b_ri = jnp.stack([bb_re, bb_im], axis=1).reshape(nb, gb, 2, n, cch)
    c_ri = jnp.stack([c_re, -c_im], axis=1).reshape(nb, gb, 2, cch, n)
    b8 = (b_ri.transpose(0, 1, 4, 2, 3)[:, :, :, :, None, :]
          * eye[None, :, None, None, :, None]).reshape(nb, gb * cch, 2 * gb * n)
    c8 = (c_ri.transpose(0, 2, 1, 4, 3)[:, :, :, :, None, :]
          * eye[None, None, :, None, :, None]).reshape(nb, 2 * gb * n, gb * cch)
    a8 = jnp.stack([ab_re.reshape(nb, gb * n), ab_im.reshape(nb, gb * n)], axis=1)
    d8 = d_skip.reshape(nb, 1, gb * cch)
    return b8, c8, a8, d8


def _s5_sample_kernel(u_ref, sr_ref, si_ref, b_ref, c_ref, a_ref, d_ref, y_ref, nr_ref, ni_ref):
    u = u_ref[...]
    half = sr_ref.shape[-1]
    bu = jnp.dot(u, b_ref[0], precision=HIGHEST, preferred_element_type=F32)
    ar = a_ref[0, 0:1, :]
    ai = a_ref[0, 1:2, :]
    sr = sr_ref[...]
    si = si_ref[...]
    xr = ar * sr - ai * si + bu[:, :half]
    xi = ar * si + ai * sr + bu[:, half:]
    nr_ref[...] = xr
    ni_ref[...] = xi
    y = jnp.dot(jnp.concatenate([xr, xi], axis=1), c_ref[0], precision=HIGHEST,
                preferred_element_type=F32) + d_ref[0] * u
    y_ref[...] = jax.nn.gelu(y)


def _s5_sample(z, s_re, s_im, ops):
    b8, c8, a8, d8 = ops
    nb = b8.shape[0]
    b = z.shape[0]
    wu = b8.shape[1]
    ws = a8.shape[-1]
    return pl.pallas_call(
        _s5_sample_kernel,
        grid=(nb,),
        in_specs=[pl.BlockSpec((b, wu), lambda i: (0, i)),
                  pl.BlockSpec((b, ws), lambda i: (0, i)),
                  pl.BlockSpec((b, ws), lambda i: (0, i)),
                  pl.BlockSpec((1, wu, 2 * ws), lambda i: (i, 0, 0)),
                  pl.BlockSpec((1, 2 * ws, wu), lambda i: (i, 0, 0)),
                  pl.BlockSpec((1, 2, ws), lambda i: (i, 0, 0)),
                  pl.BlockSpec((1, 1, wu), lambda i: (i, 0, 0))],
        out_specs=[pl.BlockSpec((b, wu), lambda i: (0, i)),
                   pl.BlockSpec((b, ws), lambda i: (0, i)),
                   pl.BlockSpec((b, ws), lambda i: (0, i))],
        out_shape=[jax.ShapeDtypeStruct((b, nb * wu), F32),
                   jax.ShapeDtypeStruct((b, nb * ws), F32),
                   jax.ShapeDtypeStruct((b, nb * ws), F32)],
        compiler_params=_params(("parallel",)),
        name="s5_sample",
    )(z, s_re, s_im, b8, c8, a8, d8)


HG_HEADS_PER_STEP = 2
HG_SAFE_SPAN = 60.0


def _hgrn_intra_factored(q, kk, v, g):
    c, sub = HG_CHUNK, HG_SUB
    n_sub = c // sub
    row = lax.broadcasted_iota(jnp.int32, (c, 1), 0)
    starts = [g[i * sub - 1:i * sub, :] if i else jnp.zeros_like(g[0:1, :]) for i in range(n_sub)]
    g_start_rows = jnp.concatenate([jnp.broadcast_to(s, (sub, s.shape[1])) for s in starts], axis=0)
    q_fac = (q * jnp.exp(g - g_start_rows)).astype(BF16)
    pad = jnp.zeros((LANES - c, kk.shape[1]), BF16)
    k_fac = []
    for i in range(n_sub):
        ki = kk * jnp.exp(jnp.where(row < (i + 1) * sub, starts[i] - g, -jnp.inf))
        k_fac += [ki.astype(BF16), pad]
    wide = _dot_nt(q_fac, jnp.concatenate(k_fac, axis=0))
    scores = jnp.concatenate([wide[i * sub:(i + 1) * sub, i * LANES:i * LANES + c] for i in range(n_sub)], axis=0)
    causal = lax.broadcasted_iota(jnp.int32, (c, c), 0) >= lax.broadcasted_iota(jnp.int32, (c, c), 1)
    return _dot(jnp.where(causal, scores, 0.0).astype(BF16), v.astype(BF16))


def _hgrn_intra_exact(q, kk, v, g):
    c, sub = HG_CHUNK, HG_SUB
    row = lax.broadcasted_iota(jnp.int32, (c, 1), 0)
    row_in_sub = row % sub
    neg = -jnp.inf
    blocks = [jnp.zeros((sub, c), F32)]
    for i in range(1, c // sub):
        g_start = g[i * sub - 1:i * sub, :]
        qi = q[i * sub:(i + 1) * sub, :] * jnp.exp(g[i * sub:(i + 1) * sub, :] - g_start)
        ki = kk * jnp.exp(jnp.where(row < i * sub, g_start - g, neg))
        blocks.append(_dot_nt(qi.astype(BF16), ki.astype(BF16)))
    o = _dot(jnp.concatenate(blocks, axis=0).astype(BF16), v.astype(BF16))
    o = o + jnp.sum(q * kk, axis=-1, keepdims=True) * v
    for d in range(1, sub):
        gs = pltpu.roll(g, d, 0)
        ks = pltpu.roll(kk, d, 0)
        vs = pltpu.roll(v, d, 0)
        dec = jnp.exp(jnp.where(row_in_sub >= d, g - gs, neg))
        o = o + jnp.sum(q * ks * dec, axis=-1, keepdims=True) * vs
    return o


def _hgrn_prompt_kernel(q_ref, f_ref, i_ref, g_ref, lb_ref, gain_ref, y_ref, sfin_ref, st_sc, inter_sc,
                        *, n_chunks):
    c = HG_CHUNK
    sub = HG_SUB
    hd = HG_HEAD_DIM
    n_h = st_sc.shape[0]
    t = pl.program_id(2)

    @pl.when(t == 0)
    def _():
        st_sc[...] = jnp.zeros_like(st_sc)

    gain = gain_ref[...]
    lb_all = lb_ref[...]
    tri = (lax.broadcasted_iota(jnp.int32, (c, c), 0) >= lax.broadcasted_iota(jnp.int32, (c, c), 1)).astype(BF16)

    def chunk(ci, carry):
        r0 = pl.multiple_of(ci * c, c)

        def forget_and_decay():
            f = lb_all + (1.0 - lb_all) * _sigmoid(f_ref[pl.ds(r0, c), :])
            log_f = jnp.log(f)
            hi = log_f.astype(BF16)
            rest = log_f - hi.astype(F32)
            mid = rest.astype(BF16)
            lo = (rest - mid.astype(F32)).astype(BF16)
            sums = _dot(tri, jnp.concatenate([hi, mid, lo], axis=1))
            w = n_h * hd
            return f, (sums[:, 2 * w:] + sums[:, w:2 * w]) + sums[:, :w]

        def operands(hh, f, g):
            lanes = slice(hh * hd, (hh + 1) * hd)
            return _silu(q_ref[pl.ds(r0, c), lanes]), 1.0 - f[:, lanes], i_ref[pl.ds(r0, c), lanes], g[:, lanes]

        def finish(o, hh):
            lanes = slice(hh * hd, (hh + 1) * hd)
            o = o * lax.rsqrt(jnp.mean(o * o, axis=-1, keepdims=True) + RMS_EPS) * gain
            y_ref[pl.ds(r0, c), lanes] = o * _silu(g_ref[pl.ds(r0, c), lanes])

        f_all, g_all = forget_and_decay()
        span = jnp.zeros((1, hd), F32)
        for hh in range(n_h):
            q, kk, v, g = operands(hh, f_all, g_all)
            for i in range(c // sub):
                g_start = g[i * sub - 1:i * sub, :] if i else jnp.zeros_like(g[0:1, :])
                span = jnp.maximum(span, g_start - g[(i + 1) * sub - 1:(i + 1) * sub, :])
            g_last = g[c - 1:c, :]
            st = st_sc[hh]
            inter = _dot_nt((q * jnp.exp(g)).astype(BF16), st.astype(BF16))
            inter_sc[hh] = inter
            finish(inter + _hgrn_intra_factored(q, kk, v, g), hh)
            k_dec = kk * jnp.exp(g_last - g)
            st_sc[hh] = st * jnp.exp(g_last) + _dot_tn(v.astype(BF16), k_dec.astype(BF16))

        @pl.when(jnp.max(span) > HG_SAFE_SPAN)
        def _():
            f_again, g_again = forget_and_decay()
            for hh in range(n_h):
                q, kk, v, g = operands(hh, f_again, g_again)
                finish(inter_sc[hh] + _hgrn_intra_exact(q, kk, v, g), hh)

        return carry

    lax.fori_loop(0, n_chunks, chunk, 0)

    @pl.when(t == pl.num_programs(2) - 1)
    def _():
        for hh in range(n_h):
            sfin_ref[0, hh] = st_sc[hh].T


def _hgrn_prompt(z, lb, gain, bt, t):
    hd = HG_HEAD_DIM
    n_heads = lb.shape[0] // hd
    n_h = HG_HEADS_PER_STEP
    groups = n_heads // n_h
    w = n_h * hd
    tb = min(512, t)
    nt = t // tb
    m = bt * t

    def col(block):
        return pl.BlockSpec((tb, w), lambda b, h, s: (b * nt + s, block * groups + h))

    return pl.pallas_call(
        functools.partial(_hgrn_prompt_kernel, n_chunks=tb // HG_CHUNK),
        grid=(bt, groups, nt),
        in_specs=[col(1), col(2), col(3), col(4),
                  pl.BlockSpec((1, w), lambda b, h, s: (0, h)),
                  pl.BlockSpec((1, hd), lambda b, h, s: (0, 0))],
        out_specs=[pl.BlockSpec((tb, w), lambda b, h, s: (b * nt + s, h)),
                   pl.BlockSpec((1, n_h, hd, hd), lambda b, h, s: (b, h, 0, 0))],
        out_shape=[jax.ShapeDtypeStruct((m, n_heads * hd), F32),
                   jax.ShapeDtypeStruct((bt, n_heads, hd, hd), F32)],
        scratch_shapes=[pltpu.VMEM((n_h, hd, hd), F32), pltpu.VMEM((n_h, HG_CHUNK, hd), F32)],
        compiler_params=_params(("parallel", "parallel", "arbitrary")),
        name="hgrn_prompt",
    )(z, z, z, z, lb.reshape(1, -1), gain.reshape(1, hd))


HG_SAMPLE_KEYS = 32


def _hgrn_sample_kernel(q_ref, f_ref, i_ref, g_ref, lb_ref, gain_ref, s_ref, y_ref, snew_ref,
                        qt_sc, ft_sc, kt_sc, vt_sc, o_sc):
    kb = pl.program_id(1)

    @pl.when(kb == 0)
    def _():
        lb = lb_ref[...]
        f = lb + (1.0 - lb) * _sigmoid(f_ref[...])
        qt_sc[...] = _silu(q_ref[...]).T
        ft_sc[...] = f.T
        kt_sc[...] = (1.0 - f).T
        vt_sc[...] = i_ref[...].T
        o_sc[...] = jnp.zeros_like(o_sc)

    vt = vt_sc[...]
    acc = o_sc[...]
    for kl in range(HG_SAMPLE_KEYS):
        k = kb * HG_SAMPLE_KEYS + kl
        st = s_ref[:, 0, kl, :].T
        s_new = st * ft_sc[pl.ds(k, 1), :] + vt * kt_sc[pl.ds(k, 1), :]
        acc = acc + s_new * qt_sc[pl.ds(k, 1), :]
        snew_ref[:, 0, kl, :] = s_new.T
    o_sc[...] = acc

    @pl.when(kb == pl.num_programs(1) - 1)
    def _():
        o = acc.T
        o = o * lax.rsqrt(jnp.mean(o * o, axis=-1, keepdims=True) + RMS_EPS) * gain_ref[...]
        y_ref[...] = o * _silu(g_ref[...])


def _hgrn_sample(z, states, layer, lb, gain):
    _, b, n_heads, hd, _ = states.shape
    kbs = HG_SAMPLE_KEYS

    def col(block):
        return pl.BlockSpec((b, hd), lambda h, k: (0, block * n_heads + h))

    sblk = pl.BlockSpec((b, 1, kbs, hd), lambda h, k: (0, h, k, 0))
    return pl.pallas_call(
        _hgrn_sample_kernel,
        grid=(n_heads, hd // kbs),
        in_specs=[col(1), col(2), col(3), col(4),
                  pl.BlockSpec((1, hd), lambda h, k: (0, h)),
                  pl.BlockSpec((1, hd), lambda h, k: (0, 0)),
                  pl.BlockSpec((None, b, 1, kbs, hd), lambda h, k: (layer, 0, h, k, 0))],
        out_specs=[pl.BlockSpec((b, hd), lambda h, k: (0, h)), sblk],
        out_shape=[jax.ShapeDtypeStruct((b, n_heads * hd), F32),
                   jax.ShapeDtypeStruct(states.shape[1:], F32)],
        scratch_shapes=[pltpu.VMEM((hd, b), F32)] * 5,
        compiler_params=_params(("parallel", "arbitrary")),
        name="hgrn_sample",
    )(z, z, z, z, lb.reshape(1, -1), gain.reshape(1, hd), states)


def _mix_to_hidden(x, z, y_a_pre, y_b, mw, dense, layer, alpha, tm):
    y_a = _glu(y_a_pre, dense["w_glu"], layer, tm)
    merged = _merge(y_a, y_b, z, dense["w_ba"], dense["w_bb"], layer, tm)
    return _ln1(merged, x, dense["w_o"], mw["ln1_g"], mw["ln1_b"], layer, alpha, min(tm, 256))


def _layer_prompt(x, p, mw, lw, layer, alpha, bt, t):
    m = bt * t
    tm = min(512, m)
    dense = mw["dense_bf16"]
    z = _inproj(x, dense["w_in"], layer, min(INPROJ_ROWS, m))
    y_a_pre, f_re, f_im = _s5_prompt(z, lw["s5_prompt_ops"], bt, t)
    y_b, hg_new = _hgrn_prompt(z, lw["lb"], lw["gn_gain"], bt, t)
    h = _mix_to_hidden(x, z, y_a_pre, y_b, mw, dense, layer, alpha, tm)
    s1, s2, cw, counts = _router_sorted(h, lw["w_router"], lw["b_router"], tm)
    pos8, tab = _plan(s1, s2, counts, tm)
    pos = pos8[:2].reshape(2 * m)
    n_tiles = _moe_num_tiles(m)
    tile_expert, tile_rows, n_used = tab[0, :n_tiles], tab[1, :n_tiles], tab[2, :1]
    xs = _dispatch(h, pos, tile_rows, tm)
    ys = _experts_sorted(xs, tile_expert, tile_rows, n_used, mw["w_g"], mw["w_u"], mw["w_d"], layer)
    x_new = _final_gather(h, cw, p, pos, ys, dense["w_pg"], dense["w_pp"], mw["ln2_g"], mw["ln2_b"], layer,
                          alpha, min(tm, 256))
    return x_new, f_re, f_im, hg_new


def _layer_sample(x, p, s_re, s_im, hg_states, mw, lw, layer, alpha):
    m = x.shape[0]
    dense = mw["dense_f32"]
    z = _inproj(x, dense["w_in"], layer, m)
    y_a_pre, n_re, n_im = _s5_sample(z, s_re, s_im, lw["s5_sample_ops"])
    y_b, hg_new = _hgrn_sample(z, hg_states, layer, lw["lb"], lw["gn_gain"])
    h = _mix_to_hidden(x, z, y_a_pre, y_b, mw, dense, layer, alpha, m)
    comb = _router_dense(h, lw["w_router"], lw["b_router"], m)
    moe = _moe_dense(h, comb, mw["w_g"], mw["w_u"], mw["w_d"], layer, m)
    x_new = _final(h, moe, p, dense["w_pg"], dense["w_pp"], mw["ln2_g"], mw["ln2_b"], layer, alpha, m)
    return x_new, n_re, n_im, hg_new


def _layer_operands(i, lb_all, s5_lambda_re, s5_lambda_im, s5_log_dt, s5_b_re, s5_b_im, s5_c_re, s5_c_im, s5_d,
                    hg_norm_gain, w_group_router, b_group_router, w_expert_router, b_expert_router):
    disc = _s5_discretise(s5_lambda_re[i], s5_lambda_im[i], s5_log_dt[i], s5_b_re[i], s5_b_im[i])
    d_model = w_group_router.shape[1]
    pad = LANES - N_EXPERTS - N_GROUPS
    w_router = jnp.concatenate([w_expert_router[i], w_group_router[i], jnp.zeros((d_model, pad), F32)], axis=1)
    b_router = jnp.concatenate([b_expert_router[i], b_group_router[i], jnp.zeros((pad,), F32)]).reshape(1, LANES)
    return {
        "s5_width": s5_d.shape[1],
        "s5_prompt_ops": _s5_prompt_operators(*disc, s5_c_re[i], s5_c_im[i], s5_d[i]),
        "s5_sample_ops": _s5_sample_operators(*disc, s5_c_re[i], s5_c_im[i], s5_d[i]),
        "lb": lb_all[i],
        "gn_gain": hg_norm_gain[i],
        "w_router": w_router, "b_router": b_router,
    }


def kernel(x_prompt, x_sample, p_prompt, p_sample, state_s5_re, state_s5_im, state_hgrn, w_in, s5_lambda_re, s5_lambda_im, s5_log_dt, s5_b_re, s5_b_im, s5_c_re, s5_c_im, s5_d, s5_w_glu, hg_lower_bounds, hg_norm_gain, w_branch_a, w_branch_b, w_out, ln1_gain, ln1_bias, w_group_router, b_group_router, w_expert_router, b_expert_router, w_exp_gate, w_exp_up, w_exp_down, w_ple_proj, w_ple_gate, ln2_gain, ln2_bias):
    depth = w_in.shape[0]
    bt, t, d_model = x_prompt.shape
    bs = x_sample.shape[0]
    alpha = (2 * depth) ** 0.25
    n_groups, n_state = s5_lambda_re.shape[1:]

    lb_soft = jax.nn.softmax(hg_lower_bounds.astype(F32), axis=0)
    lb_all = jnp.cumsum(lb_soft, axis=0) - lb_soft[0]

    dense_f32 = {"w_in": w_in, "w_glu": s5_w_glu, "w_ba": w_branch_a, "w_bb": w_branch_b,
                 "w_o": w_out, "w_pp": w_ple_proj, "w_pg": w_ple_gate}
    mw = {"dense_f32": dense_f32,
          "dense_bf16": {k: v.astype(BF16) for k, v in dense_f32.items()},
          "w_g": w_exp_gate, "w_u": w_exp_up, "w_d": w_exp_down,
          "ln1_g": ln1_gain, "ln1_b": ln1_bias, "ln2_g": ln2_gain, "ln2_b": ln2_bias}
    pp = p_prompt.reshape(depth, bt * t, -1)
    ps = p_sample.reshape(depth, bs, -1)

    xp = x_prompt.reshape(bt * t, d_model)
    xs = x_sample.reshape(bs, d_model)
    outs = [[] for _ in range(6)]
    for i in range(depth):
        lw = _layer_operands(i, lb_all, s5_lambda_re, s5_lambda_im, s5_log_dt, s5_b_re, s5_b_im, s5_c_re,
                             s5_c_im, s5_d, hg_norm_gain, w_group_router, b_group_router, w_expert_router,
                             b_expert_router)
        xp, a_re, a_im, a_hg = _layer_prompt(xp, pp, mw, lw, i, alpha, bt, t)
        xs, b_re, b_im, b_hg = _layer_sample(
            xs, ps, state_s5_re[i].reshape(bs, n_groups * n_state), state_s5_im[i].reshape(bs, n_groups * n_state),
            state_hgrn, mw, lw, i, alpha)
        for lst, val in zip(outs, (a_re.reshape(bt, n_groups, n_state), a_im.reshape(bt, n_groups, n_state), a_hg,
                                   b_re.reshape(bs, n_groups, n_state), b_im.reshape(bs, n_groups, n_state), b_hg)):
            lst.append(val)

    return (xp.reshape(bt, t, d_model), xs.reshape(bs, 1, d_model), *[jnp.stack(o) for o in outs])
```

```python
import functools
import math

import jax
import jax.numpy as jnp
from jax import lax
from jax.experimental import pallas as pl
from jax.experimental.pallas import tpu as pltpu

F32 = jnp.float32
BF16 = jnp.bfloat16
HIGHEST = lax.Precision.HIGHEST

LANES = 128
SUBLANES = 8
VMEM_LIMIT_BYTES = 56 * 1024 * 1024

S5_GROUP = 16
S5_STATE = 64
S5_CHUNK = 16
HG_HEAD_DIM = 128
HG_CHUNK = 64
HG_SUB = 16
N_GROUPS = 4
EXPERTS_PER_GROUP = 8
N_EXPERTS = N_GROUPS * EXPERTS_PER_GROUP
LN_EPS = 1e-5
RMS_EPS = 1e-6


def _params(semantics):
    return pltpu.CompilerParams(dimension_semantics=semantics, vmem_limit_bytes=VMEM_LIMIT_BYTES)


def _dot(a, b):
    return jnp.dot(a, b, preferred_element_type=F32)


def _mm(a, w):
    if w.dtype == F32:
        return jnp.dot(a, w, precision=HIGHEST, preferred_element_type=F32)
    return jnp.dot(a.astype(BF16), w, preferred_element_type=F32)


def _dot_nt(a, b):
    return lax.dot_general(a, b, (((1,), (1,)), ((), ())), preferred_element_type=F32)


def _dot_tn(a, b):
    return lax.dot_general(a, b, (((0,), (0,)), ((), ())), preferred_element_type=F32)


def _sigmoid(x):
    return jax.nn.sigmoid(x)


def _silu(x):
    return x * jax.nn.sigmoid(x)


def _layer_norm_rows(x, gain, bias):
    mu = jnp.mean(x, axis=-1, keepdims=True)
    xc = x - mu
    var = jnp.mean(xc * xc, axis=-1, keepdims=True)
    return xc * lax.rsqrt(var + LN_EPS) * gain + bias


INPROJ_ROWS = 1024


def _inproj_kernel(x_ref, w_ref, o_ref, xb_ref):
    if w_ref.dtype == F32:
        o_ref[...] = _mm(x_ref[...], w_ref[...])
        return

    @pl.when(pl.program_id(1) == 0)
    def _():
        xb_ref[...] = x_ref[...].astype(BF16)

    o_ref[...] = _dot(xb_ref[...], w_ref[...])


def _layer_spec(block, layer, index_map):
    return pl.BlockSpec((None,) + block, lambda *idx: (layer,) + index_map(*idx))


def _inproj(x, w, layer, tm, tn=1024):
    m, k = x.shape
    n = w.shape[-1]
    return pl.pallas_call(
        _inproj_kernel,
        grid=(m // tm, n // tn),
        in_specs=[pl.BlockSpec((tm, k), lambda i, j: (i, 0)),
                  _layer_spec((k, tn), layer, lambda i, j: (0, j))],
        out_specs=pl.BlockSpec((tm, tn), lambda i, j: (i, j)),
        out_shape=jax.ShapeDtypeStruct((m, n), F32),
        scratch_shapes=[pltpu.VMEM((tm, k), BF16)],
        compiler_params=_params(("parallel", "arbitrary")),
        name="inproj",
    )(x, w)


def _glu_kernel(a_ref, w_ref, o_ref):
    a = a_ref[...]
    o_ref[...] = a * _sigmoid(_mm(a, w_ref[...]))


def _glu(a, w, layer, tm):
    m, k = a.shape
    return pl.pallas_call(
        _glu_kernel,
        grid=(m // tm,),
        in_specs=[pl.BlockSpec((tm, k), lambda i: (i, 0)),
                  _layer_spec((k, k), layer, lambda i: (0, 0))],
        out_specs=pl.BlockSpec((tm, k), lambda i: (i, 0)),
        out_shape=jax.ShapeDtypeStruct((m, k), F32),
        compiler_params=_params(("parallel",)),
        name="s5_glu",
    )(a, w)


def _merge_kernel(ya_ref, yb_ref, ga_ref, gb_ref, wa_ref, wb_ref, o_ref):
    pa = _mm(ya_ref[...], wa_ref[...])
    pb = _mm(yb_ref[...], wb_ref[...])
    o_ref[...] = _sigmoid(ga_ref[...]) * pa + _sigmoid(gb_ref[...]) * pb


def _merge(ya, yb, z, wa, wb, layer, tm, tn=1024):
    m, k = ya.shape
    n = wa.shape[-1]
    nj = n // tn
    return pl.pallas_call(
        _merge_kernel,
        grid=(m // tm, nj),
        in_specs=[pl.BlockSpec((tm, k), lambda i, j: (i, 0)),
                  pl.BlockSpec((tm, k), lambda i, j: (i, 0)),
                  pl.BlockSpec((tm, tn), lambda i, j: (i, 5 + j)),
                  pl.BlockSpec((tm, tn), lambda i, j: (i, 5 + nj + j)),
                  _layer_spec((k, tn), layer, lambda i, j: (0, j)),
                  _layer_spec((k, tn), layer, lambda i, j: (0, j))],
        out_specs=pl.BlockSpec((tm, tn), lambda i, j: (i, j)),
        out_shape=jax.ShapeDtypeStruct((m, n), F32),
        compiler_params=_params(("parallel", "arbitrary")),
        name="branch_merge",
    )(ya, yb, z, z, wa, wb)


def _ln1_kernel(mg_ref, x_ref, w_ref, g_ref, b_ref, o_ref, *, alpha):
    r = alpha * x_ref[...] + _mm(mg_ref[...], w_ref[...])
    o_ref[...] = _layer_norm_rows(r, g_ref[...], b_ref[...])


def _ln1(merged, x, w, gain, bias, layer, alpha, tm):
    m, d = x.shape
    row = pl.BlockSpec((tm, d), lambda i: (i, 0))
    vec = _layer_spec((1, d), layer, lambda i: (0, 0))
    return pl.pallas_call(
        functools.partial(_ln1_kernel, alpha=alpha),
        grid=(m // tm,),
        in_specs=[row, row, _layer_spec((d, d), layer, lambda i: (0, 0)), vec, vec],
        out_specs=row,
        out_shape=jax.ShapeDtypeStruct((m, d), F32),
        compiler_params=_params(("parallel",)),
        name="outproj_ln1",
    )(merged, x, w, gain[:, None, :], bias[:, None, :])


def _final_kernel(h_ref, moe_ref, p_ref, wg_ref, wp_ref, g_ref, b_ref, o_ref, *, alpha):
    h = h_ref[...]
    gate = _sigmoid(_mm(h, wg_ref[...]))
    proj = _mm(p_ref[...], wp_ref[...])
    r = alpha * h + moe_ref[...] + gate * proj
    o_ref[...] = _layer_norm_rows(r, g_ref[...], b_ref[...])


def _final(h, moe, p, wg, wp, gain, bias, layer, alpha, tm):
    m, d = h.shape
    dp = p.shape[-1]
    row = pl.BlockSpec((tm, d), lambda i: (i, 0))
    vec = _layer_spec((1, d), layer, lambda i: (0, 0))
    return pl.pallas_call(
        functools.partial(_final_kernel, alpha=alpha),
        grid=(m // tm,),
        in_specs=[row, row, _layer_spec((tm, dp), layer, lambda i: (i, 0)),
                  _layer_spec((d, d), layer, lambda i: (0, 0)),
                  _layer_spec((dp, d), layer, lambda i: (0, 0)), vec, vec],
        out_specs=row,
        out_shape=jax.ShapeDtypeStruct((m, d), F32),
        compiler_params=_params(("parallel",)),
        name="ple_ln2",
    )(h, moe, p, wg, wp, gain[:, None, :], bias[:, None, :])


def _route(h, w, b):
    logits = jnp.dot(h, w, precision=HIGHEST, preferred_element_type=F32) + b
    lane = lax.broadcasted_iota(jnp.int32, logits.shape, 1)
    lane_f = lane.astype(F32)
    neg = -jnp.inf
    gl = jnp.where(lane >= N_EXPERTS, jnp.where(lane < N_EXPERTS + N_GROUPS, logits, neg), neg)
    gmax = jnp.max(gl, axis=-1, keepdims=True)
    denom = jnp.sum(jnp.exp(gl - gmax), axis=-1, keepdims=True)
    grp_p = 1.0 / denom
    gidx = jnp.min(jnp.where(gl == gmax, lane_f, 1e9), axis=-1, keepdims=True) - N_EXPERTS
    lane_grp = (lane // EXPERTS_PER_GROUP).astype(F32)
    el = jnp.where(lane < N_EXPERTS, jnp.where(lane_grp == gidx, logits, neg), neg)
    t1 = jnp.max(el, axis=-1, keepdims=True)
    i1 = jnp.min(jnp.where(el == t1, lane_f, 1e9), axis=-1, keepdims=True)
    el2 = jnp.where(lane_f == i1, neg, el)
    t2 = jnp.max(el2, axis=-1, keepdims=True)
    i2 = jnp.min(jnp.where(el2 == t2, lane_f, 1e9), axis=-1, keepdims=True)
    e2 = jnp.exp(t2 - t1)
    w1 = 1.0 / (1.0 + e2)
    w2 = e2 * w1
    return lane, lane_f, i1, i2, grp_p * w1, grp_p * w2


def _router_dense_kernel(h_ref, w_ref, b_ref, o_ref):
    _, lane_f, i1, i2, c1, c2 = _route(h_ref[...], w_ref[...], b_ref[...])
    o_ref[...] = jnp.where(lane_f == i1, c1, jnp.where(lane_f == i2, c2, 0.0))


def _router_dense(h, w_r, b_r, tm):
    m, d = h.shape
    return pl.pallas_call(
        _router_dense_kernel,
        grid=(m // tm,),
        in_specs=[pl.BlockSpec((tm, d), lambda i: (i, 0)),
                  pl.BlockSpec((d, LANES), lambda i: (0, 0)),
                  pl.BlockSpec((1, LANES), lambda i: (0, 0))],
        out_specs=pl.BlockSpec((tm, LANES), lambda i: (i, 0)),
        out_shape=jax.ShapeDtypeStruct((m, LANES), F32),
        compiler_params=_params(("parallel",)),
        name="router_dense",
    )(h, w_r, b_r)


def _moe_dense_kernel(h_ref, c_ref, wg_ref, wu_ref, wd_ref, o_ref):
    e = pl.program_id(1)

    @pl.when(e == 0)
    def _():
        o_ref[...] = jnp.zeros_like(o_ref)

    comb = c_ref[...]
    lane = lax.broadcasted_iota(jnp.int32, comb.shape, 1)
    col = jnp.sum(jnp.where(lane == e, comb, 0.0), axis=-1, keepdims=True)
    h = h_ref[...]
    hid = _silu(_mm(h, wg_ref[0])) * _mm(h, wu_ref[0]) * col
    o_ref[...] += _mm(hid, wd_ref[0])


def _moe_dense(h, comb, wg, wu, wd, layer, tm):
    m, d = h.shape
    _, n_e, _, f = wg.shape
    return pl.pallas_call(
        _moe_dense_kernel,
        grid=(m // tm, n_e),
        in_specs=[pl.BlockSpec((tm, d), lambda i, e: (i, 0)),
                  pl.BlockSpec((tm, LANES), lambda i, e: (i, 0)),
                  _layer_spec((1, d, f), layer, lambda i, e: (e, 0, 0)),
                  _layer_spec((1, d, f), layer, lambda i, e: (e, 0, 0)),
                  _layer_spec((1, f, d), layer, lambda i, e: (e, 0, 0))],
        out_specs=pl.BlockSpec((tm, d), lambda i, e: (i, 0)),
        out_shape=jax.ShapeDtypeStruct((m, d), F32),
        compiler_params=_params(("parallel", "arbitrary")),
        name="moe_dense",
    )(h, comb, wg, wu, wd)


MOE_TILE = 256


def _moe_num_tiles(m):
    return 2 * m // MOE_TILE + N_EXPERTS


def _router_sorted_kernel(h_ref, w_ref, b_ref, s1_ref, s2_ref, cw_ref, cnt_ref, carry_sc):
    @pl.when(pl.program_id(0) == 0)
    def _():
        carry_sc[...] = jnp.zeros_like(carry_sc)

    lane, lane_f, i1, i2, c1, c2 = _route(h_ref[...], w_ref[...], b_ref[...])
    sel1 = jnp.where(lane_f == i1, 1.0, 0.0)
    sel2 = jnp.where(lane_f == i2, 1.0, 0.0)
    cnt = sel1 + sel2
    tm = cnt.shape[0]
    strict_lower = (lax.broadcasted_iota(jnp.int32, (tm, tm), 0) > lax.broadcasted_iota(jnp.int32, (tm, tm), 1))
    prefix = _dot(strict_lower.astype(BF16), cnt.astype(BF16)) + carry_sc[0:1, :]
    s1_ref[...] = sel1 * (prefix + 1.0)
    s2_ref[...] = sel2 * (prefix + 1.0)
    cw_ref[...] = jnp.where(lane == 0, c1, jnp.where(lane == 1, c2, 0.0))
    carry_sc[...] = carry_sc[...] + jnp.sum(cnt, axis=0, keepdims=True)
    cnt_ref[...] = carry_sc[...]


def _router_sorted(h, w_r, b_r, tm):
    m, d = h.shape
    tok = pl.BlockSpec((tm, LANES), lambda i: (i, 0))
    return pl.pallas_call(
        _router_sorted_kernel,
        grid=(m // tm,),
        in_specs=[pl.BlockSpec((tm, d), lambda i: (i, 0)),
                  pl.BlockSpec((d, LANES), lambda i: (0, 0)),
                  pl.BlockSpec((1, LANES), lambda i: (0, 0))],
        out_specs=[tok, tok, tok, pl.BlockSpec((SUBLANES, LANES), lambda i: (0, 0))],
        out_shape=[jax.ShapeDtypeStruct((m, LANES), F32)] * 3 + [jax.ShapeDtypeStruct((SUBLANES, LANES), F32)],
        scratch_shapes=[pltpu.VMEM((SUBLANES, LANES), F32)],
        compiler_params=_params(("arbitrary",)),
        name="router_sorted",
    )(h, w_r, b_r)


def _plan_kernel(s1_ref, s2_ref, cnt_ref, pos_ref, tab_ref):
    te = float(MOE_TILE)
    lane = lax.broadcasted_iota(jnp.int32, (1, LANES), 1)
    cnt = jnp.where(lane < N_EXPERTS, cnt_ref[0:1, :], 0.0)
    padded = jnp.floor((cnt + (te - 1.0)) * (1.0 / te)) * te
    r128 = lax.broadcasted_iota(jnp.int32, (LANES, LANES), 0)
    c128 = lax.broadcasted_iota(jnp.int32, (LANES, LANES), 1)
    before = jnp.where(r128 < c128, 1.0, 0.0)
    off = jnp.dot(jnp.broadcast_to(padded, (SUBLANES, LANES)), before, precision=HIGHEST,
                  preferred_element_type=F32)[0:1, :]
    s1 = s1_ref[...]
    s2 = s2_ref[...]
    v1 = jnp.where(s1 > 0.0, s1 - 1.0 + off, 0.0)
    v2 = jnp.where(s2 > 0.0, s2 - 1.0 + off, 0.0)
    ones8 = jnp.ones((SUBLANES, LANES), F32)
    p1 = lax.dot_general(ones8, v1, (((1,), (1,)), ((), ())), precision=HIGHEST, preferred_element_type=F32)
    p2 = lax.dot_general(ones8, v2, (((1,), (1,)), ((), ())), precision=HIGHEST, preferred_element_type=F32)
    row8 = lax.broadcasted_iota(jnp.int32, p1.shape, 0)
    pos_ref[...] = jnp.where(row8 == 0, p1, p2).astype(jnp.int32)

    def per_expert(row):
        return jnp.broadcast_to(row, (LANES, LANES)).T
    off_e = per_expert(off)
    end_e = per_expert(off + padded)
    cnt_e = per_expert(cnt)
    start = c128.astype(F32) * te
    is_e = r128 < N_EXPERTS
    tile_e = jnp.sum(jnp.where(is_e, jnp.where(end_e <= start, 1.0, 0.0), 0.0), axis=0, keepdims=True)
    rows = jnp.clip(cnt_e - (start - off_e), 0.0, te)
    owns = jnp.where(is_e, jnp.where(off_e <= start, jnp.where(start < end_e, rows, 0.0), 0.0), 0.0)
    n_rows = jnp.sum(owns, axis=0, keepdims=True)
    tile_e = jnp.minimum(tile_e, float(N_EXPERTS - 1))
    n_used = jnp.sum(jnp.where(n_rows > 0.0, 1.0, 0.0), axis=-1, keepdims=True)
    row_t = lax.broadcasted_iota(jnp.int32, (SUBLANES, LANES), 0)
    tab_ref[...] = jnp.where(row_t == 0, tile_e, jnp.where(row_t == 1, n_rows, n_used)).astype(jnp.int32)


def _plan(s1, s2, counts, tm):
    m = s1.shape[0]
    tok = pl.BlockSpec((tm, LANES), lambda i: (i, 0))
    return pl.pallas_call(
        _plan_kernel,
        grid=(m // tm,),
        in_specs=[tok, tok, pl.BlockSpec((SUBLANES, LANES), lambda i: (0, 0))],
        out_specs=[pl.BlockSpec((SUBLANES, tm), lambda i: (0, i)),
                   pl.BlockSpec((SUBLANES, LANES), lambda i: (0, 0))],
        out_shape=[jax.ShapeDtypeStruct((SUBLANES, m), jnp.int32),
                   jax.ShapeDtypeStruct((SUBLANES, LANES), jnp.int32)],
        compiler_params=_params(("arbitrary",)),
        name="moe_plan",
    )(s1, s2, counts)


DMA_UNROLL = 8


def _dispatch_kernel(pos_ref, rows_ref, h_ref, xs_hbm, zero_buf, zero_sem, sem, *, m, n_tiles):
    i = pl.program_id(0)
    tm = h_ref.shape[0]

    @pl.when(i == 0)
    def _():
        zero_buf[...] = jnp.zeros_like(zero_buf)

        def zero_copy(j):
            return pltpu.make_async_copy(zero_buf, xs_hbm.at[pl.ds(j * MOE_TILE, MOE_TILE)], zero_sem)

        def zero_start(j, c):
            @pl.when(rows_ref[j] < MOE_TILE)
            def _():
                zero_copy(j).start()
            return c

        def zero_wait(j, c):
            @pl.when(rows_ref[j] < MOE_TILE)
            def _():
                zero_copy(j).wait()
            return c

        lax.fori_loop(0, n_tiles, zero_start, 0)
        lax.fori_loop(0, n_tiles, zero_wait, 0)

    def body(r, c):
        t = i * tm + r
        src = h_ref.at[pl.ds(r, 1)]
        pltpu.make_async_copy(src, xs_hbm.at[pl.ds(pos_ref[t], 1)], sem).start()
        pltpu.make_async_copy(src, xs_hbm.at[pl.ds(pos_ref[m + t], 1)], sem).start()
        return c
    lax.fori_loop(0, tm, body, 0, unroll=DMA_UNROLL)

    pltpu.make_async_copy(xs_hbm.at[pl.ds(0, 2 * tm)], xs_hbm.at[pl.ds(0, 2 * tm)], sem).wait()


def _dispatch(h, pos, tile_rows, tm):
    m, d = h.shape
    n_tiles = _moe_num_tiles(m)
    grid_spec = pltpu.PrefetchScalarGridSpec(
        num_scalar_prefetch=2,
        grid=(m // tm,),
        in_specs=[pl.BlockSpec((tm, d), lambda i, pos, rows: (i, 0))],
        out_specs=pl.BlockSpec(memory_space=pl.ANY),
        scratch_shapes=[pltpu.VMEM((MOE_TILE, d), F32), pltpu.SemaphoreType.DMA(()),
                        pltpu.SemaphoreType.DMA(())],
    )
    return pl.pallas_call(
        functools.partial(_dispatch_kernel, m=m, n_tiles=n_tiles),
        grid_spec=grid_spec,
        out_shape=jax.ShapeDtypeStruct((n_tiles * MOE_TILE, d), F32),
        compiler_params=_params(("arbitrary",)),
        name="moe_dispatch",
    )(pos, tile_rows, h)


def _experts_kernel(te_ref, nr_ref, nu_ref, x_ref, wg_ref, wu_ref, wd_ref, y_ref, wgb, wub, wdb):
    j = pl.program_id(0)
    new_expert = jnp.logical_or(j == 0, te_ref[j] != te_ref[jnp.maximum(j - 1, 0)])

    @pl.when(jnp.logical_and(new_expert, nr_ref[j] > 0))
    def _():
        wgb[...] = wg_ref[0].astype(BF16)
        wub[...] = wu_ref[0].astype(BF16)
        wdb[...] = wd_ref[0].astype(BF16)

    @pl.when(nr_ref[j] > 0)
    def _():
        x = x_ref[...].astype(BF16)
        hid = _silu(_dot(x, wgb[...])) * _dot(x, wub[...])
        y_ref[...] = _dot(hid.astype(BF16), wdb[...])

    @pl.when(nr_ref[j] == 0)
    def _():
        y_ref[...] = jnp.zeros_like(y_ref)


def _experts_sorted(xs, tile_expert, tile_rows, n_used, wg, wu, wd, layer):
    n_rows, d = xs.shape
    _, n_e, _, f = wg.shape
    n_tiles = n_rows // MOE_TILE

    def used_tile(j, te, nr, nu):
        return (jnp.minimum(j, nu[0] - 1), 0)

    def expert_block(j, te, nr, nu):
        return (layer, te[j], 0, 0)

    grid_spec = pltpu.PrefetchScalarGridSpec(
        num_scalar_prefetch=3,
        grid=(n_tiles,),
        in_specs=[pl.BlockSpec((MOE_TILE, d), used_tile),
                  pl.BlockSpec((None, 1, d, f), expert_block),
                  pl.BlockSpec((None, 1, d, f), expert_block),
                  pl.BlockSpec((None, 1, f, d), expert_block)],
        out_specs=pl.BlockSpec((MOE_TILE, d), lambda j, te, nr, nu: (j, 0)),
        scratch_shapes=[pltpu.VMEM((d, f), BF16), pltpu.VMEM((d, f), BF16), pltpu.VMEM((f, d), BF16)],
    )
    return pl.pallas_call(
        _experts_kernel,
        grid_spec=grid_spec,
        out_shape=jax.ShapeDtypeStruct((n_rows, d), F32),
        compiler_params=_params(("arbitrary",)),
        name="moe_experts_sorted",
    )(tile_expert, tile_rows, n_used, xs, wg, wu, wd)


def _final_gather_kernel(pos_ref, h_ref, cw_ref, p_ref, wg_ref, wp_ref, g_ref, b_ref, ys_hbm, o_ref,
                         ybuf, sem, *, alpha, m):
    i = pl.program_id(0)
    slot = i % 2
    tm = h_ref.shape[0]

    last = pl.num_programs(0) - 1

    def start_rows(tile, dst_slot, r):
        t = tile * tm + r
        for pick in range(2):
            pltpu.make_async_copy(ys_hbm.at[pl.ds(pos_ref[pick * m + t], 1)],
                                  ybuf.at[dst_slot, pl.ds(pick * tm + r, 1)], sem.at[dst_slot]).start()

    def wait_tile(s):
        pltpu.make_async_copy(ys_hbm.at[pl.ds(0, 2 * tm)], ybuf.at[s], sem.at[s]).wait()

    @pl.when(i == 0)
    def _():
        def body(r, c):
            start_rows(0, 0, r)
            return c
        lax.fori_loop(0, tm, body, 0, unroll=DMA_UNROLL)

    nxt = jnp.minimum(i + 1, last)
    for r in range(tm):
        start_rows(nxt, 1 - slot, r)

    h = h_ref[...]
    gate = _sigmoid(_mm(h, wg_ref[...]))
    proj = _mm(p_ref[...], wp_ref[...])

    wait_tile(slot)
    cw = cw_ref[...]
    moe = cw[:, 0:1] * ybuf[slot, 0:tm] + cw[:, 1:2] * ybuf[slot, tm:2 * tm]
    r = alpha * h + moe + gate * proj
    o_ref[...] = _layer_norm_rows(r, g_ref[...], b_ref[...])

    @pl.when(i == last)
    def _():
        wait_tile(1 - slot)


def _final_gather(h, cw, p, pos, ys, wg, wp, gain, bias, layer, alpha, tm):
    m, d = h.shape
    dp = p.shape[-1]
    row = pl.BlockSpec((tm, d), lambda i, pos: (i, 0))
    vec = _layer_spec((1, d), layer, lambda i, pos: (0, 0))
    grid_spec = pltpu.PrefetchScalarGridSpec(
        num_scalar_prefetch=1,
        grid=(m // tm,),
        in_specs=[row, pl.BlockSpec((tm, LANES), lambda i, pos: (i, 0)),
                  _layer_spec((tm, dp), layer, lambda i, pos: (i, 0)),
                  _layer_spec((d, d), layer, lambda i, pos: (0, 0)),
                  _layer_spec((dp, d), layer, lambda i, pos: (0, 0)), vec, vec,
                  pl.BlockSpec(memory_space=pl.ANY)],
        out_specs=row,
        scratch_shapes=[pltpu.VMEM((2, 2 * tm, d), F32), pltpu.SemaphoreType.DMA((2,))],
    )
    return pl.pallas_call(
        functools.partial(_final_gather_kernel, alpha=alpha, m=m),
        grid_spec=grid_spec,
        out_shape=jax.ShapeDtypeStruct((m, d), F32),
        compiler_params=_params(("arbitrary",)),
        name="ple_ln2_gather",
    )(pos, h, cw, p, wg, wp, gain[:, None, :], bias[:, None, :], ys)


def _s5_discretise(lam_re, lam_im, log_dt, b_re, b_im):
    dt = jnp.exp(log_dt)[:, None]
    mag = jnp.exp(lam_re * dt)
    ab_re = mag * jnp.cos(lam_im * dt)
    ab_im = mag * jnp.sin(lam_im * dt)
    den = lam_re * lam_re + lam_im * lam_im
    nr = ab_re - 1.0
    zr = (nr * lam_re + ab_im * lam_im) / den
    zi = (ab_im * lam_re - nr * lam_im) / den
    bb_re = zr[..., None] * b_re - zi[..., None] * b_im
    bb_im = zr[..., None] * b_im + zi[..., None] * b_re
    return ab_re, ab_im, bb_re, bb_im


def _s5_prompt_operators(ab_re, ab_im, bb_re, bb_im, c_re, c_im, d_skip):
    g, n = ab_re.shape
    cch = bb_re.shape[-1]
    L = S5_CHUNK
    pw_re = [jnp.ones_like(ab_re)]
    pw_im = [jnp.zeros_like(ab_im)]
    for _ in range(L):
        pr, pi = pw_re[-1], pw_im[-1]
        pw_re.append(pr * ab_re - pi * ab_im)
        pw_im.append(pr * ab_im + pi * ab_re)
    a_re = jnp.stack(pw_re)
    a_im = jnp.stack(pw_im)
    w_re = a_re[:L, :, :, None] * bb_re - a_im[:L, :, :, None] * bb_im
    w_im = a_re[:L, :, :, None] * bb_im + a_im[:L, :, :, None] * bb_re
    kern = (jnp.einsum('gcn,kgnd->kgcd', c_re, w_re, precision=HIGHEST)
            - jnp.einsum('gcn,kgnd->kgcd', c_im, w_im, precision=HIGHEST))
    steps = jnp.arange(L)
    lag_is = (steps[None, None, :] - steps[None, :, None] == steps[:, None, None]).astype(F32)
    toe = jnp.einsum('kio,kgcd->gidoc', lag_is, kern, precision=HIGHEST)
    toe = toe.reshape(g, L * cch, L * cch)

    p_re = w_re[::-1].transpose(1, 0, 3, 2).reshape(g, L * cch, n)
    p_im = w_im[::-1].transpose(1, 0, 3, 2).reshape(g, L * cch, n)
    q_re = (jnp.einsum('gcn,tgn->gntc', c_re, a_re[1:]) - jnp.einsum('gcn,tgn->gntc', c_im, a_im[1:]))
    q_im = -(jnp.einsum('gcn,tgn->gntc', c_re, a_im[1:]) + jnp.einsum('gcn,tgn->gntc', c_im, a_re[1:]))
    q_re = q_re.reshape(g, n, L * cch)
    q_im = q_im.reshape(g, n, L * cch)

    hp = g // 2
    w = L * cch
    pe_re, po_re, pe_im, po_im = (x.astype(BF16) for x in (p_re[0::2], p_re[1::2], p_im[0::2], p_im[1::2]))
    qe_re, qo_re, qe_im, qo_im = (x.astype(BF16) for x in (q_re[0::2], q_re[1::2], q_im[0::2], q_im[1::2]))
    zp = jnp.zeros_like(pe_re)
    zq = jnp.zeros_like(qe_re)
    p2 = jnp.concatenate([jnp.concatenate([pe_re, zp, pe_im, zp], axis=2),
                          jnp.concatenate([zp, po_re, zp, po_im], axis=2)], axis=1)
    q2 = jnp.concatenate([jnp.concatenate([qe_re, zq], axis=2), jnp.concatenate([zq, qo_re], axis=2),
                          jnp.concatenate([qe_im, zq], axis=2), jnp.concatenate([zq, qo_im], axis=2)], axis=1)
    a_chunk = jnp.stack([a_re[L].reshape(hp, 2 * n), a_im[L].reshape(hp, 2 * n)], axis=1)
    d2 = jnp.broadcast_to(d_skip.reshape(hp, 2, 1, cch), (hp, 2, L, cch)).reshape(hp, 1, 2 * w)
    return toe.astype(BF16), p2, q2, a_chunk, d2


def _s5_prompt_kernel(u_ref, t_ref, p_ref, q_ref, a_ref, d_ref, y_ref, fre_ref, fim_ref,
                      s_sc, xin_sc, *, n_chunks, bt):
    half = a_ref.shape[-1]
    u0 = jnp.concatenate([u_ref[0, 0], u_ref[0, 1]], axis=1)
    u1 = jnp.concatenate([u_ref[1, 0], u_ref[1, 1]], axis=1)
    ub0 = u0.astype(BF16)
    ub1 = u1.astype(BF16)
    y_intra = jnp.concatenate([_dot(ub0, t_ref[0]), _dot(ub1, t_ref[1])], axis=1)
    s_sc[...] = _dot(jnp.concatenate([ub0, ub1], axis=1), p_ref[0])
    ar = a_ref[0, 0:1, :]
    ai = a_ref[0, 1:2, :]
    xr = jnp.zeros((bt, half), F32)
    xi = jnp.zeros((bt, half), F32)
    for j in range(n_chunks):
        rows = slice(j * bt, (j + 1) * bt)
        xin_sc[rows, 0:half] = xr
        xin_sc[rows, half:2 * half] = xi
        sr = s_sc[rows, 0:half]
        si = s_sc[rows, half:2 * half]
        xr, xi = ar * xr - ai * xi + sr, ar * xi + ai * xr + si
    fre_ref[...] = xr
    fim_ref[...] = xi
    y_carry = _dot(xin_sc[...].astype(BF16), q_ref[0])
    y = y_intra + y_carry + d_ref[0] * jnp.concatenate([u0, u1], axis=1)
    y = jax.nn.gelu(y)
    for k in range(4):
        y_ref[k // 2, k % 2] = y[:, k * LANES:(k + 1) * LANES]


S5_RELAYOUT_CHUNKS = 32
S5_RELAYOUT_GROUPS = LANES // S5_GROUP


def _to_chunks_kernel(x_ref, o_ref, *, bt):
    b = pl.program_id(2)
    L, cch = S5_CHUNK, S5_GROUP
    nj = x_ref.shape[0] // L
    per_tile = LANES // cch
    at_pos =[x_ref[pl.ds(p, nj, stride=L), :] for p in range(L)]
    for g in range(S5_RELAYOUT_GROUPS):
        for h in range(L // per_tile):
            piece = jnp.concatenate([at_pos[h * per_tile + p][:, g * cch:(g + 1) * cch] for p in range(per_tile)],
                                    axis=1)
            o_ref[g, h, pl.ds(b, nj, stride=bt), :] = piece


def _from_chunks_kernel(y_ref, o_ref, *, bt):
    b = pl.program_id(2)
    L, cch = S5_CHUNK, S5_GROUP
    nj = o_ref.shape[0] // L
    per_tile = LANES // cch
    tiles = [[y_ref[g, h, pl.ds(b, nj, stride=bt), :] for h in range(L // per_tile)]
             for g in range(S5_RELAYOUT_GROUPS)]
    for p in range(L):
        h, q = divmod(p, per_tile)
        o_ref[pl.ds(p, nj, stride=L), :] = jnp.concatenate(
            [tiles[g][h][:, q * cch:(q + 1) * cch] for g in range(S5_RELAYOUT_GROUPS)], axis=1)


def _to_chunks(z, g, bt, t):
    L, cch, gt = S5_CHUNK, S5_GROUP, S5_RELAYOUT_GROUPS
    nj = min(S5_RELAYOUT_CHUNKS, t // L)
    n_chunks = t // L
    nb = n_chunks // nj
    halves = L * cch // LANES
    return pl.pallas_call(
        functools.partial(_to_chunks_kernel, bt=bt),
        grid=(nb, g // gt, bt),
        in_specs=[pl.BlockSpec((nj * L, LANES), lambda j, lt, b: (b * nb + j, lt))],
        out_specs=pl.BlockSpec((gt, halves, nj * bt, LANES), lambda j, lt, b: (lt, 0, j, 0)),
        out_shape=jax.ShapeDtypeStruct((g, halves, n_chunks * bt, LANES), F32),
        compiler_params=_params(("parallel", "parallel", "arbitrary")),
        name="s5_to_chunks",
    )(z)


def _from_chunks(y_t, bt, t):
    L, cch, gt = S5_CHUNK, S5_GROUP, S5_RELAYOUT_GROUPS
    nj = min(S5_RELAYOUT_CHUNKS, t // L)
    g, halves = y_t.shape[:2]
    n_chunks = t // L
    nb = n_chunks // nj
    return pl.pallas_call(
        functools.partial(_from_chunks_kernel, bt=bt),
        grid=(nb, g // gt, bt),
        in_specs=[pl.BlockSpec((gt, halves, nj * bt, LANES), lambda j, lt, b: (lt, 0, j, 0))],
        out_specs=pl.BlockSpec((nj * L, LANES), lambda j, lt, b: (b * nb + j, lt)),
        out_shape=jax.ShapeDtypeStruct((bt * t, g * cch), F32),
        compiler_params=_params(("parallel", "parallel", "arbitrary")),
        name="s5_from_chunks",
    )(y_t)


def _s5_prompt(z, ops, bt, t):
    toe, p2, q2, a_chunk, d2 = ops
    g = toe.shape[0]
    hp = g // 2
    L = S5_CHUNK
    cch = S5_GROUP
    n_chunks = t // L
    r = n_chunks * bt
    w = L * cch
    half = a_chunk.shape[-1]
    u_t = _to_chunks(z, g, bt, t)
    y_t, f_re, f_im = pl.pallas_call(
        functools.partial(_s5_prompt_kernel, n_chunks=n_chunks, bt=bt),
        grid=(hp,),
        in_specs=[pl.BlockSpec((2, w // LANES, r, LANES), lambda i: (i, 0, 0, 0)),
                  pl.BlockSpec((2, w, w), lambda i: (i, 0, 0)),
                  pl.BlockSpec((1, 2 * w, 2 * half), lambda i: (i, 0, 0)),
                  pl.BlockSpec((1, 2 * half, 2 * w), lambda i: (i, 0, 0)),
                  pl.BlockSpec((1, 2, half), lambda i: (i, 0, 0)),
                  pl.BlockSpec((1, 1, 2 * w), lambda i: (i, 0, 0))],
        out_specs=[pl.BlockSpec((2, w // LANES, r, LANES), lambda i: (i, 0, 0, 0)),
                   pl.BlockSpec((bt, half), lambda i: (0, i)),
                   pl.BlockSpec((bt, half), lambda i: (0, i))],
        out_shape=[jax.ShapeDtypeStruct((g, w // LANES, r, LANES), F32),
                   jax.ShapeDtypeStruct((bt, hp * half), F32),
                   jax.ShapeDtypeStruct((bt, hp * half), F32)],
        scratch_shapes=[pltpu.VMEM((r, 2 * half), F32), pltpu.VMEM((r, 2 * half), F32)],
        compiler_params=_params(("parallel",)),
        name="s5_prompt",
    )(u_t, toe, p2, q2, a_chunk, d2)
    return _from_chunks(y_t, bt, t), f_re, f_im


S5_SAMPLE_GROUPS = 8


def _s5_sample_operators(ab_re, ab_im, bb_re, bb_im, c_re, c_im, d_skip):
    g, n = ab_re.shape
    cch = bb_re.shape[-1]
    gb = S5_SAMPLE_GROUPS
    nb = g // gb
    eye = jnp.eye(gb, dtype=F32)
    b_ri = jnp.stack([bb_re, bb_im], axis=1).reshape(nb, gb, 2, n, cch)
    c_ri = jnp.stack([c_re, -c_im], axis=1).reshape(nb, gb, 2, cch, n)
    b8 = (b_ri.transpose(0, 1, 4, 2, 3)[:, :, :, :, None, :]
          * eye[None, :, None, None, :, None]).reshape(nb, gb * cch, 2 * gb * n)
    c8 = (c_ri.transpose(0, 2, 1, 4, 3)[:, :, :, :, None, :]
          * eye[None, None, :, None, :, None]).reshape(nb, 2 * gb * n, gb * cch)
    a8 = jnp.stack([ab_re.reshape(nb, gb * n), ab_im.reshape(nb, gb * n)], axis=1)
    d8 = d_skip.reshape(nb, 1, gb * cch)
    return b8, c8, a8, d8


def _s5_sample_kernel(u_ref, sr_ref, si_ref, b_ref, c_ref, a_ref, d_ref, y_ref, nr_ref, ni_ref):
    u = u_ref[...]
    half = sr_ref.shape[-1]
    bu = jnp.dot(u, b_ref[0], precision=HIGHEST, preferred_element_type=F32)
    ar = a_ref[0, 0:1, :]
    ai = a_ref[0, 1:2, :]
    sr = sr_ref[...]
    si = si_ref[...]
    xr = ar * sr - ai * si + bu[:, :half]
    xi = ar * si + ai * sr + bu[:, half:]
    nr_ref[...] = xr
    ni_ref[...] = xi
    y = jnp.dot(jnp.concatenate([xr, xi], axis=1), c_ref[0], precision=HIGHEST,
                preferred_element_type=F32) + d_ref[0] * u
    y_ref[...] = jax.nn.gelu(y)


def _s5_sample(z, s_re, s_im, ops):
    b8, c8, a8, d8 = ops
    nb = b8.shape[0]
    b = z.shape[0]
    wu = b8.shape[1]
    ws = a8.shape[-1]
    return pl.pallas_call(
        _s5_sample_kernel,
        grid=(nb,),
        in_specs=[pl.BlockSpec((b, wu), lambda i: (0, i)),
                  pl.BlockSpec((b, ws), lambda i: (0, i)),
                  pl.BlockSpec((b, ws), lambda i: (0, i)),
                  pl.BlockSpec((1, wu, 2 * ws), lambda i: (i, 0, 0)),
                  pl.BlockSpec((1, 2 * ws, wu), lambda i: (i, 0, 0)),
                  pl.BlockSpec((1, 2, ws), lambda i: (i, 0, 0)),
                  pl.BlockSpec((1, 1, wu), lambda i: (i, 0, 0))],
        out_specs=[pl.BlockSpec((b, wu), lambda i: (0, i)),
                   pl.BlockSpec((b, ws), lambda i: (0, i)),
                   pl.BlockSpec((b, ws), lambda i: (0, i))],
        out_shape=[jax.ShapeDtypeStruct((b, nb * wu), F32),
                   jax.ShapeDtypeStruct((b, nb * ws), F32),
                   jax.ShapeDtypeStruct((b, nb * ws), F32)],
        compiler_params=_params(("parallel",)),
        name="s5_sample",
    )(z, s_re, s_im, b8, c8, a8, d8)


HG_HEADS_PER_STEP = 8
HG_SAFE_SPAN = 60.0


def _hgrn_intra_factored(q, kk, v, g):
    c, sub = HG_CHUNK, HG_SUB
    n_sub = c // sub
    row = lax.broadcasted_iota(jnp.int32, (c, 1), 0)
    starts = [g[i * sub - 1:i * sub, :] if i else jnp.zeros_like(g[0:1, :]) for i in range(n_sub)]
    g_start_rows = jnp.concatenate([jnp.broadcast_to(s, (sub, s.shape[1])) for s in starts], axis=0)
    q_fac = (q * jnp.exp(g - g_start_rows)).astype(BF16)
    pad = jnp.zeros((LANES - c, kk.shape[1]), BF16)
    k_fac = []
    for i in range(n_sub):
        ki = kk * jnp.exp(jnp.where(row < (i + 1) * sub, starts[i] - g, -jnp.inf))
        k_fac += [ki.astype(BF16), pad]
    wide = _dot_nt(q_fac, jnp.concatenate(k_fac, axis=0))
    scores = jnp.concatenate([wide[i * sub:(i + 1) * sub, i * LANES:i * LANES + c] for i in range(n_sub)], axis=0)
    causal = lax.broadcasted_iota(jnp.int32, (c, c), 0) >= lax.broadcasted_iota(jnp.int32, (c, c), 1)
    return _dot(jnp.where(causal, scores, 0.0).astype(BF16), v.astype(BF16))


def _hgrn_intra_exact(q, kk, v, g):
    c, sub = HG_CHUNK, HG_SUB
    row = lax.broadcasted_iota(jnp.int32, (c, 1), 0)
    row_in_sub = row % sub
    neg = -jnp.inf
    blocks = [jnp.zeros((sub, c), F32)]
    for i in range(1, c // sub):
        g_start = g[i * sub - 1:i * sub, :]
        qi = q[i * sub:(i + 1) * sub, :] * jnp.exp(g[i * sub:(i + 1) * sub, :] - g_start)
        ki = kk * jnp.exp(jnp.where(row < i * sub, g_start - g, neg))
        blocks.append(_dot_nt(qi.astype(BF16), ki.astype(BF16)))
    o = _dot(jnp.concatenate(blocks, axis=0).astype(BF16), v.astype(BF16))
    o = o + jnp.sum(q * kk, axis=-1, keepdims=True) * v
    for d in range(1, sub):
        gs = pltpu.roll(g, d, 0)
        ks = pltpu.roll(kk, d, 0)
        vs = pltpu.roll(v, d, 0)
        dec = jnp.exp(jnp.where(row_in_sub >= d, g - gs, neg))
        o = o + jnp.sum(q * ks * dec, axis=-1, keepdims=True) * vs
    return o


def _hgrn_prompt_kernel(q_ref, f_ref, i_ref, g_ref, lb_ref, gain_ref, y_ref, sfin_ref, st_sc, inter_sc,
                        *, n_chunks):
    c = HG_CHUNK
    sub = HG_SUB
    hd = HG_HEAD_DIM
    n_h = st_sc.shape[0]
    t = pl.program_id(2)

    @pl.when(t == 0)
    def _():
        st_sc[...] = jnp.zeros_like(st_sc)

    gain = gain_ref[...]
    lb_all = lb_ref[...]
    tri = (lax.broadcasted_iota(jnp.int32, (c, c), 0) >= lax.broadcasted_iota(jnp.int32, (c, c), 1)).astype(BF16)

    def chunk(ci, carry):
        r0 = pl.multiple_of(ci * c, c)

        def forget_and_decay():
            f = lb_all + (1.0 - lb_all) * _sigmoid(f_ref[pl.ds(r0, c), :])
            log_f = jnp.log(f)
            hi = log_f.astype(BF16)
            rest = log_f - hi.astype(F32)
            mid = rest.astype(BF16)
            lo = (rest - mid.astype(F32)).astype(BF16)
            sums = _dot(tri, jnp.concatenate([hi, mid, lo], axis=1))
            w = n_h * hd
            return f, (sums[:, 2 * w:] + sums[:, w:2 * w]) + sums[:, :w]

        def operands(hh, f, g):
            lanes = slice(hh * hd, (hh + 1) * hd)
            return _silu(q_ref[pl.ds(r0, c), lanes]), 1.0 - f[:, lanes], i_ref[pl.ds(r0, c), lanes], g[:, lanes]

        def finish(o, hh):
            lanes = slice(hh * hd, (hh + 1) * hd)
            o = o * lax.rsqrt(jnp.mean(o * o, axis=-1, keepdims=True) + RMS_EPS) * gain
            y_ref[pl.ds(r0, c), lanes] = o * _silu(g_ref[pl.ds(r0, c), lanes])

        f_all, g_all = forget_and_decay()
        span = jnp.zeros((1, hd), F32)
        for hh in range(n_h):
            q, kk, v, g = operands(hh, f_all, g_all)
            for i in range(c // sub):
                g_start = g[i * sub - 1:i * sub, :] if i else jnp.zeros_like(g[0:1, :])
                span = jnp.maximum(span, g_start - g[(i + 1) * sub - 1:(i + 1) * sub, :])
            g_last = g[c - 1:c, :]
            st = st_sc[hh]
            inter = _dot_nt((q * jnp.exp(g)).astype(BF16), st.astype(BF16))
            inter_sc[hh] = inter
            finish(inter + _hgrn_intra_factored(q, kk, v, g), hh)
            k_dec = kk * jnp.exp(g_last - g)
            st_sc[hh] = st * jnp.exp(g_last) + _dot_tn(v.astype(BF16), k_dec.astype(BF16))

        @pl.when(jnp.max(span) > HG_SAFE_SPAN)
        def _():
            f_again, g_again = forget_and_decay()
            for hh in range(n_h):
                q, kk, v, g = operands(hh, f_again, g_again)
                finish(inter_sc[hh] + _hgrn_intra_exact(q, kk, v, g), hh)

        return carry

    lax.fori_loop(0, n_chunks, chunk, 0)

    @pl.when(t == pl.num_programs(2) - 1)
    def _():
        for hh in range(n_h):
            sfin_ref[0, hh] = st_sc[hh].T


def _hgrn_prompt(z, lb, gain, bt, t):
    hd = HG_HEAD_DIM
    n_heads = lb.shape[0] // hd
    n_h = HG_HEADS_PER_STEP
    groups = n_heads // n_h
    w = n_h * hd
    tb = min(512, t)
    nt = t // tb
    m = bt * t

    def col(block):
        return pl.BlockSpec((tb, w), lambda b, h, s: (b * nt + s, block * groups + h))

    return pl.pallas_call(
        functools.partial(_hgrn_prompt_kernel, n_chunks=tb // HG_CHUNK),
        grid=(bt, groups, nt),
        in_specs=[col(1), col(2), col(3), col(4),
                  pl.BlockSpec((1, w), lambda b, h, s: (0, h)),
                  pl.BlockSpec((1, hd), lambda b, h, s: (0, 0))],
        out_specs=[pl.BlockSpec((tb, w), lambda b, h, s: (b * nt + s, h)),
                   pl.BlockSpec((1, n_h, hd, hd), lambda b, h, s: (b, h, 0, 0))],
        out_shape=[jax.ShapeDtypeStruct((m, n_heads * hd), F32),
                   jax.ShapeDtypeStruct((bt, n_heads, hd, hd), F32)],
        scratch_shapes=[pltpu.VMEM((n_h, hd, hd), F32), pltpu.VMEM((n_h, HG_CHUNK, hd), F32)],
        compiler_params=_params(("parallel", "parallel", "arbitrary")),
        name="hgrn_prompt",
    )(z, z, z, z, lb.reshape(1, -1), gain.reshape(1, hd))


HG_SAMPLE_KEYS = 32


def _hgrn_sample_kernel(q_ref, f_ref, i_ref, g_ref, lb_ref, gain_ref, s_ref, y_ref, snew_ref,
                        qt_sc, ft_sc, kt_sc, vt_sc, o_sc):
    kb = pl.program_id(1)

    @pl.when(kb == 0)
    def _():
        lb = lb_ref[...]
        f = lb + (1.0 - lb) * _sigmoid(f_ref[...])
        qt_sc[...] = _silu(q_ref[...]).T
        ft_sc[...] = f.T
        kt_sc[...] = (1.0 - f).T
        vt_sc[...] = i_ref[...].T
        o_sc[...] = jnp.zeros_like(o_sc)

    vt = vt_sc[...]
    acc = o_sc[...]
    for kl in range(HG_SAMPLE_KEYS):
        k = kb * HG_SAMPLE_KEYS + kl
        st = s_ref[:, 0, kl, :].T
        s_new = st * ft_sc[pl.ds(k, 1), :] + vt * kt_sc[pl.ds(k, 1), :]
        acc = acc + s_new * qt_sc[pl.ds(k, 1), :]
        snew_ref[:, 0, kl, :] = s_new.T
    o_sc[...] = acc

    @pl.when(kb == pl.num_programs(1) - 1)
    def _():
        o = acc.T
        o = o * lax.rsqrt(jnp.mean(o * o, axis=-1, keepdims=True) + RMS_EPS) * gain_ref[...]
        y_ref[...] = o * _silu(g_ref[...])


def _hgrn_sample(z, states, layer, lb, gain):
    _, b, n_heads, hd, _ = states.shape
    kbs = HG_SAMPLE_KEYS

    def col(block):
        return pl.BlockSpec((b, hd), lambda h, k: (0, block * n_heads + h))

    sblk = pl.BlockSpec((b, 1, kbs, hd), lambda h, k: (0, h, k, 0))
    return pl.pallas_call(
        _hgrn_sample_kernel,
        grid=(n_heads, hd // kbs),
        in_specs=[col(1), col(2), col(3), col(4),
                  pl.BlockSpec((1, hd), lambda h, k: (0, h)),
                  pl.BlockSpec((1, hd), lambda h, k: (0, 0)),
                  pl.BlockSpec((None, b, 1, kbs, hd), lambda h, k: (layer, 0, h, k, 0))],
        out_specs=[pl.BlockSpec((b, hd), lambda h, k: (0, h)), sblk],
        out_shape=[jax.ShapeDtypeStruct((b, n_heads * hd), F32),
                   jax.ShapeDtypeStruct(states.shape[1:], F32)],
        scratch_shapes=[pltpu.VMEM((hd, b), F32)] * 5,
        compiler_params=_params(("parallel", "arbitrary")),
        name="hgrn_sample",
    )(z, z, z, z, lb.reshape(1, -1), gain.reshape(1, hd), states)


def _mix_to_hidden(x, z, y_a_pre, y_b, mw, dense, layer, alpha, tm):
    y_a = _glu(y_a_pre, dense["w_glu"], layer, tm)
    merged = _merge(y_a, y_b, z, dense["w_ba"], dense["w_bb"], layer, tm)
    return _ln1(merged, x, dense["w_o"], mw["ln1_g"], mw["ln1_b"], layer, alpha, min(tm, 256))


def _layer_prompt(x, p, mw, lw, layer, alpha, bt, t):
    m = bt * t
    tm = min(512, m)
    dense = mw["dense_bf16"]
    z = _inproj(x, dense["w_in"], layer, min(INPROJ_ROWS, m))
    y_a_pre, f_re, f_im = _s5_prompt(z, lw["s5_prompt_ops"], bt, t)
    y_b, hg_new = _hgrn_prompt(z, lw["lb"], lw["gn_gain"], bt, t)
    h = _mix_to_hidden(x, z, y_a_pre, y_b, mw, dense, layer, alpha, tm)
    s1, s2, cw, counts = _router_sorted(h, lw["w_router"], lw["b_router"], tm)
    pos8, tab = _plan(s1, s2, counts, tm)
    pos = pos8[:2].reshape(2 * m)
    n_tiles = _moe_num_tiles(m)
    tile_expert, tile_rows, n_used = tab[0, :n_tiles], tab[1, :n_tiles], tab[2, :1]
    xs = _dispatch(h, pos, tile_rows, tm)
    ys = _experts_sorted(xs, tile_expert, tile_rows, n_used, mw["w_g"], mw["w_u"], mw["w_d"], layer)
    x_new = _final_gather(h, cw, p, pos, ys, dense["w_pg"], dense["w_pp"], mw["ln2_g"], mw["ln2_b"], layer,
                          alpha, min(tm, 256))
    return x_new, f_re, f_im, hg_new


def _layer_sample(x, p, s_re, s_im, hg_states, mw, lw, layer, alpha):
    m = x.shape[0]
    dense = mw["dense_f32"]
    z = _inproj(x, dense["w_in"], layer, m)
    y_a_pre, n_re, n_im = _s5_sample(z, s_re, s_im, lw["s5_sample_ops"])
    y_b, hg_new = _hgrn_sample(z, hg_states, layer, lw["lb"], lw["gn_gain"])
    h = _mix_to_hidden(x, z, y_a_pre, y_b, mw, dense, layer, alpha, m)
    comb = _router_dense(h, lw["w_router"], lw["b_router"], m)
    moe = _moe_dense(h, comb, mw["w_g"], mw["w_u"], mw["w_d"], layer, m)
    x_new = _final(h, moe, p, dense["w_pg"], dense["w_pp"], mw["ln2_g"], mw["ln2_b"], layer, alpha, m)
    return x_new, n_re, n_im, hg_new


def _layer_operands(i, lb_all, s5_lambda_re, s5_lambda_im, s5_log_dt, s5_b_re, s5_b_im, s5_c_re, s5_c_im, s5_d,
                    hg_norm_gain, w_group_router, b_group_router, w_expert_router, b_expert_router):
    disc = _s5_discretise(s5_lambda_re[i], s5_lambda_im[i], s5_log_dt[i], s5_b_re[i], s5_b_im[i])
    d_model = w_group_router.shape[1]
    pad = LANES - N_EXPERTS - N_GROUPS
    w_router = jnp.concatenate([w_expert_router[i], w_group_router[i], jnp.zeros((d_model, pad), F32)], axis=1)
    b_router = jnp.concatenate([b_expert_router[i], b_group_router[i], jnp.zeros((pad,), F32)]).reshape(1, LANES)
    return {
        "s5_width": s5_d.shape[1],
        "s5_prompt_ops": _s5_prompt_operators(*disc, s5_c_re[i], s5_c_im[i], s5_d[i]),
        "s5_sample_ops": _s5_sample_operators(*disc, s5_c_re[i], s5_c_im[i], s5_d[i]),
        "lb": lb_all[i],
        "gn_gain": hg_norm_gain[i],
        "w_router": w_router, "b_router": b_router,
    }


def kernel(x_prompt, x_sample, p_prompt, p_sample, state_s5_re, state_s5_im, state_hgrn, w_in, s5_lambda_re, s5_lambda_im, s5_log_dt, s5_b_re, s5_b_im, s5_c_re, s5_c_im, s5_d, s5_w_glu, hg_lower_bounds, hg_norm_gain, w_branch_a, w_branch_b, w_out, ln1_gain, ln1_bias, w_group_router, b_group_router, w_expert_router, b_expert_router, w_exp_gate, w_exp_up, w_exp_down, w_ple_proj, w_ple_gate, ln2_gain, ln2_bias):
    depth = w_in.shape[0]
    bt, t, d_model = x_prompt.shape
    bs = x_sample.shape[0]
    alpha = (2 * depth) ** 0.25
    n_groups, n_state = s5_lambda_re.shape[1:]

    lb_soft = jax.nn.softmax(hg_lower_bounds.astype(F32), axis=0)
    lb_all = jnp.cumsum(lb_soft, axis=0) - lb_soft[0]

    dense_f32 = {"w_in": w_in, "w_glu": s5_w_glu, "w_ba": w_branch_a, "w_bb": w_branch_b,
                 "w_o": w_out, "w_pp": w_ple_proj, "w_pg": w_ple_gate}
    mw = {"dense_f32": dense_f32,
          "dense_bf16": {k: v.astype(BF16) for k, v in dense_f32.items()},
          "w_g": w_exp_gate, "w_u": w_exp_up, "w_d": w_exp_down,
          "ln1_g": ln1_gain, "ln1_b": ln1_bias, "ln2_g": ln2_gain, "ln2_b": ln2_bias}
    pp = p_prompt.reshape(depth, bt * t, -1)
    ps = p_sample.reshape(depth, bs, -1)

    xp = x_prompt.reshape(bt * t, d_model)
    xs = x_sample.reshape(bs, d_model)
    outs = [[] for _ in range(6)]
    for i in range(depth):
        lw = _layer_operands(i, lb_all, s5_lambda_re, s5_lambda_im, s5_log_dt, s5_b_re, s5_b_im, s5_c_re,
                             s5_c_im, s5_d, hg_norm_gain, w_group_router, b_group_router, w_expert_router,
                             b_expert_router)
        xp, a_re, a_im, a_hg = _layer_prompt(xp, pp, mw, lw, i, alpha, bt, t)
        xs, b_re, b_im, b_hg = _layer_sample(
            xs, ps, state_s5_re[i].reshape(bs, n_groups * n_state), state_s5_im[i].reshape(bs, n_groups * n_state),
            state_hgrn, mw, lw, i, alpha)
        for lst, val in zip(outs, (a_re.reshape(bt, n_groups, n_state), a_im.reshape(bt, n_groups, n_state), a_hg,
                                   b_re.reshape(bs, n_groups, n_state), b_im.reshape(bs, n_groups, n_state), b_hg)):
            lst.append(val)

    return (xp.reshape(bt, t, d_model), xs.reshape(bs, 1, d_model), *[jnp.stack(o) for o in outs])
```

```python
import functools
import math

import jax
import jax.numpy as jnp
from jax import lax
from jax.experimental import pallas as pl
from jax.experimental.pallas import tpu as pltpu

F32 = jnp.float32
BF16 = jnp.bfloat16
HIGHEST = lax.Precision.HIGHEST

LANES = 128
SUBLANES = 8
VMEM_LIMIT_BYTES = 56 * 1024 * 1024

S5_GROUP = 16
S5_STATE = 64
S5_CHUNK = 16
HG_HEAD_DIM = 128
HG_CHUNK = 64
HG_SUB = 16
N_GROUPS = 4
EXPERTS_PER_GROUP = 8
N_EXPERTS = N_GROUPS * EXPERTS_PER_GROUP
LN_EPS = 1e-5
RMS_EPS = 1e-6


def _params(semantics):
    return pltpu.CompilerParams(dimension_semantics=semantics, vmem_limit_bytes=VMEM_LIMIT_BYTES)


def _dot(a, b):
    return jnp.dot(a, b, preferred_element_type=F32)


def _mm(a, w):
    if w.dtype == F32:
        a_hi = a.astype(BF16)
        a_lo = (a - a_hi.astype(F32)).astype(BF16)
        w_hi = w.astype(BF16)
        w_lo = (w - w_hi.astype(F32)).astype(BF16)
        return _dot(a_hi, w_hi) + (_dot(a_hi, w_lo) + _dot(a_lo, w_hi))
    return jnp.dot(a.astype(BF16), w, preferred_element_type=F32)


def _dot_nt(a, b):
    return lax.dot_general(a, b, (((1,), (1,)), ((), ())), preferred_element_type=F32)


def _dot_tn(a, b):
    return lax.dot_general(a, b, (((0,), (0,)), ((), ())), preferred_element_type=F32)


def _sigmoid(x):
    return 0.5 * jnp.tanh(0.5 * x) + 0.5


def _silu(x):
    return x * _sigmoid(x)


def _layer_norm_rows(x, gain, bias):
    mu = jnp.mean(x, axis=-1, keepdims=True)
    xc = x - mu
    var = jnp.mean(xc * xc, axis=-1, keepdims=True)
    return xc * lax.rsqrt(var + LN_EPS) * gain + bias


INPROJ_ROWS = 1024


def _inproj_kernel(x_ref, w_ref, o_ref, xb_ref):
    if w_ref.dtype == F32:
        o_ref[...] = _mm(x_ref[...], w_ref[...])
        return

    @pl.when(pl.program_id(1) == 0)
    def _():
        xb_ref[...] = x_ref[...].astype(BF16)

    o_ref[...] = _dot(xb_ref[...], w_ref[...])


def _layer_spec(block, layer, index_map):
    return pl.BlockSpec((None,) + block, lambda *idx: (layer,) + index_map(*idx))


def _inproj(x, w, layer, tm, tn=1024):
    m, k = x.shape
    n = w.shape[-1]
    return pl.pallas_call(
        _inproj_kernel,
        grid=(m // tm, n // tn),
        in_specs=[pl.BlockSpec((tm, k), lambda i, j: (i, 0)),
                  _layer_spec((k, tn), layer, lambda i, j: (0, j))],
        out_specs=pl.BlockSpec((tm, tn), lambda i, j: (i, j)),
        out_shape=jax.ShapeDtypeStruct((m, n), F32),
        scratch_shapes=[pltpu.VMEM((tm, k), BF16)],
        compiler_params=_params(("parallel", "arbitrary")),
        name="inproj",
    )(x, w)


def _glu_kernel(a_ref, w_ref, o_ref):
    a = a_ref[...]
    o_ref[...] = a * _sigmoid(_mm(a, w_ref[...]))


def _glu(a, w, layer, tm):
    m, k = a.shape
    return pl.pallas_call(
        _glu_kernel,
        grid=(m // tm,),
        in_specs=[pl.BlockSpec((tm, k), lambda i: (i, 0)),
                  _layer_spec((k, k), layer, lambda i: (0, 0))],
        out_specs=pl.BlockSpec((tm, k), lambda i: (i, 0)),
        out_shape=jax.ShapeDtypeStruct((m, k), F32),
        compiler_params=_params(("parallel",)),
        name="s5_glu",
    )(a, w)


def _merge_kernel(ya_ref, yb_ref, ga_ref, gb_ref, wa_ref, wb_ref, o_ref):
    pa = _mm(ya_ref[...], wa_ref[...])
    pb = _mm(yb_ref[...], wb_ref[...])
    o_ref[...] = _sigmoid(ga_ref[...]) * pa + _sigmoid(gb_ref[...]) * pb


def _merge(ya, yb, z, wa, wb, layer, tm, tn=1024):
    m, k = ya.shape
    n = wa.shape[-1]
    nj = n // tn
    return pl.pallas_call(
        _merge_kernel,
        grid=(m // tm, nj),
        in_specs=[pl.BlockSpec((tm, k), lambda i, j: (i, 0)),
                  pl.BlockSpec((tm, k), lambda i, j: (i, 0)),
                  pl.BlockSpec((tm, tn), lambda i, j: (i, 5 + j)),
                  pl.BlockSpec((tm, tn), lambda i, j: (i, 5 + nj + j)),
                  _layer_spec((k, tn), layer, lambda i, j: (0, j)),
                  _layer_spec((k, tn), layer, lambda i, j: (0, j))],
        out_specs=pl.BlockSpec((tm, tn), lambda i, j: (i, j)),
        out_shape=jax.ShapeDtypeStruct((m, n), F32),
        compiler_params=_params(("parallel", "arbitrary")),
        name="branch_merge",
    )(ya, yb, z, z, wa, wb)


def _ln1_kernel(mg_ref, x_ref, w_ref, g_ref, b_ref, o_ref, *, alpha):
    r = alpha * x_ref[...] + _mm(mg_ref[...], w_ref[...])
    o_ref[...] = _layer_norm_rows(r, g_ref[...], b_ref[...])


def _ln1(merged, x, w, gain, bias, layer, alpha, tm):
    m, d = x.shape
    row = pl.BlockSpec((tm, d), lambda i: (i, 0))
    vec = _layer_spec((1, d), layer, lambda i: (0, 0))
    return pl.pallas_call(
        functools.partial(_ln1_kernel, alpha=alpha),
        grid=(m // tm,),
        in_specs=[row, row, _layer_spec((d, d), layer, lambda i: (0, 0)), vec, vec],
        out_specs=row,
        out_shape=jax.ShapeDtypeStruct((m, d), F32),
        compiler_params=_params(("parallel",)),
        name="outproj_ln1",
    )(merged, x, w, gain[:, None, :], bias[:, None, :])


def _final_kernel(h_ref, moe_ref, p_ref, wg_ref, wp_ref, g_ref, b_ref, o_ref, *, alpha):
    h = h_ref[...]
    gate = _sigmoid(_mm(h, wg_ref[...]))
    proj = _mm(p_ref[...], wp_ref[...])
    r = alpha * h + moe_ref[...] + gate * proj
    o_ref[...] = _layer_norm_rows(r, g_ref[...], b_ref[...])


def _final(h, moe, p, wg, wp, gain, bias, layer, alpha, tm):
    m, d = h.shape
    dp = p.shape[-1]
    row = pl.BlockSpec((tm, d), lambda i: (i, 0))
    vec = _layer_spec((1, d), layer, lambda i: (0, 0))
    return pl.pallas_call(
        functools.partial(_final_kernel, alpha=alpha),
        grid=(m // tm,),
        in_specs=[row, row, _layer_spec((tm, dp), layer, lambda i: (i, 0)),
                  _layer_spec((d, d), layer, lambda i: (0, 0)),
                  _layer_spec((dp, d), layer, lambda i: (0, 0)), vec, vec],
        out_specs=row,
        out_shape=jax.ShapeDtypeStruct((m, d), F32),
        compiler_params=_params(("parallel",)),
        name="ple_ln2",
    )(h, moe, p, wg, wp, gain[:, None, :], bias[:, None, :])


def _route(h, w, b):
    logits = jnp.dot(h, w, precision=HIGHEST, preferred_element_type=F32) + b
    lane = lax.broadcasted_iota(jnp.int32, logits.shape, 1)
    lane_f = lane.astype(F32)
    neg = -jnp.inf
    gl = jnp.where(lane >= N_EXPERTS, jnp.where(lane < N_EXPERTS + N_GROUPS, logits, neg), neg)
    gmax = jnp.max(gl, axis=-1, keepdims=True)
    denom = jnp.sum(jnp.exp(gl - gmax), axis=-1, keepdims=True)
    grp_p = 1.0 / denom
    gidx = jnp.min(jnp.where(gl == gmax, lane_f, 1e9), axis=-1, keepdims=True) - N_EXPERTS
    lane_grp = (lane // EXPERTS_PER_GROUP).astype(F32)
    el = jnp.where(lane < N_EXPERTS, jnp.where(lane_grp == gidx, logits, neg), neg)
    t1 = jnp.max(el, axis=-1, keepdims=True)
    i1 = jnp.min(jnp.where(el == t1, lane_f, 1e9), axis=-1, keepdims=True)
    el2 = jnp.where(lane_f == i1, neg, el)
    t2 = jnp.max(el2, axis=-1, keepdims=True)
    i2 = jnp.min(jnp.where(el2 == t2, lane_f, 1e9), axis=-1, keepdims=True)
    e2 = jnp.exp(t2 - t1)
    w1 = 1.0 / (1.0 + e2)
    w2 = e2 * w1
    return lane, lane_f, i1, i2, grp_p * w1, grp_p * w2


def _router_dense_kernel(h_ref, w_ref, b_ref, o_ref):
    _, lane_f, i1, i2, c1, c2 = _route(h_ref[...], w_ref[...], b_ref[...])
    o_ref[...] = jnp.where(lane_f == i1, c1, jnp.where(lane_f == i2, c2, 0.0))


def _router_dense(h, w_r, b_r, tm):
    m, d = h.shape
    return pl.pallas_call(
        _router_dense_kernel,
        grid=(m // tm,),
        in_specs=[pl.BlockSpec((tm, d), lambda i: (i, 0)),
                  pl.BlockSpec((d, LANES), lambda i: (0, 0)),
                  pl.BlockSpec((1, LANES), lambda i: (0, 0))],
        out_specs=pl.BlockSpec((tm, LANES), lambda i: (i, 0)),
        out_shape=jax.ShapeDtypeStruct((m, LANES), F32),
        compiler_params=_params(("parallel",)),
        name="router_dense",
    )(h, w_r, b_r)


def _moe_dense_kernel(h_ref, c_ref, wg_ref, wu_ref, wd_ref, o_ref):
    e = pl.program_id(1)

    @pl.when(e == 0)
    def _():
        o_ref[...] = jnp.zeros_like(o_ref)

    comb = c_ref[...]
    lane = lax.broadcasted_iota(jnp.int32, comb.shape, 1)
    col = jnp.sum(jnp.where(lane == e, comb, 0.0), axis=-1, keepdims=True)
    h = h_ref[...]
    hid = _silu(_mm(h, wg_ref[0])) * _mm(h, wu_ref[0]) * col
    o_ref[...] += _mm(hid, wd_ref[0])


def _moe_dense(h, comb, wg, wu, wd, layer, tm):
    m, d = h.shape
    _, n_e, _, f = wg.shape
    return pl.pallas_call(
        _moe_dense_kernel,
        grid=(m // tm, n_e),
        in_specs=[pl.BlockSpec((tm, d), lambda i, e: (i, 0)),
                  pl.BlockSpec((tm, LANES), lambda i, e: (i, 0)),
                  _layer_spec((1, d, f), layer, lambda i, e: (e, 0, 0)),
                  _layer_spec((1, d, f), layer, lambda i, e: (e, 0, 0)),
                  _layer_spec((1, f, d), layer, lambda i, e: (e, 0, 0))],
        out_specs=pl.BlockSpec((tm, d), lambda i, e: (i, 0)),
        out_shape=jax.ShapeDtypeStruct((m, d), F32),
        compiler_params=_params(("parallel", "arbitrary")),
        name="moe_dense",
    )(h, comb, wg, wu, wd)


MOE_TILE = 256


def _moe_num_tiles(m):
    return 2 * m // MOE_TILE + N_EXPERTS


def _router_sorted_kernel(h_ref, w_ref, b_ref, s1_ref, s2_ref, cw_ref, cnt_ref, carry_sc):
    @pl.when(pl.program_id(0) == 0)
    def _():
        carry_sc[...] = jnp.zeros_like(carry_sc)

    lane, lane_f, i1, i2, c1, c2 = _route(h_ref[...], w_ref[...], b_ref[...])
    sel1 = jnp.where(lane_f == i1, 1.0, 0.0)
    sel2 = jnp.where(lane_f == i2, 1.0, 0.0)
    cnt = sel1 + sel2
    tm = cnt.shape[0]
    strict_lower = (lax.broadcasted_iota(jnp.int32, (tm, tm), 0) > lax.broadcasted_iota(jnp.int32, (tm, tm), 1))
    prefix = _dot(strict_lower.astype(BF16), cnt.astype(BF16)) + carry_sc[0:1, :]
    s1_ref[...] = sel1 * (prefix + 1.0)
    s2_ref[...] = sel2 * (prefix + 1.0)
    cw_ref[...] = jnp.where(lane == 0, c1, jnp.where(lane == 1, c2, 0.0))
    carry_sc[...] = carry_sc[...] + jnp.sum(cnt, axis=0, keepdims=True)
    cnt_ref[...] = carry_sc[...]


def _router_sorted(h, w_r, b_r, tm):
    m, d = h.shape
    tok = pl.BlockSpec((tm, LANES), lambda i: (i, 0))
    return pl.pallas_call(
        _router_sorted_kernel,
        grid=(m // tm,),
        in_specs=[pl.BlockSpec((tm, d), lambda i: (i, 0)),
                  pl.BlockSpec((d, LANES), lambda i: (0, 0)),
                  pl.BlockSpec((1, LANES), lambda i: (0, 0))],
        out_specs=[tok, tok, tok, pl.BlockSpec((SUBLANES, LANES), lambda i: (0, 0))],
        out_shape=[jax.ShapeDtypeStruct((m, LANES), F32)] * 3 + [jax.ShapeDtypeStruct((SUBLANES, LANES), F32)],
        scratch_shapes=[pltpu.VMEM((SUBLANES, LANES), F32)],
        compiler_params=_params(("arbitrary",)),
        name="router_sorted",
    )(h, w_r, b_r)


def _plan_kernel(s1_ref, s2_ref, cnt_ref, pos_ref, tab_ref):
    te = float(MOE_TILE)
    lane = lax.broadcasted_iota(jnp.int32, (1, LANES), 1)
    cnt = jnp.where(lane < N_EXPERTS, cnt_ref[0:1, :], 0.0)
    padded = jnp.floor((cnt + (te - 1.0)) * (1.0 / te)) * te
    r128 = lax.broadcasted_iota(jnp.int32, (LANES, LANES), 0)
    c128 = lax.broadcasted_iota(jnp.int32, (LANES, LANES), 1)
    before = jnp.where(r128 < c128, 1.0, 0.0)
    off = jnp.dot(jnp.broadcast_to(padded, (SUBLANES, LANES)), before, precision=HIGHEST,
                  preferred_element_type=F32)[0:1, :]
    s1 = s1_ref[...]
    s2 = s2_ref[...]
    v1 = jnp.where(s1 > 0.0, s1 - 1.0 + off, 0.0)
    v2 = jnp.where(s2 > 0.0, s2 - 1.0 + off, 0.0)
    ones8 = jnp.ones((SUBLANES, LANES), F32)
    p1 = lax.dot_general(ones8, v1, (((1,), (1,)), ((), ())), precision=HIGHEST, preferred_element_type=F32)
    p2 = lax.dot_general(ones8, v2, (((1,), (1,)), ((), ())), precision=HIGHEST, preferred_element_type=F32)
    row8 = lax.broadcasted_iota(jnp.int32, p1.shape, 0)
    pos_ref[...] = jnp.where(row8 == 0, p1, p2).astype(jnp.int32)

    def per_expert(row):
        return jnp.broadcast_to(row, (LANES, LANES)).T
    off_e = per_expert(off)
    end_e = per_expert(off + padded)
    cnt_e = per_expert(cnt)
    start = c128.astype(F32) * te
    is_e = r128 < N_EXPERTS
    tile_e = jnp.sum(jnp.where(is_e, jnp.where(end_e <= start, 1.0, 0.0), 0.0), axis=0, keepdims=True)
    rows = jnp.clip(cnt_e - (start - off_e), 0.0, te)
    owns = jnp.where(is_e, jnp.where(off_e <= start, jnp.where(start < end_e, rows, 0.0), 0.0), 0.0)
    n_rows = jnp.sum(owns, axis=0, keepdims=True)
    tile_e = jnp.minimum(tile_e, float(N_EXPERTS - 1))
    n_used = jnp.sum(jnp.where(n_rows > 0.0, 1.0, 0.0), axis=-1, keepdims=True)
    row_t = lax.broadcasted_iota(jnp.int32, (SUBLANES, LANES), 0)
    tab_ref[...] = jnp.where(row_t == 0, tile_e, jnp.where(row_t == 1, n_rows, n_used)).astype(jnp.int32)


def _plan(s1, s2, counts, tm):
    m = s1.shape[0]
    tok = pl.BlockSpec((tm, LANES), lambda i: (i, 0))
    return pl.pallas_call(
        _plan_kernel,
        grid=(m // tm,),
        in_specs=[tok, tok, pl.BlockSpec((SUBLANES, LANES), lambda i: (0, 0))],
        out_specs=[pl.BlockSpec((SUBLANES, tm), lambda i: (0, i)),
                   pl.BlockSpec((SUBLANES, LANES), lambda i: (0, 0))],
        out_shape=[jax.ShapeDtypeStruct((SUBLANES, m), jnp.int32),
                   jax.ShapeDtypeStruct((SUBLANES, LANES), jnp.int32)],
        compiler_params=_params(("arbitrary",)),
        name="moe_plan",
    )(s1, s2, counts)


DMA_UNROLL = 8


def _dispatch_kernel(pos_ref, rows_ref, h_ref, xs_hbm, zero_buf, zero_sem, sem, *, m, n_tiles):
    i = pl.program_id(0)
    tm = h_ref.shape[0]

    @pl.when(i == 0)
    def _():
        zero_buf[...] = jnp.zeros_like(zero_buf)

        def zero_copy(j):
            return pltpu.make_async_copy(zero_buf, xs_hbm.at[pl.ds(j * MOE_TILE, MOE_TILE)], zero_sem)

        def zero_start(j, c):
            @pl.when(rows_ref[j] < MOE_TILE)
            def _():
                zero_copy(j).start()
            return c

        def zero_wait(j, c):
            @pl.when(rows_ref[j] < MOE_TILE)
            def _():
                zero_copy(j).wait()
            return c

        lax.fori_loop(0, n_tiles, zero_start, 0)
        lax.fori_loop(0, n_tiles, zero_wait, 0)

    def body(r, c):
        t = i * tm + r
        src = h_ref.at[pl.ds(r, 1)]
        pltpu.make_async_copy(src, xs_hbm.at[pl.ds(pos_ref[t], 1)], sem).start()
        pltpu.make_async_copy(src, xs_hbm.at[pl.ds(pos_ref[m + t], 1)], sem).start()
        return c
    lax.fori_loop(0, tm, body, 0, unroll=DMA_UNROLL)

    pltpu.make_async_copy(xs_hbm.at[pl.ds(0, 2 * tm)], xs_hbm.at[pl.ds(0, 2 * tm)], sem).wait()


def _dispatch(h, pos, tile_rows, tm):
    m, d = h.shape
    n_tiles = _moe_num_tiles(m)
    grid_spec = pltpu.PrefetchScalarGridSpec(
        num_scalar_prefetch=2,
        grid=(m // tm,),
        in_specs=[pl.BlockSpec((tm, d), lambda i, pos, rows: (i, 0))],
        out_specs=pl.BlockSpec(memory_space=pl.ANY),
        scratch_shapes=[pltpu.VMEM((MOE_TILE, d), F32), pltpu.SemaphoreType.DMA(()),
                        pltpu.SemaphoreType.DMA(())],
    )
    return pl.pallas_call(
        functools.partial(_dispatch_kernel, m=m, n_tiles=n_tiles),
        grid_spec=grid_spec,
        out_shape=jax.ShapeDtypeStruct((n_tiles * MOE_TILE, d), F32),
        compiler_params=_params(("arbitrary",)),
        name="moe_dispatch",
    )(pos, tile_rows, h)


def _experts_kernel(te_ref, nr_ref, nu_ref, x_ref, wg_ref, wu_ref, wd_ref, y_ref, wgb, wub, wdb):
    j = pl.program_id(0)
    new_expert = jnp.logical_or(j == 0, te_ref[j] != te_ref[jnp.maximum(j - 1, 0)])

    @pl.when(jnp.logical_and(new_expert, nr_ref[j] > 0))
    def _():
        wgb[...] = wg_ref[0].astype(BF16)
        wub[...] = wu_ref[0].astype(BF16)
        wdb[...] = wd_ref[0].astype(BF16)

    @pl.when(nr_ref[j] > 0)
    def _():
        x = x_ref[...].astype(BF16)
        hid = _silu(_dot(x, wgb[...])) * _dot(x, wub[...])
        y_ref[...] = _dot(hid.astype(BF16), wdb[...])

    @pl.when(nr_ref[j] == 0)
    def _():
        y_ref[...] = jnp.zeros_like(y_ref)


def _experts_sorted(xs, tile_expert, tile_rows, n_used, wg, wu, wd, layer):
    n_rows, d = xs.shape
    _, n_e, _, f = wg.shape
    n_tiles = n_rows // MOE_TILE

    def used_tile(j, te, nr, nu):
        return (jnp.minimum(j, nu[0] - 1), 0)

    def expert_block(j, te, nr, nu):
        return (layer, te[j], 0, 0)

    grid_spec = pltpu.PrefetchScalarGridSpec(
        num_scalar_prefetch=3,
        grid=(n_tiles,),
        in_specs=[pl.BlockSpec((MOE_TILE, d), used_tile),
                  pl.BlockSpec((None, 1, d, f), expert_block),
                  pl.BlockSpec((None, 1, d, f), expert_block),
                  pl.BlockSpec((None, 1, f, d), expert_block)],
        out_specs=pl.BlockSpec((MOE_TILE, d), lambda j, te, nr, nu: (j, 0)),
        scratch_shapes=[pltpu.VMEM((d, f), BF16), pltpu.VMEM((d, f), BF16), pltpu.VMEM((f, d), BF16)],
    )
    return pl.pallas_call(
        _experts_kernel,
        grid_spec=grid_spec,
        out_shape=jax.ShapeDtypeStruct((n_rows, d), F32),
        compiler_params=_params(("arbitrary",)),
        name="moe_experts_sorted",
    )(tile_expert, tile_rows, n_used, xs, wg, wu, wd)


def _final_gather_kernel(pos_ref, h_ref, cw_ref, p_ref, wg_ref, wp_ref, g_ref, b_ref, ys_hbm, o_ref,
                         ybuf, sem, *, alpha, m):
    i = pl.program_id(0)
    slot = i % 2
    tm = h_ref.shape[0]

    last = pl.num_programs(0) - 1

    def start_rows(tile, dst_slot, r):
        t = tile * tm + r
        for pick in range(2):
            pltpu.make_async_copy(ys_hbm.at[pl.ds(pos_ref[pick * m + t], 1)],
                                  ybuf.at[dst_slot, pl.ds(pick * tm + r, 1)], sem.at[dst_slot]).start()

    def wait_tile(s):
        pltpu.make_async_copy(ys_hbm.at[pl.ds(0, 2 * tm)], ybuf.at[s], sem.at[s]).wait()

    @pl.when(i == 0)
    def _():
        def body(r, c):
            start_rows(0, 0, r)
            return c
        lax.fori_loop(0, tm, body, 0, unroll=DMA_UNROLL)

    nxt = jnp.minimum(i + 1, last)
    for r in range(tm):
        start_rows(nxt, 1 - slot, r)

    h = h_ref[...]
    gate = _sigmoid(_mm(h, wg_ref[...]))
    proj = _mm(p_ref[...], wp_ref[...])

    wait_tile(slot)
    cw = cw_ref[...]
    moe = cw[:, 0:1] * ybuf[slot, 0:tm] + cw[:, 1:2] * ybuf[slot, tm:2 * tm]
    r = alpha * h + moe + gate * proj
    o_ref[...] = _layer_norm_rows(r, g_ref[...], b_ref[...])

    @pl.when(i == last)
    def _():
        wait_tile(1 - slot)


def _final_gather(h, cw, p, pos, ys, wg, wp, gain, bias, layer, alpha, tm):
    m, d = h.shape
    dp = p.shape[-1]
    row = pl.BlockSpec((tm, d), lambda i, pos: (i, 0))
    vec = _layer_spec((1, d), layer, lambda i, pos: (0, 0))
    grid_spec = pltpu.PrefetchScalarGridSpec(
        num_scalar_prefetch=1,
        grid=(m // tm,),
        in_specs=[row, pl.BlockSpec((tm, LANES), lambda i, pos: (i, 0)),
                  _layer_spec((tm, dp), layer, lambda i, pos: (i, 0)),
                  _layer_spec((d, d), layer, lambda i, pos: (0, 0)),
                  _layer_spec((dp, d), layer, lambda i, pos: (0, 0)), vec, vec,
                  pl.BlockSpec(memory_space=pl.ANY)],
        out_specs=row,
        scratch_shapes=[pltpu.VMEM((2, 2 * tm, d), F32), pltpu.SemaphoreType.DMA((2,))],
    )
    return pl.pallas_call(
        functools.partial(_final_gather_kernel, alpha=alpha, m=m),
        grid_spec=grid_spec,
        out_shape=jax.ShapeDtypeStruct((m, d), F32),
        compiler_params=_params(("arbitrary",)),
        name="ple_ln2_gather",
    )(pos, h, cw, p, wg, wp, gain[:, None, :], bias[:, None, :], ys)


def _s5_discretise(lam_re, lam_im, log_dt, b_re, b_im):
    dt = jnp.exp(log_dt)[:, None]
    mag = jnp.exp(lam_re * dt)
    ab_re = mag * jnp.cos(lam_im * dt)
    ab_im = mag * jnp.sin(lam_im * dt)
    den = lam_re * lam_re + lam_im * lam_im
    nr = ab_re - 1.0
    zr = (nr * lam_re + ab_im * lam_im) / den
    zi = (ab_im * lam_re - nr * lam_im) / den
    bb_re = zr[..., None] * b_re - zi[..., None] * b_im
    bb_im = zr[..., None] * b_im + zi[..., None] * b_re
    return ab_re, ab_im, bb_re, bb_im


def _s5_prompt_operators(ab_re, ab_im, bb_re, bb_im, c_re, c_im, d_skip):
    g, n = ab_re.shape
    cch = bb_re.shape[-1]
    L = S5_CHUNK
    pw_re = [jnp.ones_like(ab_re)]
    pw_im = [jnp.zeros_like(ab_im)]
    for _ in range(L):
        pr, pi = pw_re[-1], pw_im[-1]
        pw_re.append(pr * ab_re - pi * ab_im)
        pw_im.append(pr * ab_im + pi * ab_re)
    a_re = jnp.stack(pw_re)
    a_im = jnp.stack(pw_im)
    w_re = a_re[:L, :, :, None] * bb_re - a_im[:L, :, :, None] * bb_im
    w_im = a_re[:L, :, :, None] * bb_im + a_im[:L, :, :, None] * bb_re
    kern = (jnp.einsum('gcn,kgnd->kgcd', c_re, w_re, precision=HIGHEST)
            - jnp.einsum('gcn,kgnd->kgcd', c_im, w_im, precision=HIGHEST))
    steps = jnp.arange(L)
    lag_is = (steps[None, None, :] - steps[None, :, None] == steps[:, None, None]).astype(F32)
    toe = jnp.einsum('kio,kgcd->gidoc', lag_is, kern, precision=HIGHEST)
    toe = toe.reshape(g, L * cch, L * cch)

    p_re = w_re[::-1].transpose(1, 0, 3, 2).reshape(g, L * cch, n)
    p_im = w_im[::-1].transpose(1, 0, 3, 2).reshape(g, L * cch, n)
    q_re = (jnp.einsum('gcn,tgn->gntc', c_re, a_re[1:]) - jnp.einsum('gcn,tgn->gntc', c_im, a_im[1:]))
    q_im = -(jnp.einsum('gcn,tgn->gntc', c_re, a_im[1:]) + jnp.einsum('gcn,tgn->gntc', c_im, a_re[1:]))
    q_re = q_re.reshape(g, n, L * cch)
    q_im = q_im.reshape(g, n, L * cch)

    hp = g // 2
    w = L * cch
    pe_re, po_re, pe_im, po_im = (x.astype(BF16) for x in (p_re[0::2], p_re[1::2], p_im[0::2], p_im[1::2]))
    qe_re, qo_re, qe_im, qo_im = (x.astype(BF16) for x in (q_re[0::2], q_re[1::2], q_im[0::2], q_im[1::2]))
    zp = jnp.zeros_like(pe_re)
    zq = jnp.zeros_like(qe_re)
    p2 = jnp.concatenate([jnp.concatenate([pe_re, zp, pe_im, zp], axis=2),
                          jnp.concatenate([zp, po_re, zp, po_im], axis=2)], axis=1)
    q2 = jnp.concatenate([jnp.concatenate([qe_re, zq], axis=2), jnp.concatenate([zq, qo_re], axis=2),
                          jnp.concatenate([qe_im, zq], axis=2), jnp.concatenate([zq, qo_im], axis=2)], axis=1)
    a_chunk = jnp.stack([a_re[L].reshape(hp, 2 * n), a_im[L].reshape(hp, 2 * n)], axis=1)
    d2 = jnp.broadcast_to(d_skip.reshape(hp, 2, 1, cch), (hp, 2, L, cch)).reshape(hp, 1, 2 * w)
    return toe.astype(BF16), p2, q2, a_chunk, d2


def _s5_prompt_kernel(u_ref, t_ref, p_ref, q_ref, a_ref, d_ref, y_ref, fre_ref, fim_ref,
                      s_sc, xin_sc, *, n_chunks, bt):
    half = a_ref.shape[-1]
    u0 = jnp.concatenate([u_ref[0, 0], u_ref[0, 1]], axis=1)
    u1 = jnp.concatenate([u_ref[1, 0], u_ref[1, 1]], axis=1)
    ub0 = u0.astype(BF16)
    ub1 = u1.astype(BF16)
    y_intra = jnp.concatenate([_dot(ub0, t_ref[0]), _dot(ub1, t_ref[1])], axis=1)
    s_sc[...] = _dot(jnp.concatenate([ub0, ub1], axis=1), p_ref[0])
    ar = a_ref[0, 0:1, :]
    ai = a_ref[0, 1:2, :]
    xr = jnp.zeros((bt, half), F32)
    xi = jnp.zeros((bt, half), F32)
    for j in range(n_chunks):
        rows = slice(j * bt, (j + 1) * bt)
        xin_sc[rows, 0:half] = xr
        xin_sc[rows, half:2 * half] = xi
        sr = s_sc[rows, 0:half]
        si = s_sc[rows, half:2 * half]
        xr, xi = ar * xr - ai * xi + sr, ar * xi + ai * xr + si
    fre_ref[...] = xr
    fim_ref[...] = xi
    y_carry = _dot(xin_sc[...].astype(BF16), q_ref[0])
    y = y_intra + y_carry + d_ref[0] * jnp.concatenate([u0, u1], axis=1)
    y = jax.nn.gelu(y)
    for k in range(4):
        y_ref[k // 2, k % 2] = y[:, k * LANES:(k + 1) * LANES]


S5_RELAYOUT_CHUNKS = 32
S5_RELAYOUT_GROUPS = LANES // S5_GROUP


def _to_chunks_kernel(x_ref, o_ref):
    L, cch = S5_CHUNK, S5_GROUP
    bt = x_ref.shape[0]
    nj = x_ref.shape[1] // L
    per_tile = LANES // cch
    for b in range(bt):
        at_pos = [x_ref[b, pl.ds(p, nj, stride=L), :] for p in range(L)]
        for g in range(S5_RELAYOUT_GROUPS):
            for h in range(L // per_tile):
                piece = jnp.concatenate(
                    [at_pos[h * per_tile + p][:, g * cch:(g + 1) * cch] for p in range(per_tile)], axis=1)
                o_ref[g, h, pl.ds(b, nj, stride=bt), :] = piece


def _from_chunks_kernel(y_ref, o_ref):
    L, cch = S5_CHUNK, S5_GROUP
    bt = o_ref.shape[0]
    nj = o_ref.shape[1] // L
    per_tile = LANES // cch
    for b in range(bt):
        tiles = [[y_ref[g, h, pl.ds(b, nj, stride=bt), :] for h in range(L // per_tile)]
                 for g in range(S5_RELAYOUT_GROUPS)]
        for p in range(L):
            h, q = divmod(p, per_tile)
            o_ref[b, pl.ds(p, nj, stride=L), :] = jnp.concatenate(
                [tiles[g][h][:, q * cch:(q + 1) * cch] for g in range(S5_RELAYOUT_GROUPS)], axis=1)


def _to_chunks(z, g, bt, t):
    L, cch, gt = S5_CHUNK, S5_GROUP, S5_RELAYOUT_GROUPS
    nj = min(S5_RELAYOUT_CHUNKS, t // L)
    n_chunks = t // L
    halves = L * cch // LANES
    return pl.pallas_call(
        _to_chunks_kernel,
        grid=(n_chunks // nj, g // gt),
        in_specs=[pl.BlockSpec((bt, nj * L, LANES), lambda j, lt: (0, j, lt))],
        out_specs=pl.BlockSpec((gt, halves, nj * bt, LANES), lambda j, lt: (lt, 0, j, 0)),
        out_shape=jax.ShapeDtypeStruct((g, halves, n_chunks * bt, LANES), F32),
        compiler_params=_params(("parallel", "parallel")),
        name="s5_to_chunks",
    )(z.reshape(bt, t, -1))


def _from_chunks(y_t, bt, t):
    L, cch, gt = S5_CHUNK, S5_GROUP, S5_RELAYOUT_GROUPS
    nj = min(S5_RELAYOUT_CHUNKS, t // L)
    g, halves = y_t.shape[:2]
    n_chunks = t // L
    return pl.pallas_call(
        _from_chunks_kernel,
        grid=(n_chunks // nj, g // gt),
        in_specs=[pl.BlockSpec((gt, halves, nj * bt, LANES), lambda j, lt: (lt, 0, j, 0))],
        out_specs=pl.BlockSpec((bt, nj * L, LANES), lambda j, lt: (0, j, lt)),
        out_shape=jax.ShapeDtypeStruct((bt, t, g * cch), F32),
        compiler_params=_params(("parallel", "parallel")),
        name="s5_from_chunks",
    )(y_t).reshape(bt * t, g * cch)


def _s5_prompt(z, ops, bt, t):
    toe, p2, q2, a_chunk, d2 = ops
    g = toe.shape[0]
    hp = g // 2
    L = S5_CHUNK
    cch = S5_GROUP
    n_chunks = t // L
    r = n_chunks * bt
    w = L * cch
    half = a_chunk.shape[-1]
    u_t = _to_chunks(z, g, bt, t)
    y_t, f_re, f_im = pl.pallas_call(
        functools.partial(_s5_prompt_kernel, n_chunks=n_chunks, bt=bt),
        grid=(hp,),
        in_specs=[pl.BlockSpec((2, w // LANES, r, LANES), lambda i: (i, 0, 0, 0)),
                  pl.BlockSpec((2, w, w), lambda i: (i, 0, 0)),
                  pl.BlockSpec((1, 2 * w, 2 * half), lambda i: (i, 0, 0)),
                  pl.BlockSpec((1, 2 * half, 2 * w), lambda i: (i, 0, 0)),
                  pl.BlockSpec((1, 2, half), lambda i: (i, 0, 0)),
                  pl.BlockSpec((1, 1, 2 * w), lambda i: (i, 0, 0))],
        out_specs=[pl.BlockSpec((2, w // LANES, r, LANES), lambda i: (i, 0, 0, 0)),
                   pl.BlockSpec((bt, half), lambda i: (0, i)),
                   pl.BlockSpec((bt, half), lambda i: (0, i))],
        out_shape=[jax.ShapeDtypeStruct((g, w // LANES, r, LANES), F32),
                   jax.ShapeDtypeStruct((bt, hp * half), F32),
                   jax.ShapeDtypeStruct((bt, hp * half), F32)],
        scratch_shapes=[pltpu.VMEM((r, 2 * half), F32), pltpu.VMEM((r, 2 * half), F32)],
        compiler_params=_params(("parallel",)),
        name="s5_prompt",
    )(u_t, toe, p2, q2, a_chunk, d2)
    return _from_chunks(y_t, bt, t), f_re, f_im


S5_SAMPLE_GROUPS = 8


def _s5_sample_operators(ab_re, ab_im, bb_re, bb_im, c_re, c_im, d_skip):
    g, n = ab_re.shape
    cch = bb_re.shape[-1]
    gb = S5_SAMPLE_GROUPS
    nb = g // gb
    eye = jnp.eye(gb, dtype=F32)
    b_ri = jnp.stack([bb_re, bb_im], axis=1).reshape(nb, gb, 2, n, cch)
    c_ri = jnp.stack([c_re, -c_im], axis=1).reshape(nb, gb, 2, cch, n)
    b8 = (b_ri.transpose(0, 1, 4, 2, 3)[:, :, :, :, None, :]
          * eye[None, :, None, None, :, None]).reshape(nb, gb * cch, 2 * gb * n)
    c8 = (c_ri.transpose(0, 2, 1, 4, 3)[:, :, :, :, None, :]
          * eye[None, None, :, None, :, None]).reshape(nb, 2 * gb * n, gb * cch)
    a8 = jnp.stack([ab_re.reshape(nb, gb * n), ab_im.reshape(nb, gb * n)], axis=1)
    d8 = d_skip.reshape(nb, 1, gb * cch)
    return b8, c8, a8, d8


def _s5_sample_kernel(u_ref, sr_ref, si_ref, b_ref, c_ref, a_ref, d_ref, y_ref, nr_ref, ni_ref):
    u = u_ref[...]
    half = sr_ref.shape[-1]
    bu = jnp.dot(u, b_ref[0], precision=HIGHEST, preferred_element_type=F32)
    ar = a_ref[0, 0:1, :]
    ai = a_ref[0, 1:2, :]
    sr = sr_ref[...]
    si = si_ref[...]
    xr = ar * sr - ai * si + bu[:, :half]
    xi = ar * si + ai * sr + bu[:, half:]
    nr_ref[...] = xr
    ni_ref[...] = xi
    y = jnp.dot(jnp.concatenate([xr, xi], axis=1), c_ref[0], precision=HIGHEST,
                preferred_element_type=F32) + d_ref[0] * u
    y_ref[...] = jax.nn.gelu(y)


def _s5_sample(z, s_re, s_im, ops):
    b8, c8, a8, d8 = ops
    nb = b8.shape[0]
    b = z.shape[0]
    wu = b8.shape[1]
    ws = a8.shape[-1]
    return pl.pallas_call(
        _s5_sample_kernel,
        grid=(nb,),
        in_specs=[pl.BlockSpec((b, wu), lambda i: (0, i)),
                  pl.BlockSpec((b, ws), lambda i: (0, i)),
                  pl.BlockSpec((b, ws), lambda i: (0, i)),
                  pl.BlockSpec((1, wu, 2 * ws), lambda i: (i, 0, 0)),
                  pl.BlockSpec((1, 2 * ws, wu), lambda i: (i, 0, 0)),
                  pl.BlockSpec((1, 2, ws), lambda i: (i, 0, 0)),
                  pl.BlockSpec((1, 1, wu), lambda i: (i, 0, 0))],
        out_specs=[pl.BlockSpec((b, wu), lambda i: (0, i)),
                   pl.BlockSpec((b, ws), lambda i: (0, i)),
                   pl.BlockSpec((b, ws), lambda i: (0, i))],
        out_shape=[jax.ShapeDtypeStruct((b, nb * wu), F32),
                   jax.ShapeDtypeStruct((b, nb * ws), F32),
                   jax.ShapeDtypeStruct((b, nb * ws), F32)],
        compiler_params=_params(("parallel",)),
        name="s5_sample",
    )(z, s_re, s_im, b8, c8, a8, d8)


HG_HEADS_PER_STEP = 8
HG_SAFE_SPAN = 60.0


def _hgrn_intra_factored(q, kk, v, g):
    c, sub = HG_CHUNK, HG_SUB
    n_sub = c // sub
    row = lax.broadcasted_iota(jnp.int32, (c, 1), 0)
    starts = [g[i * sub - 1:i * sub, :] if i else jnp.zeros_like(g[0:1, :]) for i in range(n_sub)]
    g_start_rows = jnp.concatenate([jnp.broadcast_to(s, (sub, s.shape[1])) for s in starts], axis=0)
    q_fac = (q * jnp.exp(g - g_start_rows)).astype(BF16)
    pad = jnp.zeros((LANES - c, kk.shape[1]), BF16)
    k_fac = []
    for i in range(n_sub):
        ki = kk * jnp.exp(jnp.where(row < (i + 1) * sub, starts[i] - g, -jnp.inf))
        k_fac += [ki.astype(BF16), pad]
    wide = _dot_nt(q_fac, jnp.concatenate(k_fac, axis=0))
    scores = jnp.concatenate([wide[i * sub:(i + 1) * sub, i * LANES:i * LANES + c] for i in range(n_sub)], axis=0)
    causal = lax.broadcasted_iota(jnp.int32, (c, c), 0) >= lax.broadcasted_iota(jnp.int32, (c, c), 1)
    return _dot(jnp.where(causal, scores, 0.0).astype(BF16), v.astype(BF16))


def _hgrn_intra_exact(q, kk, v, g):
    c, sub = HG_CHUNK, HG_SUB
    row = lax.broadcasted_iota(jnp.int32, (c, 1), 0)
    row_in_sub = row % sub
    neg = -jnp.inf
    blocks = [jnp.zeros((sub, c), F32)]
    for i in range(1, c // sub):
        g_start = g[i * sub - 1:i * sub, :]
        qi = q[i * sub:(i + 1) * sub, :] * jnp.exp(g[i * sub:(i + 1) * sub, :] - g_start)
        ki = kk * jnp.exp(jnp.where(row < i * sub, g_start - g, neg))
        blocks.append(_dot_nt(qi.astype(BF16), ki.astype(BF16)))
    o = _dot(jnp.concatenate(blocks, axis=0).astype(BF16), v.astype(BF16))
    o = o + jnp.sum(q * kk, axis=-1, keepdims=True) * v
    for d in range(1, sub):
        gs = pltpu.roll(g, d, 0)
        ks = pltpu.roll(kk, d, 0)
        vs = pltpu.roll(v, d, 0)
        dec = jnp.exp(jnp.where(row_in_sub >= d, g - gs, neg))
        o = o + jnp.sum(q * ks * dec, axis=-1, keepdims=True) * vs
    return o


def _hgrn_prompt_kernel(q_ref, f_ref, i_ref, g_ref, lb_ref, gain_ref, y_ref, sfin_ref, st_sc, inter_sc,
                        *, n_chunks):
    c = HG_CHUNK
    sub = HG_SUB
    hd = HG_HEAD_DIM
    n_h = st_sc.shape[0]
    t = pl.program_id(2)

    @pl.when(t == 0)
    def _():
        st_sc[...] = jnp.zeros_like(st_sc)

    gain = gain_ref[...]
    lb_all = lb_ref[...]
    tri = (lax.broadcasted_iota(jnp.int32, (c, c), 0) >= lax.broadcasted_iota(jnp.int32, (c, c), 1)).astype(BF16)

    def chunk(ci, carry):
        r0 = pl.multiple_of(ci * c, c)

        def forget_and_decay():
            f = lb_all + (1.0 - lb_all) * jax.nn.sigmoid(f_ref[pl.ds(r0, c), :])
            log_f = jnp.log(f)
            hi = log_f.astype(BF16)
            rest = log_f - hi.astype(F32)
            mid = rest.astype(BF16)
            lo = (rest - mid.astype(F32)).astype(BF16)
            sums = _dot(tri, jnp.concatenate([hi, mid, lo], axis=1))
            w = n_h * hd
            return f, (sums[:, 2 * w:] + sums[:, w:2 * w]) + sums[:, :w]

        def operands(hh, f, g):
            lanes = slice(hh * hd, (hh + 1) * hd)
            return _silu(q_ref[pl.ds(r0, c), lanes]), 1.0 - f[:, lanes], i_ref[pl.ds(r0, c), lanes], g[:, lanes]

        def finish(o, hh):
            lanes = slice(hh * hd, (hh + 1) * hd)
            o = o * lax.rsqrt(jnp.mean(o * o, axis=-1, keepdims=True) + RMS_EPS) * gain
            y_ref[pl.ds(r0, c), lanes] = o * _silu(g_ref[pl.ds(r0, c), lanes])

        f_all, g_all = forget_and_decay()
        span = jnp.zeros((1, hd), F32)
        for hh in range(n_h):
            q, kk, v, g = operands(hh, f_all, g_all)
            for i in range(c // sub):
                g_start = g[i * sub - 1:i * sub, :] if i else jnp.zeros_like(g[0:1, :])
                span = jnp.maximum(span, g_start - g[(i + 1) * sub - 1:(i + 1) * sub, :])
            g_last = g[c - 1:c, :]
            st = st_sc[hh]
            inter = _dot_nt((q * jnp.exp(g)).astype(BF16), st.astype(BF16))
            inter_sc[hh] = inter
            finish(inter + _hgrn_intra_factored(q, kk, v, g), hh)
            k_dec = kk * jnp.exp(g_last - g)
            st_sc[hh] = st * jnp.exp(g_last) + _dot_tn(v.astype(BF16), k_dec.astype(BF16))

        @pl.when(jnp.max(span) > HG_SAFE_SPAN)
        def _():
            f_again, g_again = forget_and_decay()
            for hh in range(n_h):
                q, kk, v, g = operands(hh, f_again, g_again)
                finish(inter_sc[hh] + _hgrn_intra_exact(q, kk, v, g), hh)

        return carry

    lax.fori_loop(0, n_chunks, chunk, 0)

    @pl.when(t == pl.num_programs(2) - 1)
    def _():
        for hh in range(n_h):
            sfin_ref[0, hh] = st_sc[hh].T


def _hgrn_prompt(z, lb, gain, bt, t):
    hd = HG_HEAD_DIM
    n_heads = lb.shape[0] // hd
    n_h = HG_HEADS_PER_STEP
    groups = n_heads // n_h
    w = n_h * hd
    tb = min(512, t)
    nt = t // tb
    m = bt * t

    def col(block):
        return pl.BlockSpec((tb, w), lambda b, h, s: (b * nt + s, block * groups + h))

    return pl.pallas_call(
        functools.partial(_hgrn_prompt_kernel, n_chunks=tb // HG_CHUNK),
        grid=(bt, groups, nt),
        in_specs=[col(1), col(2), col(3), col(4),
                  pl.BlockSpec((1, w), lambda b, h, s: (0, h)),
                  pl.BlockSpec((1, hd), lambda b, h, s: (0, 0))],
        out_specs=[pl.BlockSpec((tb, w), lambda b, h, s: (b * nt + s, h)),
                   pl.BlockSpec((1, n_h, hd, hd), lambda b, h, s: (b, h, 0, 0))],
        out_shape=[jax.ShapeDtypeStruct((m, n_heads * hd), F32),
                   jax.ShapeDtypeStruct((bt, n_heads, hd, hd), F32)],
        scratch_shapes=[pltpu.VMEM((n_h, hd, hd), F32), pltpu.VMEM((n_h, HG_CHUNK, hd), F32)],
        compiler_params=_params(("parallel", "parallel", "arbitrary")),
        name="hgrn_prompt",
    )(z, z, z, z, lb.reshape(1, -1), gain.reshape(1, hd))


HG_SAMPLE_KEYS = 32


def _hgrn_sample_kernel(q_ref, f_ref, i_ref, g_ref, lb_ref, gain_ref, s_ref, y_ref, snew_ref,
                        qt_sc, ft_sc, kt_sc, vt_sc, o_sc):
    kb = pl.program_id(1)

    @pl.when(kb == 0)
    def _():
        lb = lb_ref[...]
        f = lb + (1.0 - lb) * jax.nn.sigmoid(f_ref[...])
        qt_sc[...] = _silu(q_ref[...]).T
        ft_sc[...] = f.T
        kt_sc[...] = (1.0 - f).T
        vt_sc[...] = i_ref[...].T
        o_sc[...] = jnp.zeros_like(o_sc)

    vt = vt_sc[...]
    acc = o_sc[...]
    for kl in range(HG_SAMPLE_KEYS):
        k = kb * HG_SAMPLE_KEYS + kl
        st = s_ref[:, 0, kl, :].T
        s_new = st * ft_sc[pl.ds(k, 1), :] + vt * kt_sc[pl.ds(k, 1), :]
        acc = acc + s_new * qt_sc[pl.ds(k, 1), :]
        snew_ref[:, 0, kl, :] = s_new.T
    o_sc[...] = acc

    @pl.when(kb == pl.num_programs(1) - 1)
    def _():
        o = acc.T
        o = o * lax.rsqrt(jnp.mean(o * o, axis=-1, keepdims=True) + RMS_EPS) * gain_ref[...]
        y_ref[...] = o * _silu(g_ref[...])


def _hgrn_sample(z, states, layer, lb, gain):
    _, b, n_heads, hd, _ = states.shape
    kbs = HG_SAMPLE_KEYS

    def col(block):
        return pl.BlockSpec((b, hd), lambda h, k: (0, block * n_heads + h))

    sblk = pl.BlockSpec((b, 1, kbs, hd), lambda h, k: (0, h, k, 0))
    return pl.pallas_call(
        _hgrn_sample_kernel,
        grid=(n_heads, hd // kbs),
        in_specs=[col(1), col(2), col(3), col(4),
                  pl.BlockSpec((1, hd), lambda h, k: (0, h)),
                  pl.BlockSpec((1, hd), lambda h, k: (0, 0)),
                  pl.BlockSpec((None, b, 1, kbs, hd), lambda h, k: (layer, 0, h, k, 0))],
        out_specs=[pl.BlockSpec((b, hd), lambda h, k: (0, h)), sblk],
        out_shape=[jax.ShapeDtypeStruct((b, n_heads * hd), F32),
                   jax.ShapeDtypeStruct(states.shape[1:], F32)],
        scratch_shapes=[pltpu.VMEM((hd, b), F32)] * 5,
        compiler_params=_params(("parallel", "arbitrary")),
        name="hgrn_sample",
    )(z, z, z, z, lb.reshape(1, -1), gain.reshape(1, hd), states)


def _mix_to_hidden(x, z, y_a_pre, y_b, mw, dense, layer, alpha, tm):
    y_a = _glu(y_a_pre, dense["w_glu"], layer, tm)
    merged = _merge(y_a, y_b, z, dense["w_ba"], dense["w_bb"], layer, tm)
    return _ln1(merged, x, dense["w_o"], mw["ln1_g"], mw["ln1_b"], layer, alpha, min(tm, 256))


def _layer_prompt(x, p, mw, lw, layer, alpha, bt, t):
    m = bt * t
    tm = min(512, m)
    dense = mw["dense_bf16"]
    z = _inproj(x, dense["w_in"], layer, min(INPROJ_ROWS, m))
    y_a_pre, f_re, f_im = _s5_prompt(z, lw["s5_prompt_ops"], bt, t)
    y_b, hg_new = _hgrn_prompt(z, lw["lb"], lw["gn_gain"], bt, t)
    h = _mix_to_hidden(x, z, y_a_pre, y_b, mw, dense, layer, alpha, tm)
    s1, s2, cw, counts = _router_sorted(h, lw["w_router"], lw["b_router"], tm)
    pos8, tab = _plan(s1, s2, counts, tm)
    pos = pos8[:2].reshape(2 * m)
    n_tiles = _moe_num_tiles(m)
    tile_expert, tile_rows, n_used = tab[0, :n_tiles], tab[1, :n_tiles], tab[2, :1]
    xs = _dispatch(h, pos, tile_rows, tm)
    ys = _experts_sorted(xs, tile_expert, tile_rows, n_used, mw["w_g"], mw["w_u"], mw["w_d"], layer)
    x_new = _final_gather(h, cw, p, pos, ys, dense["w_pg"], dense["w_pp"], mw["ln2_g"], mw["ln2_b"], layer,
                          alpha, min(tm, 256))
    return x_new, f_re, f_im, hg_new


def _layer_sample(x, p, s_re, s_im, hg_states, mw, lw, layer, alpha):
    m = x.shape[0]
    dense = mw["dense_f32"]
    z = _inproj(x, dense["w_in"], layer, m)
    y_a_pre, n_re, n_im = _s5_sample(z, s_re, s_im, lw["s5_sample_ops"])
    y_b, hg_new = _hgrn_sample(z, hg_states, layer, lw["lb"], lw["gn_gain"])
    h = _mix_to_hidden(x, z, y_a_pre, y_b, mw, dense, layer, alpha, m)
    comb = _router_dense(h, lw["w_router"], lw["b_router"], m)
    moe = _moe_dense(h, comb, mw["w_g"], mw["w_u"], mw["w_d"], layer, m)
    x_new = _final(h, moe, p, dense["w_pg"], dense["w_pp"], mw["ln2_g"], mw["ln2_b"], layer, alpha, m)
    return x_new, n_re, n_im, hg_new


def _layer_operands(i, lb_all, s5_lambda_re, s5_lambda_im, s5_log_dt, s5_b_re, s5_b_im, s5_c_re, s5_c_im, s5_d,
                    hg_norm_gain, w_group_router, b_group_router, w_expert_router, b_expert_router):
    disc = _s5_discretise(s5_lambda_re[i], s5_lambda_im[i], s5_log_dt[i], s5_b_re[i], s5_b_im[i])
    d_model = w_group_router.shape[1]
    pad = LANES - N_EXPERTS - N_GROUPS
    w_router = jnp.concatenate([w_expert_router[i], w_group_router[i], jnp.zeros((d_model, pad), F32)], axis=1)
    b_router = jnp.concatenate([b_expert_router[i], b_group_router[i], jnp.zeros((pad,), F32)]).reshape(1, LANES)
    return {
        "s5_width": s5_d.shape[1],
        "s5_prompt_ops": _s5_prompt_operators(*disc, s5_c_re[i], s5_c_im[i], s5_d[i]),
        "s5_sample_ops": _s5_sample_operators(*disc, s5_c_re[i], s5_c_im[i], s5_d[i]),
        "lb": lb_all[i],
        "gn_gain": hg_norm_gain[i],
        "w_router": w_router, "b_router": b_router,
    }


def kernel(x_prompt, x_sample, p_prompt, p_sample, state_s5_re, state_s5_im, state_hgrn, w_in, s5_lambda_re, s5_lambda_im, s5_log_dt, s5_b_re, s5_b_im, s5_c_re, s5_c_im, s5_d, s5_w_glu, hg_lower_bounds, hg_norm_gain, w_branch_a, w_branch_b, w_out, ln1_gain, ln1_bias, w_group_router, b_group_router, w_expert_router, b_expert_router, w_exp_gate, w_exp_up, w_exp_down, w_ple_proj, w_ple_gate, ln2_gain, ln2_bias):
    depth = w_in.shape[0]
    bt, t, d_model = x_prompt.shape
    bs = x_sample.shape[0]
    alpha = (2 * depth) ** 0.25
    n_groups, n_state = s5_lambda_re.shape[1:]

    lb_soft = jax.nn.softmax(hg_lower_bounds.astype(F32), axis=0)
    lb_all = jnp.cumsum(lb_soft, axis=0) - lb_soft[0]

    dense_f32 = {"w_in": w_in, "w_glu": s5_w_glu, "w_ba": w_branch_a, "w_bb": w_branch_b,
                 "w_o": w_out, "w_pp": w_ple_proj, "w_pg": w_ple_gate}
    mw = {"dense_f32": dense_f32,
          "dense_bf16": {k: v.astype(BF16) for k, v in dense_f32.items()},
          "w_g": w_exp_gate, "w_u": w_exp_up, "w_d": w_exp_down,
          "ln1_g": ln1_gain, "ln1_b": ln1_bias, "ln2_g": ln2_gain, "ln2_b": ln2_bias}
    pp = p_prompt.reshape(depth, bt * t, -1)
    ps = p_sample.reshape(depth, bs, -1)

    xp = x_prompt.reshape(bt * t, d_model)
    xs = x_sample.reshape(bs, d_model)
    outs = [[] for _ in range(6)]
    for i in range(depth):
        lw = _layer_operands(i, lb_all, s5_lambda_re, s5_lambda_im, s5_log_dt, s5_b_re, s5_b_im, s5_c_re,
                             s5_c_im, s5_d, hg_norm_gain, w_group_router, b_group_router, w_expert_router,
                             b_expert_router)
        xp, a_re, a_im, a_hg = _layer_prompt(xp, pp, mw, lw, i, alpha, bt, t)
        xs, b_re, b_im, b_hg = _layer_sample(
            xs, ps, state_s5_re[i].reshape(bs, n_groups * n_state), state_s5_im[i].reshape(bs, n_groups * n_state),
            state_hgrn, mw, lw, i, alpha)
        for lst, val in zip(outs, (a_re.reshape(bt, n_groups, n_state), a_im.reshape(bt, n_groups, n_state), a_hg,
                                   b_re.reshape(bs, n_groups, n_state), b_im.reshape(bs, n_groups, n_state), b_hg)):
            lst.append(val)

    return (xp.reshape(bt, t, d_model), xs.reshape(bs, 1, d_model), *[jnp.stack(o) for o in outs])
```

```python
import functools
import math

import jax
import jax.numpy as jnp
from jax import lax
from jax.experimental import pallas as pl
from jax.experimental.pallas import tpu as pltpu

F32 = jnp.float32
BF16 = jnp.bfloat16
HIGHEST = lax.Precision.HIGHEST

LANES = 128
SUBLANES = 8
VMEM_LIMIT_BYTES = 56 * 1024 * 1024

S5_GROUP = 16
S5_STATE = 64
S5_CHUNK = 16
HG_HEAD_DIM = 128
HG_CHUNK = 64
HG_SUB = 16
N_GROUPS = 4
EXPERTS_PER_GROUP = 8
N_EXPERTS = N_GROUPS * EXPERTS_PER_GROUP
LN_EPS = 1e-5
RMS_EPS = 1e-6


def _params(semantics):
    return pltpu.CompilerParams(dimension_semantics=semantics, vmem_limit_bytes=VMEM_LIMIT_BYTES)


def _dot(a, b):
    return jnp.dot(a, b, preferred_element_type=F32)


def _mm(a, w):
    if w.dtype == F32:
        a_hi = a.astype(BF16)
        a_lo = (a - a_hi.astype(F32)).astype(BF16)
        w_hi = w.astype(BF16)
        w_lo = (w - w_hi.astype(F32)).astype(BF16)
        return _dot(a_hi, w_hi) + (_dot(a_hi, w_lo) + _dot(a_lo, w_hi))
    return jnp.dot(a.astype(BF16), w, preferred_element_type=F32)


def _dot_nt(a, b):
    return lax.dot_general(a, b, (((1,), (1,)), ((), ())), preferred_element_type=F32)


def _dot_tn(a, b):
    return lax.dot_general(a, b, (((0,), (0,)), ((), ())), preferred_element_type=F32)


def _sigmoid(x):
    return 0.5 * jnp.tanh(0.5 * x) + 0.5


def _silu(x):
    return x * _sigmoid(x)


def _layer_norm_rows(x, gain, bias):
    mu = jnp.mean(x, axis=-1, keepdims=True)
    xc = x - mu
    var = jnp.mean(xc * xc, axis=-1, keepdims=True)
    return xc * lax.rsqrt(var + LN_EPS) * gain + bias


INPROJ_ROWS = 1024


def _inproj_kernel(x_ref, w_ref, o_ref, xb_ref):
    if w_ref.dtype == F32:
        o_ref[...] = _mm(x_ref[...], w_ref[...])
        return

    @pl.when(pl.program_id(1) == 0)
    def _():
        xb_ref[...] = x_ref[...].astype(BF16)

    o_ref[...] = _dot(xb_ref[...], w_ref[...])


def _layer_spec(block, layer, index_map):
    return pl.BlockSpec((None,) + block, lambda *idx: (layer,) + index_map(*idx))


def _inproj(x, w, layer, tm, tn=1024):
    m, k = x.shape
    n = w.shape[-1]
    return pl.pallas_call(
        _inproj_kernel,
        grid=(m // tm, n // tn),
        in_specs=[pl.BlockSpec((tm, k), lambda i, j: (i, 0)),
                  _layer_spec((k, tn), layer, lambda i, j: (0, j))],
        out_specs=pl.BlockSpec((tm, tn), lambda i, j: (i, j)),
        out_shape=jax.ShapeDtypeStruct((m, n), F32),
        scratch_shapes=[pltpu.VMEM((tm, k), BF16)],
        compiler_params=_params(("parallel", "arbitrary")),
        name="inproj",
    )(x, w)


def _glu_kernel(a_ref, w_ref, o_ref):
    a = a_ref[...]
    o_ref[...] = a * _sigmoid(_mm(a, w_ref[...]))


def _glu(a, w, layer, tm):
    m, k = a.shape
    return pl.pallas_call(
        _glu_kernel,
        grid=(m // tm,),
        in_specs=[pl.BlockSpec((tm, k), lambda i: (i, 0)),
                  _layer_spec((k, k), layer, lambda i: (0, 0))],
        out_specs=pl.BlockSpec((tm, k), lambda i: (i, 0)),
        out_shape=jax.ShapeDtypeStruct((m, k), F32),
        compiler_params=_params(("parallel",)),
        name="s5_glu",
    )(a, w)


def _merge_kernel(ya_ref, yb_ref, ga_ref, gb_ref, wa_ref, wb_ref, o_ref):
    pa = _mm(ya_ref[...], wa_ref[...])
    pb = _mm(yb_ref[...], wb_ref[...])
    o_ref[...] = _sigmoid(ga_ref[...]) * pa + _sigmoid(gb_ref[...]) * pb


MERGE_ROWS = 1024
MERGE_COLS = 512


def _merge(ya, yb, z, wa, wb, layer, tm, tn=MERGE_COLS):
    m, k = ya.shape
    n = wa.shape[-1]
    nj = n // tn
    gate_a = 5 * k // tn
    gate_b = gate_a + nj
    return pl.pallas_call(
        _merge_kernel,
        grid=(m // tm, nj),
        in_specs=[pl.BlockSpec((tm, k), lambda i, j: (i, 0)),
                  pl.BlockSpec((tm, k), lambda i, j: (i, 0)),
                  pl.BlockSpec((tm, tn), lambda i, j: (i, gate_a + j)),
                  pl.BlockSpec((tm, tn), lambda i, j: (i, gate_b + j)),
                  _layer_spec((k, tn), layer, lambda i, j: (0, j)),
                  _layer_spec((k, tn), layer, lambda i, j: (0, j))],
        out_specs=pl.BlockSpec((tm, tn), lambda i, j: (i, j)),
        out_shape=jax.ShapeDtypeStruct((m, n), F32),
        compiler_params=_params(("parallel", "arbitrary")),
        name="branch_merge",
    )(ya, yb, z, z, wa, wb)


def _ln1_kernel(mg_ref, x_ref, w_ref, g_ref, b_ref, o_ref, *, alpha):
    r = alpha * x_ref[...] + _mm(mg_ref[...], w_ref[...])
    o_ref[...] = _layer_norm_rows(r, g_ref[...], b_ref[...])


def _ln1(merged, x, w, gain, bias, layer, alpha, tm):
    m, d = x.shape
    row = pl.BlockSpec((tm, d), lambda i: (i, 0))
    vec = _layer_spec((1, d), layer, lambda i: (0, 0))
    return pl.pallas_call(
        functools.partial(_ln1_kernel, alpha=alpha),
        grid=(m // tm,),
        in_specs=[row, row, _layer_spec((d, d), layer, lambda i: (0, 0)), vec, vec],
        out_specs=row,
        out_shape=jax.ShapeDtypeStruct((m, d), F32),
        compiler_params=_params(("parallel",)),
        name="outproj_ln1",
    )(merged, x, w, gain[:, None, :], bias[:, None, :])


def _final_kernel(h_ref, moe_ref, p_ref, wg_ref, wp_ref, g_ref, b_ref, o_ref, *, alpha):
    h = h_ref[...]
    gate = _sigmoid(_mm(h, wg_ref[...]))
    proj = _mm(p_ref[...], wp_ref[...])
    r = alpha * h + moe_ref[...] + gate * proj
    o_ref[...] = _layer_norm_rows(r, g_ref[...], b_ref[...])


def _final(h, moe, p, wg, wp, gain, bias, layer, alpha, tm):
    m, d = h.shape
    dp = p.shape[-1]
    row = pl.BlockSpec((tm, d), lambda i: (i, 0))
    vec = _layer_spec((1, d), layer, lambda i: (0, 0))
    return pl.pallas_call(
        functools.partial(_final_kernel, alpha=alpha),
        grid=(m // tm,),
        in_specs=[row, row, _layer_spec((tm, dp), layer, lambda i: (i, 0)),
                  _layer_spec((d, d), layer, lambda i: (0, 0)),
                  _layer_spec((dp, d), layer, lambda i: (0, 0)), vec, vec],
        out_specs=row,
        out_shape=jax.ShapeDtypeStruct((m, d), F32),
        compiler_params=_params(("parallel",)),
        name="ple_ln2",
    )(h, moe, p, wg, wp, gain[:, None, :], bias[:, None, :])


def _route(h, w, b):
    logits = _mm(h, w) + b
    lane = lax.broadcasted_iota(jnp.int32, logits.shape, 1)
    lane_f = lane.astype(F32)
    neg = -jnp.inf
    gl = jnp.where(lane >= N_EXPERTS, jnp.where(lane < N_EXPERTS + N_GROUPS, logits, neg), neg)
    gmax = jnp.max(gl, axis=-1, keepdims=True)
    denom = jnp.sum(jnp.exp(gl - gmax), axis=-1, keepdims=True)
    grp_p = 1.0 / denom
    gidx = jnp.min(jnp.where(gl == gmax, lane_f, 1e9), axis=-1, keepdims=True) - N_EXPERTS
    lane_grp = (lane // EXPERTS_PER_GROUP).astype(F32)
    el = jnp.where(lane < N_EXPERTS, jnp.where(lane_grp == gidx, logits, neg), neg)
    t1 = jnp.max(el, axis=-1, keepdims=True)
    i1 = jnp.min(jnp.where(el == t1, lane_f, 1e9), axis=-1, keepdims=True)
    el2 = jnp.where(lane_f == i1, neg, el)
    t2 = jnp.max(el2, axis=-1, keepdims=True)
    i2 = jnp.min(jnp.where(el2 == t2, lane_f, 1e9), axis=-1, keepdims=True)
    e2 = jnp.exp(t2 - t1)
    w1 = 1.0 / (1.0 + e2)
    w2 = e2 * w1
    return lane, lane_f, i1, i2, grp_p * w1, grp_p * w2


def _router_dense_kernel(h_ref, w_ref, b_ref, o_ref):
    _, lane_f, i1, i2, c1, c2 = _route(h_ref[...], w_ref[...], b_ref[...])
    o_ref[...] = jnp.where(lane_f == i1, c1, jnp.where(lane_f == i2, c2, 0.0))


def _router_dense(h, w_r, b_r, tm):
    m, d = h.shape
    return pl.pallas_call(
        _router_dense_kernel,
        grid=(m // tm,),
        in_specs=[pl.BlockSpec((tm, d), lambda i: (i, 0)),
                  pl.BlockSpec((d, LANES), lambda i: (0, 0)),
                  pl.BlockSpec((1, LANES), lambda i: (0, 0))],
        out_specs=pl.BlockSpec((tm, LANES), lambda i: (i, 0)),
        out_shape=jax.ShapeDtypeStruct((m, LANES), F32),
        compiler_params=_params(("parallel",)),
        name="router_dense",
    )(h, w_r, b_r)


def _moe_dense_kernel(h_ref, c_ref, wg_ref, wu_ref, wd_ref, o_ref):
    e = pl.program_id(1)

    @pl.when(e == 0)
    def _():
        o_ref[...] = jnp.zeros_like(o_ref)

    comb = c_ref[...]
    lane = lax.broadcasted_iota(jnp.int32, comb.shape, 1)
    col = jnp.sum(jnp.where(lane == e, comb, 0.0), axis=-1, keepdims=True)
    h = h_ref[...]
    hid = _silu(_mm(h, wg_ref[0])) * _mm(h, wu_ref[0]) * col
    o_ref[...] += _mm(hid, wd_ref[0])


def _moe_dense(h, comb, wg, wu, wd, layer, tm):
    m, d = h.shape
    _, n_e, _, f = wg.shape
    return pl.pallas_call(
        _moe_dense_kernel,
        grid=(m // tm, n_e),
        in_specs=[pl.BlockSpec((tm, d), lambda i, e: (i, 0)),
                  pl.BlockSpec((tm, LANES), lambda i, e: (i, 0)),
                  _layer_spec((1, d, f), layer, lambda i, e: (e, 0, 0)),
                  _layer_spec((1, d, f), layer, lambda i, e: (e, 0, 0)),
                  _layer_spec((1, f, d), layer, lambda i, e: (e, 0, 0))],
        out_specs=pl.BlockSpec((tm, d), lambda i, e: (i, 0)),
        out_shape=jax.ShapeDtypeStruct((m, d), F32),
        compiler_params=_params(("parallel", "arbitrary")),
        name="moe_dense",
    )(h, comb, wg, wu, wd)


MOE_TILE = 256


def _moe_num_tiles(m):
    return 2 * m // MOE_TILE + N_EXPERTS


def _router_sorted_kernel(h_ref, w_ref, b_ref, s1_ref, s2_ref, cw_ref, cnt_ref, carry_sc):
    @pl.when(pl.program_id(0) == 0)
    def _():
        carry_sc[...] = jnp.zeros_like(carry_sc)

    lane, lane_f, i1, i2, c1, c2 = _route(h_ref[...], w_ref[...], b_ref[...])
    sel1 = jnp.where(lane_f == i1, 1.0, 0.0)
    sel2 = jnp.where(lane_f == i2, 1.0, 0.0)
    cnt = sel1 + sel2
    tm = cnt.shape[0]
    strict_lower = (lax.broadcasted_iota(jnp.int32, (tm, tm), 0) > lax.broadcasted_iota(jnp.int32, (tm, tm), 1))
    prefix = _dot(strict_lower.astype(BF16), cnt.astype(BF16)) + carry_sc[0:1, :]
    s1_ref[...] = sel1 * (prefix + 1.0)
    s2_ref[...] = sel2 * (prefix + 1.0)
    cw_ref[...] = jnp.where(lane == 0, c1, jnp.where(lane == 1, c2, 0.0))
    carry_sc[...] = carry_sc[...] + jnp.sum(cnt, axis=0, keepdims=True)
    cnt_ref[...] = carry_sc[...]


def _router_sorted(h, w_r, b_r, tm):
    m, d = h.shape
    tok = pl.BlockSpec((tm, LANES), lambda i: (i, 0))
    return pl.pallas_call(
        _router_sorted_kernel,
        grid=(m // tm,),
        in_specs=[pl.BlockSpec((tm, d), lambda i: (i, 0)),
                  pl.BlockSpec((d, LANES), lambda i: (0, 0)),
                  pl.BlockSpec((1, LANES), lambda i: (0, 0))],
        out_specs=[tok, tok, tok, pl.BlockSpec((SUBLANES, LANES), lambda i: (0, 0))],
        out_shape=[jax.ShapeDtypeStruct((m, LANES), F32)] * 3 + [jax.ShapeDtypeStruct((SUBLANES, LANES), F32)],
        scratch_shapes=[pltpu.VMEM((SUBLANES, LANES), F32)],
        compiler_params=_params(("arbitrary",)),
        name="router_sorted",
    )(h, w_r, b_r)


def _plan_kernel(s1_ref, s2_ref, cnt_ref, pos_ref, tab_ref):
    te = float(MOE_TILE)
    lane = lax.broadcasted_iota(jnp.int32, (1, LANES), 1)
    cnt = jnp.where(lane < N_EXPERTS, cnt_ref[0:1, :], 0.0)
    padded = jnp.floor((cnt + (te - 1.0)) * (1.0 / te)) * te
    r128 = lax.broadcasted_iota(jnp.int32, (LANES, LANES), 0)
    c128 = lax.broadcasted_iota(jnp.int32, (LANES, LANES), 1)
    before = jnp.where(r128 < c128, 1.0, 0.0)
    off = jnp.dot(jnp.broadcast_to(padded, (SUBLANES, LANES)), before, precision=HIGHEST,
                  preferred_element_type=F32)[0:1, :]
    s1 = s1_ref[...]
    s2 = s2_ref[...]
    v1 = jnp.where(s1 > 0.0, s1 - 1.0 + off, 0.0)
    v2 = jnp.where(s2 > 0.0, s2 - 1.0 + off, 0.0)
    ones8 = jnp.ones((SUBLANES, LANES), F32)
    p1 = lax.dot_general(ones8, v1, (((1,), (1,)), ((), ())), precision=HIGHEST, preferred_element_type=F32)
    p2 = lax.dot_general(ones8, v2, (((1,), (1,)), ((), ())), precision=HIGHEST, preferred_element_type=F32)
    row8 = lax.broadcasted_iota(jnp.int32, p1.shape, 0)
    pos_ref[...] = jnp.where(row8 == 0, p1, p2).astype(jnp.int32)

    def per_expert(row):
        return jnp.broadcast_to(row, (LANES, LANES)).T
    off_e = per_expert(off)
    end_e = per_expert(off + padded)
    cnt_e = per_expert(cnt)
    start = c128.astype(F32) * te
    is_e = r128 < N_EXPERTS
    tile_e = jnp.sum(jnp.where(is_e, jnp.where(end_e <= start, 1.0, 0.0), 0.0), axis=0, keepdims=True)
    rows = jnp.clip(cnt_e - (start - off_e), 0.0, te)
    owns = jnp.where(is_e, jnp.where(off_e <= start, jnp.where(start < end_e, rows, 0.0), 0.0), 0.0)
    n_rows = jnp.sum(owns, axis=0, keepdims=True)
    tile_e = jnp.minimum(tile_e, float(N_EXPERTS - 1))
    n_used = jnp.sum(jnp.where(n_rows > 0.0, 1.0, 0.0), axis=-1, keepdims=True)
    row_t = lax.broadcasted_iota(jnp.int32, (SUBLANES, LANES), 0)
    tab_ref[...] = jnp.where(row_t == 0, tile_e, jnp.where(row_t == 1, n_rows, n_used)).astype(jnp.int32)


def _plan(s1, s2, counts, tm):
    m = s1.shape[0]
    tok = pl.BlockSpec((tm, LANES), lambda i: (i, 0))
    return pl.pallas_call(
        _plan_kernel,
        grid=(m // tm,),
        in_specs=[tok, tok, pl.BlockSpec((SUBLANES, LANES), lambda i: (0, 0))],
        out_specs=[pl.BlockSpec((SUBLANES, tm), lambda i: (0, i)),
                   pl.BlockSpec((SUBLANES, LANES), lambda i: (0, 0))],
        out_shape=[jax.ShapeDtypeStruct((SUBLANES, m), jnp.int32),
                   jax.ShapeDtypeStruct((SUBLANES, LANES), jnp.int32)],
        compiler_params=_params(("arbitrary",)),
        name="moe_plan",
    )(s1, s2, counts)


DMA_UNROLL = 8


def _dispatch_kernel(pos_ref, rows_ref, h_ref, xs_hbm, zero_buf, zero_sem, sem, *, m, n_tiles):
    i = pl.program_id(0)
    tm = h_ref.shape[0]

    @pl.when(i == 0)
    def _():
        zero_buf[...] = jnp.zeros_like(zero_buf)

        def zero_copy(j):
            return pltpu.make_async_copy(zero_buf, xs_hbm.at[pl.ds(j * MOE_TILE, MOE_TILE)], zero_sem)

        def zero_start(j, c):
            @pl.when(rows_ref[j] < MOE_TILE)
            def _():
                zero_copy(j).start()
            return c

        def zero_wait(j, c):
            @pl.when(rows_ref[j] < MOE_TILE)
            def _():
                zero_copy(j).wait()
            return c

        lax.fori_loop(0, n_tiles, zero_start, 0)
        lax.fori_loop(0, n_tiles, zero_wait, 0)

    def body(r, c):
        t = i * tm + r
        src = h_ref.at[pl.ds(r, 1)]
        pltpu.make_async_copy(src, xs_hbm.at[pl.ds(pos_ref[t], 1)], sem).start()
        pltpu.make_async_copy(src, xs_hbm.at[pl.ds(pos_ref[m + t], 1)], sem).start()
        return c
    lax.fori_loop(0, tm, body, 0, unroll=DMA_UNROLL)

    pltpu.make_async_copy(xs_hbm.at[pl.ds(0, 2 * tm)], xs_hbm.at[pl.ds(0, 2 * tm)], sem).wait()


def _dispatch(h, pos, tile_rows, tm):
    m, d = h.shape
    n_tiles = _moe_num_tiles(m)
    grid_spec = pltpu.PrefetchScalarGridSpec(
        num_scalar_prefetch=2,
        grid=(m // tm,),
        in_specs=[pl.BlockSpec((tm, d), lambda i, pos, rows: (i, 0))],
        out_specs=pl.BlockSpec(memory_space=pl.ANY),
        scratch_shapes=[pltpu.VMEM((MOE_TILE, d), F32), pltpu.SemaphoreType.DMA(()),
                        pltpu.SemaphoreType.DMA(())],
    )
    return pl.pallas_call(
        functools.partial(_dispatch_kernel, m=m, n_tiles=n_tiles),
        grid_spec=grid_spec,
        out_shape=jax.ShapeDtypeStruct((n_tiles * MOE_TILE, d), F32),
        compiler_params=_params(("arbitrary",)),
        name="moe_dispatch",
    )(pos, tile_rows, h)


def _experts_kernel(te_ref, nr_ref, nu_ref, x_ref, wg_ref, wu_ref, wd_ref, y_ref, wgb, wub, wdb):
    j = pl.program_id(0)
    new_expert = jnp.logical_or(j == 0, te_ref[j] != te_ref[jnp.maximum(j - 1, 0)])

    @pl.when(jnp.logical_and(new_expert, nr_ref[j] > 0))
    def _():
        wgb[...] = wg_ref[0].astype(BF16)
        wub[...] = wu_ref[0].astype(BF16)
        wdb[...] = wd_ref[0].astype(BF16)

    @pl.when(nr_ref[j] > 0)
    def _():
        x = x_ref[...].astype(BF16)
        hid = _silu(_dot(x, wgb[...])) * _dot(x, wub[...])
        y_ref[...] = _dot(hid.astype(BF16), wdb[...])

    @pl.when(nr_ref[j] == 0)
    def _():
        y_ref[...] = jnp.zeros_like(y_ref)


def _experts_sorted(xs, tile_expert, tile_rows, n_used, wg, wu, wd, layer):
    n_rows, d = xs.shape
    _, n_e, _, f = wg.shape
    n_tiles = n_rows // MOE_TILE

    def used_tile(j, te, nr, nu):
        return (jnp.minimum(j, nu[0] - 1), 0)

    def expert_block(j, te, nr, nu):
        return (layer, te[j], 0, 0)

    grid_spec = pltpu.PrefetchScalarGridSpec(
        num_scalar_prefetch=3,
        grid=(n_tiles,),
        in_specs=[pl.BlockSpec((MOE_TILE, d), used_tile),
                  pl.BlockSpec((None, 1, d, f), expert_block),
                  pl.BlockSpec((None, 1, d, f), expert_block),
                  pl.BlockSpec((None, 1, f, d), expert_block)],
        out_specs=pl.BlockSpec((MOE_TILE, d), lambda j, te, nr, nu: (j, 0)),
        scratch_shapes=[pltpu.VMEM((d, f), BF16), pltpu.VMEM((d, f), BF16), pltpu.VMEM((f, d), BF16)],
    )
    return pl.pallas_call(
        _experts_kernel,
        grid_spec=grid_spec,
        out_shape=jax.ShapeDtypeStruct((n_rows, d), F32),
        compiler_params=_params(("arbitrary",)),
        name="moe_experts_sorted",
    )(tile_expert, tile_rows, n_used, xs, wg, wu, wd)


def _final_gather_kernel(pos_ref, h_ref, cw_ref, p_ref, wg_ref, wp_ref, g_ref, b_ref, ys_hbm, o_ref,
                         ybuf, sem, *, alpha, m):
    i = pl.program_id(0)
    slot = i % 2
    tm = h_ref.shape[0]

    last = pl.num_programs(0) - 1

    def start_rows(tile, dst_slot, r):
        t = tile * tm + r
        for pick in range(2):
            pltpu.make_async_copy(ys_hbm.at[pl.ds(pos_ref[pick * m + t], 1)],
                                  ybuf.at[dst_slot, pl.ds(pick * tm + r, 1)], sem.at[dst_slot]).start()

    def wait_tile(s):
        pltpu.make_async_copy(ys_hbm.at[pl.ds(0, 2 * tm)], ybuf.at[s], sem.at[s]).wait()

    @pl.when(i == 0)
    def _():
        def body(r, c):
            start_rows(0, 0, r)
            return c
        lax.fori_loop(0, tm, body, 0, unroll=DMA_UNROLL)

    nxt = jnp.minimum(i + 1, last)
    for r in range(tm):
        start_rows(nxt, 1 - slot, r)

    h = h_ref[...]
    gate = _sigmoid(_mm(h, wg_ref[...]))
    proj = _mm(p_ref[...], wp_ref[...])

    wait_tile(slot)
    cw = cw_ref[...]
    moe = cw[:, 0:1] * ybuf[slot, 0:tm] + cw[:, 1:2] * ybuf[slot, tm:2 * tm]
    r = alpha * h + moe + gate * proj
    o_ref[...] = _layer_norm_rows(r, g_ref[...], b_ref[...])

    @pl.when(i == last)
    def _():
        wait_tile(1 - slot)


def _final_gather(h, cw, p, pos, ys, wg, wp, gain, bias, layer, alpha, tm):
    m, d = h.shape
    dp = p.shape[-1]
    row = pl.BlockSpec((tm, d), lambda i, pos: (i, 0))
    vec = _layer_spec((1, d), layer, lambda i, pos: (0, 0))
    grid_spec = pltpu.PrefetchScalarGridSpec(
        num_scalar_prefetch=1,
        grid=(m // tm,),
        in_specs=[row, pl.BlockSpec((tm, LANES), lambda i, pos: (i, 0)),
                  _layer_spec((tm, dp), layer, lambda i, pos: (i, 0)),
                  _layer_spec((d, d), layer, lambda i, pos: (0, 0)),
                  _layer_spec((dp, d), layer, lambda i, pos: (0, 0)), vec, vec,
                  pl.BlockSpec(memory_space=pl.ANY)],
        out_specs=row,
        scratch_shapes=[pltpu.VMEM((2, 2 * tm, d), F32), pltpu.SemaphoreType.DMA((2,))],
    )
    return pl.pallas_call(
        functools.partial(_final_gather_kernel, alpha=alpha, m=m),
        grid_spec=grid_spec,
        out_shape=jax.ShapeDtypeStruct((m, d), F32),
        compiler_params=_params(("arbitrary",)),
        name="ple_ln2_gather",
    )(pos, h, cw, p, wg, wp, gain[:, None, :], bias[:, None, :], ys)


def _s5_discretise(lam_re, lam_im, log_dt, b_re, b_im):
    dt = jnp.exp(log_dt)[:, None]
    mag = jnp.exp(lam_re * dt)
    ab_re = mag * jnp.cos(lam_im * dt)
    ab_im = mag * jnp.sin(lam_im * dt)
    den = lam_re * lam_re + lam_im * lam_im
    nr = ab_re - 1.0
    zr = (nr * lam_re + ab_im * lam_im) / den
    zi = (ab_im * lam_re - nr * lam_im) / den
    bb_re = zr[..., None] * b_re - zi[..., None] * b_im
    bb_im = zr[..., None] * b_im + zi[..., None] * b_re
    return ab_re, ab_im, bb_re, bb_im


def _s5_prompt_operators(ab_re, ab_im, bb_re, bb_im, c_re, c_im, d_skip):
    g, n = ab_re.shape
    cch = bb_re.shape[-1]
    L = S5_CHUNK
    pw_re = [jnp.ones_like(ab_re)]
    pw_im = [jnp.zeros_like(ab_im)]
    for _ in range(L):
        pr, pi = pw_re[-1], pw_im[-1]
        pw_re.append(pr * ab_re - pi * ab_im)
        pw_im.append(pr * ab_im + pi * ab_re)
    a_re = jnp.stack(pw_re)
    a_im = jnp.stack(pw_im)
    w_re = a_re[:L, :, :, None] * bb_re - a_im[:L, :, :, None] * bb_im
    w_im = a_re[:L, :, :, None] * bb_im + a_im[:L, :, :, None] * bb_re
    kern = (jnp.einsum('gcn,kgnd->kgcd', c_re, w_re, precision=HIGHEST)
            - jnp.einsum('gcn,kgnd->kgcd', c_im, w_im, precision=HIGHEST))
    steps = jnp.arange(L)
    lag_is = (steps[None, None, :] - steps[None, :, None] == steps[:, None, None]).astype(F32)
    toe = jnp.einsum('kio,kgcd->gidoc', lag_is, kern, precision=HIGHEST)
    toe = toe.reshape(g, L * cch, L * cch)

    p_re = w_re[::-1].transpose(1, 0, 3, 2).reshape(g, L * cch, n)
    p_im = w_im[::-1].transpose(1, 0, 3, 2).reshape(g, L * cch, n)
    q_re = (jnp.einsum('gcn,tgn->gntc', c_re, a_re[1:]) - jnp.einsum('gcn,tgn->gntc', c_im, a_im[1:]))
    q_im = -(jnp.einsum('gcn,tgn->gntc', c_re, a_im[1:]) + jnp.einsum('gcn,tgn->gntc', c_im, a_re[1:]))
    q_re = q_re.reshape(g, n, L * cch)
    q_im = q_im.reshape(g, n, L * cch)

    hp = g // 2
    w = L * cch
    pe_re, po_re, pe_im, po_im = (x.astype(BF16) for x in (p_re[0::2], p_re[1::2], p_im[0::2], p_im[1::2]))
    qe_re, qo_re, qe_im, qo_im = (x.astype(BF16) for x in (q_re[0::2], q_re[1::2], q_im[0::2], q_im[1::2]))
    zp = jnp.zeros_like(pe_re)
    zq = jnp.zeros_like(qe_re)
    p2 = jnp.concatenate([jnp.concatenate([pe_re, zp, pe_im, zp], axis=2),
                          jnp.concatenate([zp, po_re, zp, po_im], axis=2)], axis=1)
    q2 = jnp.concatenate([jnp.concatenate([qe_re, zq], axis=2), jnp.concatenate([zq, qo_re], axis=2),
                          jnp.concatenate([qe_im, zq], axis=2), jnp.concatenate([zq, qo_im], axis=2)], axis=1)
    a_chunk = jnp.stack([a_re[L].reshape(hp, 2 * n), a_im[L].reshape(hp, 2 * n)], axis=1)
    d2 = jnp.broadcast_to(d_skip.reshape(hp, 2, 1, cch), (hp, 2, L, cch)).reshape(hp, 1, 2 * w)
    return toe.astype(BF16), p2, q2, a_chunk, d2


def _s5_prompt_kernel(u_ref, t_ref, p_ref, q_ref, a_ref, d_ref, y_ref, fre_ref, fim_ref,
                      s_sc, xin_sc, *, n_chunks, bt):
    half = a_ref.shape[-1]
    u0 = jnp.concatenate([u_ref[0, 0], u_ref[0, 1]], axis=1)
    u1 = jnp.concatenate([u_ref[1, 0], u_ref[1, 1]], axis=1)
    ub0 = u0.astype(BF16)
    ub1 = u1.astype(BF16)
    y_intra = jnp.concatenate([_dot(ub0, t_ref[0]), _dot(ub1, t_ref[1])], axis=1)
    s_sc[...] = _dot(jnp.concatenate([ub0, ub1], axis=1), p_ref[0])
    ar = a_ref[0, 0:1, :]
    ai = a_ref[0, 1:2, :]
    xr = jnp.zeros((bt, half), F32)
    xi = jnp.zeros((bt, half), F32)
    for j in range(n_chunks):
        rows = slice(j * bt, (j + 1) * bt)
        xin_sc[rows, 0:half] = xr
        xin_sc[rows, half:2 * half] = xi
        sr = s_sc[rows, 0:half]
        si = s_sc[rows, half:2 * half]
        xr, xi = ar * xr - ai * xi + sr, ar * xi + ai * xr + si
    fre_ref[...] = xr
    fim_ref[...] = xi
    y_carry = _dot(xin_sc[...].astype(BF16), q_ref[0])
    y = y_intra + y_carry + d_ref[0] * jnp.concatenate([u0, u1], axis=1)
    y = jax.nn.gelu(y)
    for k in range(4):
        y_ref[k // 2, k % 2] = y[:, k * LANES:(k + 1) * LANES]


S5_RELAYOUT_CHUNKS = 32
S5_RELAYOUT_GROUPS = LANES // S5_GROUP


def _to_chunks_kernel(x_ref, o_ref):
    L, cch = S5_CHUNK, S5_GROUP
    bt = x_ref.shape[0]
    nj = x_ref.shape[1] // L
    per_tile = LANES // cch
    for b in range(bt):
        at_pos = [x_ref[b, pl.ds(p, nj, stride=L), :] for p in range(L)]
        for g in range(S5_RELAYOUT_GROUPS):
            for h in range(L // per_tile):
                piece = jnp.concatenate(
                    [at_pos[h * per_tile + p][:, g * cch:(g + 1) * cch] for p in range(per_tile)], axis=1)
                o_ref[g, h, pl.ds(b, nj, stride=bt), :] = piece


def _from_chunks_kernel(y_ref, o_ref):
    L, cch = S5_CHUNK, S5_GROUP
    bt = o_ref.shape[0]
    nj = o_ref.shape[1] // L
    per_tile = LANES // cch
    for b in range(bt):
        tiles = [[y_ref[g, h, pl.ds(b, nj, stride=bt), :] for h in range(L // per_tile)]
                 for g in range(S5_RELAYOUT_GROUPS)]
        for p in range(L):
            h, q = divmod(p, per_tile)
            o_ref[b, pl.ds(p, nj, stride=L), :] = jnp.concatenate(
                [tiles[g][h][:, q * cch:(q + 1) * cch] for g in range(S5_RELAYOUT_GROUPS)], axis=1)


def _to_chunks(z, g, bt, t):
    L, cch, gt = S5_CHUNK, S5_GROUP, S5_RELAYOUT_GROUPS
    nj = min(S5_RELAYOUT_CHUNKS, t // L)
    n_chunks = t // L
    halves = L * cch // LANES
    return pl.pallas_call(
        _to_chunks_kernel,
        grid=(n_chunks // nj, g // gt),
        in_specs=[pl.BlockSpec((bt, nj * L, LANES), lambda j, lt: (0, j, lt))],
        out_specs=pl.BlockSpec((gt, halves, nj * bt, LANES), lambda j, lt: (lt, 0, j, 0)),
        out_shape=jax.ShapeDtypeStruct((g, halves, n_chunks * bt, LANES), F32),
        compiler_params=_params(("parallel", "parallel")),
        name="s5_to_chunks",
    )(z.reshape(bt, t, -1))


def _from_chunks(y_t, bt, t):
    L, cch, gt = S5_CHUNK, S5_GROUP, S5_RELAYOUT_GROUPS
    nj = min(S5_RELAYOUT_CHUNKS, t // L)
    g, halves = y_t.shape[:2]
    n_chunks = t // L
    return pl.pallas_call(
        _from_chunks_kernel,
        grid=(n_chunks // nj, g // gt),
        in_specs=[pl.BlockSpec((gt, halves, nj * bt, LANES), lambda j, lt: (lt, 0, j, 0))],
        out_specs=pl.BlockSpec((bt, nj * L, LANES), lambda j, lt: (0, j, lt)),
        out_shape=jax.ShapeDtypeStruct((bt, t, g * cch), F32),
        compiler_params=_params(("parallel", "parallel")),
        name="s5_from_chunks",
    )(y_t).reshape(bt * t, g * cch)


def _s5_prompt(z, ops, bt, t):
    toe, p2, q2, a_chunk, d2 = ops
    g = toe.shape[0]
    hp = g // 2
    L = S5_CHUNK
    cch = S5_GROUP
    n_chunks = t // L
    r = n_chunks * bt
    w = L * cch
    half = a_chunk.shape[-1]
    u_t = _to_chunks(z, g, bt, t)
    y_t, f_re, f_im = pl.pallas_call(
        functools.partial(_s5_prompt_kernel, n_chunks=n_chunks, bt=bt),
        grid=(hp,),
        in_specs=[pl.BlockSpec((2, w // LANES, r, LANES), lambda i: (i, 0, 0, 0)),
                  pl.BlockSpec((2, w, w), lambda i: (i, 0, 0)),
                  pl.BlockSpec((1, 2 * w, 2 * half), lambda i: (i, 0, 0)),
                  pl.BlockSpec((1, 2 * half, 2 * w), lambda i: (i, 0, 0)),
                  pl.BlockSpec((1, 2, half), lambda i: (i, 0, 0)),
                  pl.BlockSpec((1, 1, 2 * w), lambda i: (i, 0, 0))],
        out_specs=[pl.BlockSpec((2, w // LANES, r, LANES), lambda i: (i, 0, 0, 0)),
                   pl.BlockSpec((bt, half), lambda i: (0, i)),
                   pl.BlockSpec((bt, half), lambda i: (0, i))],
        out_shape=[jax.ShapeDtypeStruct((g, w // LANES, r, LANES), F32),
                   jax.ShapeDtypeStruct((bt, hp * half), F32),
                   jax.ShapeDtypeStruct((bt, hp * half), F32)],
        scratch_shapes=[pltpu.VMEM((r, 2 * half), F32), pltpu.VMEM((r, 2 * half), F32)],
        compiler_params=_params(("parallel",)),
        name="s5_prompt",
    )(u_t, toe, p2, q2, a_chunk, d2)
    return _from_chunks(y_t, bt, t), f_re, f_im


S5_SAMPLE_GROUPS = 8


def _s5_sample_operators(ab_re, ab_im, bb_re, bb_im, c_re, c_im, d_skip):
    g, n = ab_re.shape
    cch = bb_re.shape[-1]
    gb = S5_SAMPLE_GROUPS
    nb = g // gb
    eye = jnp.eye(gb, dtype=F32)
    b_ri = jnp.stack([bb_re, bb_im], axis=1).reshape(nb, gb, 2, n, cch)
    c_ri = jnp.stack([c_re, -c_im], axis=1).reshape(nb, gb, 2, cch, n)
    b8 = (b_ri.transpose(0, 1, 4, 2, 3)[:, :, :, :, None, :]
          * eye[None, :, None, None, :, None]).reshape(nb, gb * cch, 2 * gb * n)
    c8 = (c_ri.transpose(0, 2, 1, 4, 3)[:, :, :, :, None, :]
          * eye[None, None, :, None, :, None]).reshape(nb, 2 * gb * n, gb * cch)
    a8 = jnp.stack([ab_re.reshape(nb, gb * n), ab_im.reshape(nb, gb * n)], axis=1)
    d8 = d_skip.reshape(nb, 1, gb * cch)
    return b8, c8, a8, d8


def _s5_sample_kernel(u_ref, sr_ref, si_ref, b_ref, c_ref, a_ref, d_ref, y_ref, nr_ref, ni_ref):
    u = u_ref[...]
    half = sr_ref.shape[-1]
    bu = jnp.dot(u, b_ref[0], precision=HIGHEST, preferred_element_type=F32)
    ar = a_ref[0, 0:1, :]
    ai = a_ref[0, 1:2, :]
    sr = sr_ref[...]
    si = si_ref[...]
    xr = ar * sr - ai * si + bu[:, :half]
    xi = ar * si + ai * sr + bu[:, half:]
    nr_ref[...] = xr
    ni_ref[...] = xi
    y = jnp.dot(jnp.concatenate([xr, xi], axis=1), c_ref[0], precision=HIGHEST,
                preferred_element_type=F32) + d_ref[0] * u
    y_ref[...] = jax.nn.gelu(y)


def _s5_sample(z, s_re, s_im, ops):
    b8, c8, a8, d8 = ops
    nb = b8.shape[0]
    b = z.shape[0]
    wu = b8.shape[1]
    ws = a8.shape[-1]
    return pl.pallas_call(
        _s5_sample_kernel,
        grid=(nb,),
        in_specs=[pl.BlockSpec((b, wu), lambda i: (0, i)),
                  pl.BlockSpec((b, ws), lambda i: (0, i)),
                  pl.BlockSpec((b, ws), lambda i: (0, i)),
                  pl.BlockSpec((1, wu, 2 * ws), lambda i: (i, 0, 0)),
                  pl.BlockSpec((1, 2 * ws, wu), lambda i: (i, 0, 0)),
                  pl.BlockSpec((1, 2, ws), lambda i: (i, 0, 0)),
                  pl.BlockSpec((1, 1, wu), lambda i: (i, 0, 0))],
        out_specs=[pl.BlockSpec((b, wu), lambda i: (0, i)),
                   pl.BlockSpec((b, ws), lambda i: (0, i)),
                   pl.BlockSpec((b, ws), lambda i: (0, i))],
        out_shape=[jax.ShapeDtypeStruct((b, nb * wu), F32),
                   jax.ShapeDtypeStruct((b, nb * ws), F32),
                   jax.ShapeDtypeStruct((b, nb * ws), F32)],
        compiler_params=_params(("parallel",)),
        name="s5_sample",
    )(z, s_re, s_im, b8, c8, a8, d8)


HG_HEADS_PER_STEP = 8
HG_SAFE_SPAN = 60.0


def _hgrn_intra_factored(q, kk, v, g):
    c, sub = HG_CHUNK, HG_SUB
    n_sub = c // sub
    row = lax.broadcasted_iota(jnp.int32, (c, 1), 0)
    starts = [g[i * sub - 1:i * sub, :] if i else jnp.zeros_like(g[0:1, :]) for i in range(n_sub)]
    g_start_rows = jnp.concatenate([jnp.broadcast_to(s, (sub, s.shape[1])) for s in starts], axis=0)
    q_fac = (q * jnp.exp(g - g_start_rows)).astype(BF16)
    pad = jnp.zeros((LANES - c, kk.shape[1]), BF16)
    k_fac = []
    for i in range(n_sub):
        ki = kk * jnp.exp(jnp.where(row < (i + 1) * sub, starts[i] - g, -jnp.inf))
        k_fac += [ki.astype(BF16), pad]
    wide = _dot_nt(q_fac, jnp.concatenate(k_fac, axis=0))
    scores = jnp.concatenate([wide[i * sub:(i + 1) * sub, i * LANES:i * LANES + c] for i in range(n_sub)], axis=0)
    causal = lax.broadcasted_iota(jnp.int32, (c, c), 0) >= lax.broadcasted_iota(jnp.int32, (c, c), 1)
    return _dot(jnp.where(causal, scores, 0.0).astype(BF16), v.astype(BF16))


def _hgrn_intra_exact(q, kk, v, g):
    c, sub = HG_CHUNK, HG_SUB
    row = lax.broadcasted_iota(jnp.int32, (c, 1), 0)
    row_in_sub = row % sub
    neg = -jnp.inf
    blocks = [jnp.zeros((sub, c), F32)]
    for i in range(1, c // sub):
        g_start = g[i * sub - 1:i * sub, :]
        qi = q[i * sub:(i + 1) * sub, :] * jnp.exp(g[i * sub:(i + 1) * sub, :] - g_start)
        ki = kk * jnp.exp(jnp.where(row < i * sub, g_start - g, neg))
        blocks.append(_dot_nt(qi.astype(BF16), ki.astype(BF16)))
    o = _dot(jnp.concatenate(blocks, axis=0).astype(BF16), v.astype(BF16))
    o = o + jnp.sum(q * kk, axis=-1, keepdims=True) * v
    for d in range(1, sub):
        gs = pltpu.roll(g, d, 0)
        ks = pltpu.roll(kk, d, 0)
        vs = pltpu.roll(v, d, 0)
        dec = jnp.exp(jnp.where(row_in_sub >= d, g - gs, neg))
        o = o + jnp.sum(q * ks * dec, axis=-1, keepdims=True) * vs
    return o


def _hgrn_prompt_kernel(q_ref, f_ref, i_ref, g_ref, lb_ref, gain_ref, y_ref, sfin_ref, st_sc, inter_sc,
                        *, n_chunks):
    c = HG_CHUNK
    sub = HG_SUB
    hd = HG_HEAD_DIM
    n_h = st_sc.shape[0]
    t = pl.program_id(2)

    @pl.when(t == 0)
    def _():
        st_sc[...] = jnp.zeros_like(st_sc)

    gain = gain_ref[...]
    lb_all = lb_ref[...]
    tri = (lax.broadcasted_iota(jnp.int32, (c, c), 0) >= lax.broadcasted_iota(jnp.int32, (c, c), 1)).astype(BF16)

    def chunk(ci, carry):
        r0 = pl.multiple_of(ci * c, c)

        def forget_and_decay():
            f = lb_all + (1.0 - lb_all) * jax.nn.sigmoid(f_ref[pl.ds(r0, c), :])
            log_f = jnp.log(f)
            hi = log_f.astype(BF16)
            rest = log_f - hi.astype(F32)
            mid = rest.astype(BF16)
            lo = (rest - mid.astype(F32)).astype(BF16)
            sums = _dot(tri, jnp.concatenate([hi, mid, lo], axis=1))
            w = n_h * hd
            return f, (sums[:, 2 * w:] + sums[:, w:2 * w]) + sums[:, :w]

        def operands(hh, f, g):
            lanes = slice(hh * hd, (hh + 1) * hd)
            return _silu(q_ref[pl.ds(r0, c), lanes]), 1.0 - f[:, lanes], i_ref[pl.ds(r0, c), lanes], g[:, lanes]

        def finish(o, hh):
            lanes = slice(hh * hd, (hh + 1) * hd)
            o = o * lax.rsqrt(jnp.mean(o * o, axis=-1, keepdims=True) + RMS_EPS) * gain
            y_ref[pl.ds(r0, c), lanes] = o * _silu(g_ref[pl.ds(r0, c), lanes])

        f_all, g_all = forget_and_decay()
        span = jnp.zeros((1, hd), F32)
        for hh in range(n_h):
            q, kk, v, g = operands(hh, f_all, g_all)
            for i in range(c // sub):
                g_start = g[i * sub - 1:i * sub, :] if i else jnp.zeros_like(g[0:1, :])
                span = jnp.maximum(span, g_start - g[(i + 1) * sub - 1:(i + 1) * sub, :])
            g_last = g[c - 1:c, :]
            st = st_sc[hh]
            inter = _dot_nt((q * jnp.exp(g)).astype(BF16), st.astype(BF16))
            inter_sc[hh] = inter
            finish(inter + _hgrn_intra_factored(q, kk, v, g), hh)
            k_dec = kk * jnp.exp(g_last - g)
            st_sc[hh] = st * jnp.exp(g_last) + _dot_tn(v.astype(BF16), k_dec.astype(BF16))

        @pl.when(jnp.max(span) > HG_SAFE_SPAN)
        def _():
            f_again, g_again = forget_and_decay()
            for hh in range(n_h):
                q, kk, v, g = operands(hh, f_again, g_again)
                finish(inter_sc[hh] + _hgrn_intra_exact(q, kk, v, g), hh)

        return carry

    lax.fori_loop(0, n_chunks, chunk, 0)

    @pl.when(t == pl.num_programs(2) - 1)
    def _():
        for hh in range(n_h):
            sfin_ref[0, hh] = st_sc[hh].T


def _hgrn_prompt(z, lb, gain, bt, t):
    hd = HG_HEAD_DIM
    n_heads = lb.shape[0] // hd
    n_h = HG_HEADS_PER_STEP
    groups = n_heads // n_h
    w = n_h * hd
    tb = min(512, t)
    nt = t // tb
    m = bt * t

    def col(block):
        return pl.BlockSpec((tb, w), lambda b, h, s: (b * nt + s, block * groups + h))

    return pl.pallas_call(
        functools.partial(_hgrn_prompt_kernel, n_chunks=tb // HG_CHUNK),
        grid=(bt, groups, nt),
        in_specs=[col(1), col(2), col(3), col(4),
                  pl.BlockSpec((1, w), lambda b, h, s: (0, h)),
                  pl.BlockSpec((1, hd), lambda b, h, s: (0, 0))],
        out_specs=[pl.BlockSpec((tb, w), lambda b, h, s: (b * nt + s, h)),
                   pl.BlockSpec((1, n_h, hd, hd), lambda b, h, s: (b, h, 0, 0))],
        out_shape=[jax.ShapeDtypeStruct((m, n_heads * hd), F32),
                   jax.ShapeDtypeStruct((bt, n_heads, hd, hd), F32)],
        scratch_shapes=[pltpu.VMEM((n_h, hd, hd), F32), pltpu.VMEM((n_h, HG_CHUNK, hd), F32)],
        compiler_params=_params(("parallel", "parallel", "arbitrary")),
        name="hgrn_prompt",
    )(z, z, z, z, lb.reshape(1, -1), gain.reshape(1, hd))


HG_SAMPLE_KEYS = 32


def _hgrn_sample_kernel(q_ref, f_ref, i_ref, g_ref, lb_ref, gain_ref, s_ref, y_ref, snew_ref,
                        qt_sc, ft_sc, kt_sc, vt_sc, o_sc):
    kb = pl.program_id(1)

    @pl.when(kb == 0)
    def _():
        lb = lb_ref[...]
        f = lb + (1.0 - lb) * jax.nn.sigmoid(f_ref[...])
        qt_sc[...] = _silu(q_ref[...]).T
        ft_sc[...] = f.T
        kt_sc[...] = (1.0 - f).T
        vt_sc[...] = i_ref[...].T
        o_sc[...] = jnp.zeros_like(o_sc)

    vt = vt_sc[...]
    acc = o_sc[...]
    for kl in range(HG_SAMPLE_KEYS):
        k = kb * HG_SAMPLE_KEYS + kl
        st = s_ref[:, 0, kl, :].T
        s_new = st * ft_sc[pl.ds(k, 1), :] + vt * kt_sc[pl.ds(k, 1), :]
        acc = acc + s_new * qt_sc[pl.ds(k, 1), :]
        snew_ref[:, 0, kl, :] = s_new.T
    o_sc[...] = acc

    @pl.when(kb == pl.num_programs(1) - 1)
    def _():
        o = acc.T
        o = o * lax.rsqrt(jnp.mean(o * o, axis=-1, keepdims=True) + RMS_EPS) * gain_ref[...]
        y_ref[...] = o * _silu(g_ref[...])


def _hgrn_sample(z, states, layer, lb, gain):
    _, b, n_heads, hd, _ = states.shape
    kbs = HG_SAMPLE_KEYS

    def col(block):
        return pl.BlockSpec((b, hd), lambda h, k: (0, block * n_heads + h))

    sblk = pl.BlockSpec((b, 1, kbs, hd), lambda h, k: (0, h, k, 0))
    return pl.pallas_call(
        _hgrn_sample_kernel,
        grid=(n_heads, hd // kbs),
        in_specs=[col(1), col(2), col(3), col(4),
                  pl.BlockSpec((1, hd), lambda h, k: (0, h)),
                  pl.BlockSpec((1, hd), lambda h, k: (0, 0)),
                  pl.BlockSpec((None, b, 1, kbs, hd), lambda h, k: (layer, 0, h, k, 0))],
        out_specs=[pl.BlockSpec((b, hd), lambda h, k: (0, h)), sblk],
        out_shape=[jax.ShapeDtypeStruct((b, n_heads * hd), F32),
                   jax.ShapeDtypeStruct(states.shape[1:], F32)],
        scratch_shapes=[pltpu.VMEM((hd, b), F32)] * 5,
        compiler_params=_params(("parallel", "arbitrary")),
        name="hgrn_sample",
    )(z, z, z, z, lb.reshape(1, -1), gain.reshape(1, hd), states)


def _mix_to_hidden(x, z, y_a_pre, y_b, mw, dense, layer, alpha, tm):
    y_a = _glu(y_a_pre, dense["w_glu"], layer, tm)
    m = x.shape[0]
    merged = _merge(y_a, y_b, z, dense["w_ba"], dense["w_bb"], layer, min(MERGE_ROWS, m))
    return _ln1(merged, x, dense["w_o"], mw["ln1_g"], mw["ln1_b"], layer, alpha, tm)


def _layer_prompt(x, p, mw, lw, layer, alpha, bt, t):
    m = bt * t
    tm = min(512, m)
    dense = mw["dense_bf16"]
    z = _inproj(x, dense["w_in"], layer, min(INPROJ_ROWS, m))
    y_a_pre, f_re, f_im = _s5_prompt(z, lw["s5_prompt_ops"], bt, t)
    y_b, hg_new = _hgrn_prompt(z, lw["lb"], lw["gn_gain"], bt, t)
    h = _mix_to_hidden(x, z, y_a_pre, y_b, mw, dense, layer, alpha, tm)
    s1, s2, cw, counts = _router_sorted(h, lw["w_router"], lw["b_router"], tm)
    pos8, tab = _plan(s1, s2, counts, tm)
    pos = pos8[:2].reshape(2 * m)
    n_tiles = _moe_num_tiles(m)
    tile_expert, tile_rows, n_used = tab[0, :n_tiles], tab[1, :n_tiles], tab[2, :1]
    xs = _dispatch(h, pos, tile_rows, tm)
    ys = _experts_sorted(xs, tile_expert, tile_rows, n_used, mw["w_g"], mw["w_u"], mw["w_d"], layer)
    x_new = _final_gather(h, cw, p, pos, ys, dense["w_pg"], dense["w_pp"], mw["ln2_g"], mw["ln2_b"], layer,
                          alpha, min(tm, 256))
    return x_new, f_re, f_im, hg_new


def _layer_sample(x, p, s_re, s_im, hg_states, mw, lw, layer, alpha):
    m = x.shape[0]
    dense = mw["dense_f32"]
    z = _inproj(x, dense["w_in"], layer, m)
    y_a_pre, n_re, n_im = _s5_sample(z, s_re, s_im, lw["s5_sample_ops"])
    y_b, hg_new = _hgrn_sample(z, hg_states, layer, lw["lb"], lw["gn_gain"])
    h = _mix_to_hidden(x, z, y_a_pre, y_b, mw, dense, layer, alpha, m)
    comb = _router_dense(h, lw["w_router"], lw["b_router"], m)
    moe = _moe_dense(h, comb, mw["w_g"], mw["w_u"], mw["w_d"], layer, m)
    x_new = _final(h, moe, p, dense["w_pg"], dense["w_pp"], mw["ln2_g"], mw["ln2_b"], layer, alpha, m)
    return x_new, n_re, n_im, hg_new


def _layer_operands(i, lb_all, s5_lambda_re, s5_lambda_im, s5_log_dt, s5_b_re, s5_b_im, s5_c_re, s5_c_im, s5_d,
                    hg_norm_gain, w_group_router, b_group_router, w_expert_router, b_expert_router):
    disc = _s5_discretise(s5_lambda_re[i], s5_lambda_im[i], s5_log_dt[i], s5_b_re[i], s5_b_im[i])
    d_model = w_group_router.shape[1]
    pad = LANES - N_EXPERTS - N_GROUPS
    w_router = jnp.concatenate([w_expert_router[i], w_group_router[i], jnp.zeros((d_model, pad), F32)], axis=1)
    b_router = jnp.concatenate([b_expert_router[i], b_group_router[i], jnp.zeros((pad,), F32)]).reshape(1, LANES)
    return {
        "s5_width": s5_d.shape[1],
        "s5_prompt_ops": _s5_prompt_operators(*disc, s5_c_re[i], s5_c_im[i], s5_d[i]),
        "s5_sample_ops": _s5_sample_operators(*disc, s5_c_re[i], s5_c_im[i], s5_d[i]),
        "lb": lb_all[i],
        "gn_gain": hg_norm_gain[i],
        "w_router": w_router, "b_router": b_router,
    }


def kernel(x_prompt, x_sample, p_prompt, p_sample, state_s5_re, state_s5_im, state_hgrn, w_in, s5_lambda_re, s5_lambda_im, s5_log_dt, s5_b_re, s5_b_im, s5_c_re, s5_c_im, s5_d, s5_w_glu, hg_lower_bounds, hg_norm_gain, w_branch_a, w_branch_b, w_out, ln1_gain, ln1_bias, w_group_router, b_group_router, w_expert_router, b_expert_router, w_exp_gate, w_exp_up, w_exp_down, w_ple_proj, w_ple_gate, ln2_gain, ln2_bias):
    depth = w_in.shape[0]
    bt, t, d_model = x_prompt.shape
    bs = x_sample.shape[0]
    alpha = (2 * depth) ** 0.25
    n_groups, n_state = s5_lambda_re.shape[1:]

    lb_soft = jax.nn.softmax(hg_lower_bounds.astype(F32), axis=0)
    lb_all = jnp.cumsum(lb_soft, axis=0) - lb_soft[0]

    dense_f32 = {"w_in": w_in, "w_glu": s5_w_glu, "w_ba": w_branch_a, "w_bb": w_branch_b,
                 "w_o": w_out, "w_pp": w_ple_proj, "w_pg": w_ple_gate}
    mw = {"dense_f32": dense_f32,
          "dense_bf16": {k: v.astype(BF16) for k, v in dense_f32.items()},
          "w_g": w_exp_gate, "w_u": w_exp_up, "w_d": w_exp_down,
          "ln1_g": ln1_gain, "ln1_b": ln1_bias, "ln2_g": ln2_gain, "ln2_b": ln2_bias}
    pp = p_prompt.reshape(depth, bt * t, -1)
    ps = p_sample.reshape(depth, bs, -1)

    xp = x_prompt.reshape(bt * t, d_model)
    xs = x_sample.reshape(bs, d_model)
    outs = [[] for _ in range(6)]
    for i in range(depth):
        lw = _layer_operands(i, lb_all, s5_lambda_re, s5_lambda_im, s5_log_dt, s5_b_re, s5_b_im, s5_c_re,
                             s5_c_im, s5_d, hg_norm_gain, w_group_router, b_group_router, w_expert_router,
                             b_expert_router)
        xp, a_re, a_im, a_hg = _layer_prompt(xp, pp, mw, lw, i, alpha, bt, t)
        xs, b_re, b_im, b_hg = _layer_sample(
            xs, ps, state_s5_re[i].reshape(bs, n_groups * n_state), state_s5_im[i].reshape(bs, n_groups * n_state),
            state_hgrn, mw, lw, i, alpha)
        for lst, val in zip(outs, (a_re.reshape(bt, n_groups, n_state), a_im.reshape(bt, n_groups, n_state), a_hg,
                                   b_re.reshape(bs, n_groups, n_state), b_im.reshape(bs, n_groups, n_state), b_hg)):
            lst.append(val)

    return (xp.reshape(bt, t, d_model), xs.reshape(bs, 1, d_model), *[jnp.stack(o) for o in outs])
```

```python
import functools
import math

import jax
import jax.numpy as jnp
from jax import lax
from jax.experimental import pallas as pl
from jax.experimental.pallas import tpu as pltpu

F32 = jnp.float32
BF16 = jnp.bfloat16
HIGHEST = lax.Precision.HIGHEST

LANES = 128
SUBLANES = 8
VMEM_LIMIT_BYTES = 56 * 1024 * 1024

S5_GROUP = 16
S5_STATE = 64
S5_CHUNK = 16
HG_HEAD_DIM = 128
HG_CHUNK = 64
HG_SUB = 16
N_GROUPS = 4
EXPERTS_PER_GROUP = 8
N_EXPERTS = N_GROUPS * EXPERTS_PER_GROUP
LN_EPS = 1e-5
RMS_EPS = 1e-6


def _params(semantics):
    return pltpu.CompilerParams(dimension_semantics=semantics, vmem_limit_bytes=VMEM_LIMIT_BYTES)


def _dot(a, b):
    return jnp.dot(a, b, preferred_element_type=F32)


def _mm(a, w):
    if w.dtype == F32:
        a_hi = a.astype(BF16)
        a_lo = (a - a_hi.astype(F32)).astype(BF16)
        w_hi = w.astype(BF16)
        w_lo = (w - w_hi.astype(F32)).astype(BF16)
        return _dot(a_hi, w_hi) + (_dot(a_hi, w_lo) + _dot(a_lo, w_hi))
    return jnp.dot(a.astype(BF16), w, preferred_element_type=F32)


def _dot_nt(a, b):
    return lax.dot_general(a, b, (((1,), (1,)), ((), ())), preferred_element_type=F32)


def _dot_tn(a, b):
    return lax.dot_general(a, b, (((0,), (0,)), ((), ())), preferred_element_type=F32)


def _sigmoid(x):
    return 0.5 * jnp.tanh(0.5 * x) + 0.5


def _silu(x):
    return x * _sigmoid(x)


def _layer_norm_rows(x, gain, bias):
    mu = jnp.mean(x, axis=-1, keepdims=True)
    xc = x - mu
    var = jnp.mean(xc * xc, axis=-1, keepdims=True)
    return xc * lax.rsqrt(var + LN_EPS) * gain + bias


INPROJ_ROWS = 1024


def _inproj_kernel(x_ref, w_ref, o_ref, xb_ref):
    if w_ref.dtype == F32:
        o_ref[...] = _mm(x_ref[...], w_ref[...])
        return

    @pl.when(pl.program_id(1) == 0)
    def _():
        xb_ref[...] = x_ref[...].astype(BF16)

    o_ref[...] = _dot(xb_ref[...], w_ref[...])


def _layer_spec(block, layer, index_map):
    return pl.BlockSpec((None,) + block, lambda *idx: (layer,) + index_map(*idx))


def _inproj(x, w, layer, tm, tn=1024):
    m, k = x.shape
    n = w.shape[-1]
    return pl.pallas_call(
        _inproj_kernel,
        grid=(m // tm, n // tn),
        in_specs=[pl.BlockSpec((tm, k), lambda i, j: (i, 0)),
                  _layer_spec((k, tn), layer, lambda i, j: (0, j))],
        out_specs=pl.BlockSpec((tm, tn), lambda i, j: (i, j)),
        out_shape=jax.ShapeDtypeStruct((m, n), F32),
        scratch_shapes=[pltpu.VMEM((tm, k), BF16)],
        compiler_params=_params(("parallel", "arbitrary")),
        name="inproj",
    )(x, w)


def _glu_kernel(a_ref, w_ref, o_ref):
    a = a_ref[...]
    o_ref[...] = a * _sigmoid(_mm(a, w_ref[...]))


def _glu(a, w, layer, tm):
    m, k = a.shape
    return pl.pallas_call(
        _glu_kernel,
        grid=(m // tm,),
        in_specs=[pl.BlockSpec((tm, k), lambda i: (i, 0)),
                  _layer_spec((k, k), layer, lambda i: (0, 0))],
        out_specs=pl.BlockSpec((tm, k), lambda i: (i, 0)),
        out_shape=jax.ShapeDtypeStruct((m, k), F32),
        compiler_params=_params(("parallel",)),
        name="s5_glu",
    )(a, w)


def _merge_kernel(ya_ref, yb_ref, ga_ref, gb_ref, wa_ref, wb_ref, o_ref):
    pa = _mm(ya_ref[...], wa_ref[...])
    pb = _mm(yb_ref[...], wb_ref[...])
    o_ref[...] = _sigmoid(ga_ref[...]) * pa + _sigmoid(gb_ref[...]) * pb


MERGE_ROWS = 1024
MERGE_COLS = 512


def _merge(ya, yb, z, wa, wb, layer, tm, tn=MERGE_COLS):
    m, k = ya.shape
    n = wa.shape[-1]
    nj = n // tn
    gate_a = 5 * k // tn
    gate_b = gate_a + nj
    return pl.pallas_call(
        _merge_kernel,
        grid=(m // tm, nj),
        in_specs=[pl.BlockSpec((tm, k), lambda i, j: (i, 0)),
                  pl.BlockSpec((tm, k), lambda i, j: (i, 0)),
                  pl.BlockSpec((tm, tn), lambda i, j: (i, gate_a + j)),
                  pl.BlockSpec((tm, tn), lambda i, j: (i, gate_b + j)),
                  _layer_spec((k, tn), layer, lambda i, j: (0, j)),
                  _layer_spec((k, tn), layer, lambda i, j: (0, j))],
        out_specs=pl.BlockSpec((tm, tn), lambda i, j: (i, j)),
        out_shape=jax.ShapeDtypeStruct((m, n), F32),
        compiler_params=_params(("parallel", "arbitrary")),
        name="branch_merge",
    )(ya, yb, z, z, wa, wb)


def _ln1_kernel(mg_ref, x_ref, w_ref, g_ref, b_ref, o_ref, *, alpha):
    r = alpha * x_ref[...] + _mm(mg_ref[...], w_ref[...])
    o_ref[...] = _layer_norm_rows(r, g_ref[...], b_ref[...])


def _ln1(merged, x, w, gain, bias, layer, alpha, tm):
    m, d = x.shape
    row = pl.BlockSpec((tm, d), lambda i: (i, 0))
    vec = _layer_spec((1, d), layer, lambda i: (0, 0))
    return pl.pallas_call(
        functools.partial(_ln1_kernel, alpha=alpha),
        grid=(m // tm,),
        in_specs=[row, row, _layer_spec((d, d), layer, lambda i: (0, 0)), vec, vec],
        out_specs=row,
        out_shape=jax.ShapeDtypeStruct((m, d), F32),
        compiler_params=_params(("parallel",)),
        name="outproj_ln1",
    )(merged, x, w, gain[:, None, :], bias[:, None, :])


def _final_kernel(h_ref, moe_ref, p_ref, wg_ref, wp_ref, g_ref, b_ref, o_ref, *, alpha):
    h = h_ref[...]
    gate = _sigmoid(_mm(h, wg_ref[...]))
    proj = _mm(p_ref[...], wp_ref[...])
    r = alpha * h + moe_ref[...] + gate * proj
    o_ref[...] = _layer_norm_rows(r, g_ref[...], b_ref[...])


def _final(h, moe, p, wg, wp, gain, bias, layer, alpha, tm):
    m, d = h.shape
    dp = p.shape[-1]
    row = pl.BlockSpec((tm, d), lambda i: (i, 0))
    vec = _layer_spec((1, d), layer, lambda i: (0, 0))
    return pl.pallas_call(
        functools.partial(_final_kernel, alpha=alpha),
        grid=(m // tm,),
        in_specs=[row, row, _layer_spec((tm, dp), layer, lambda i: (i, 0)),
                  _layer_spec((d, d), layer, lambda i: (0, 0)),
                  _layer_spec((dp, d), layer, lambda i: (0, 0)), vec, vec],
        out_specs=row,
        out_shape=jax.ShapeDtypeStruct((m, d), F32),
        compiler_params=_params(("parallel",)),
        name="ple_ln2",
    )(h, moe, p, wg, wp, gain[:, None, :], bias[:, None, :])


def _route(h, w, b):
    logits = _mm(h, w) + b
    lane = lax.broadcasted_iota(jnp.int32, logits.shape, 1)
    lane_f = lane.astype(F32)
    neg = -jnp.inf
    gl = jnp.where(lane >= N_EXPERTS, jnp.where(lane < N_EXPERTS + N_GROUPS, logits, neg), neg)
    gmax = jnp.max(gl, axis=-1, keepdims=True)
    denom = jnp.sum(jnp.exp(gl - gmax), axis=-1, keepdims=True)
    grp_p = 1.0 / denom
    gidx = jnp.min(jnp.where(gl == gmax, lane_f, 1e9), axis=-1, keepdims=True) - N_EXPERTS
    lane_grp = (lane // EXPERTS_PER_GROUP).astype(F32)
    el = jnp.where(lane < N_EXPERTS, jnp.where(lane_grp == gidx, logits, neg), neg)
    t1 = jnp.max(el, axis=-1, keepdims=True)
    i1 = jnp.min(jnp.where(el == t1, lane_f, 1e9), axis=-1, keepdims=True)
    el2 = jnp.where(lane_f == i1, neg, el)
    t2 = jnp.max(el2, axis=-1, keepdims=True)
    i2 = jnp.min(jnp.where(el2 == t2, lane_f, 1e9), axis=-1, keepdims=True)
    e2 = jnp.exp(t2 - t1)
    w1 = 1.0 / (1.0 + e2)
    w2 = e2 * w1
    return lane, lane_f, i1, i2, grp_p * w1, grp_p * w2


def _router_dense_kernel(h_ref, w_ref, b_ref, o_ref):
    _, lane_f, i1, i2, c1, c2 = _route(h_ref[...], w_ref[...], b_ref[...])
    o_ref[...] = jnp.where(lane_f == i1, c1, jnp.where(lane_f == i2, c2, 0.0))


def _router_dense(h, w_r, b_r, tm):
    m, d = h.shape
    return pl.pallas_call(
        _router_dense_kernel,
        grid=(m // tm,),
        in_specs=[pl.BlockSpec((tm, d), lambda i: (i, 0)),
                  pl.BlockSpec((d, LANES), lambda i: (0, 0)),
                  pl.BlockSpec((1, LANES), lambda i: (0, 0))],
        out_specs=pl.BlockSpec((tm, LANES), lambda i: (i, 0)),
        out_shape=jax.ShapeDtypeStruct((m, LANES), F32),
        compiler_params=_params(("parallel",)),
        name="router_dense",
    )(h, w_r, b_r)


def _moe_dense_kernel(h_ref, c_ref, wg_ref, wu_ref, wd_ref, o_ref):
    e = pl.program_id(1)

    @pl.when(e == 0)
    def _():
        o_ref[...] = jnp.zeros_like(o_ref)

    comb = c_ref[...]
    lane = lax.broadcasted_iota(jnp.int32, comb.shape, 1)
    col = jnp.sum(jnp.where(lane == e, comb, 0.0), axis=-1, keepdims=True)
    h = h_ref[...]
    hid = _silu(_mm(h, wg_ref[0])) * _mm(h, wu_ref[0]) * col
    o_ref[...] += _mm(hid, wd_ref[0])


def _moe_dense(h, comb, wg, wu, wd, layer, tm):
    m, d = h.shape
    _, n_e, _, f = wg.shape
    return pl.pallas_call(
        _moe_dense_kernel,
        grid=(m // tm, n_e),
        in_specs=[pl.BlockSpec((tm, d), lambda i, e: (i, 0)),
                  pl.BlockSpec((tm, LANES), lambda i, e: (i, 0)),
                  _layer_spec((1, d, f), layer, lambda i, e: (e, 0, 0)),
                  _layer_spec((1, d, f), layer, lambda i, e: (e, 0, 0)),
                  _layer_spec((1, f, d), layer, lambda i, e: (e, 0, 0))],
        out_specs=pl.BlockSpec((tm, d), lambda i, e: (i, 0)),
        out_shape=jax.ShapeDtypeStruct((m, d), F32),
        compiler_params=_params(("parallel", "arbitrary")),
        name="moe_dense",
    )(h, comb, wg, wu, wd)


MOE_TILE = 256


def _moe_num_tiles(m):
    return 2 * m // MOE_TILE + N_EXPERTS


def _router_sorted_kernel(h_ref, w_ref, b_ref, s1_ref, s2_ref, cw_ref, cnt_ref, carry_sc):
    @pl.when(pl.program_id(0) == 0)
    def _():
        carry_sc[...] = jnp.zeros_like(carry_sc)

    lane, lane_f, i1, i2, c1, c2 = _route(h_ref[...], w_ref[...], b_ref[...])
    sel1 = jnp.where(lane_f == i1, 1.0, 0.0)
    sel2 = jnp.where(lane_f == i2, 1.0, 0.0)
    cnt = sel1 + sel2
    tm = cnt.shape[0]
    strict_lower = (lax.broadcasted_iota(jnp.int32, (tm, tm), 0) > lax.broadcasted_iota(jnp.int32, (tm, tm), 1))
    prefix = _dot(strict_lower.astype(BF16), cnt.astype(BF16)) + carry_sc[0:1, :]
    s1_ref[...] = sel1 * (prefix + 1.0)
    s2_ref[...] = sel2 * (prefix + 1.0)
    cw_ref[...] = jnp.where(lane == 0, c1, jnp.where(lane == 1, c2, 0.0))
    carry_sc[...] = carry_sc[...] + jnp.sum(cnt, axis=0, keepdims=True)
    cnt_ref[...] = carry_sc[...]


def _router_sorted(h, w_r, b_r, tm):
    m, d = h.shape
    tok = pl.BlockSpec((tm, LANES), lambda i: (i, 0))
    return pl.pallas_call(
        _router_sorted_kernel,
        grid=(m // tm,),
        in_specs=[pl.BlockSpec((tm, d), lambda i: (i, 0)),
                  pl.BlockSpec((d, LANES), lambda i: (0, 0)),
                  pl.BlockSpec((1, LANES), lambda i: (0, 0))],
        out_specs=[tok, tok, tok, pl.BlockSpec((SUBLANES, LANES), lambda i: (0, 0))],
        out_shape=[jax.ShapeDtypeStruct((m, LANES), F32)] * 3 + [jax.ShapeDtypeStruct((SUBLANES, LANES), F32)],
        scratch_shapes=[pltpu.VMEM((SUBLANES, LANES), F32)],
        compiler_params=_params(("arbitrary",)),
        name="router_sorted",
    )(h, w_r, b_r)


def _plan_kernel(s1_ref, s2_ref, cnt_ref, pos_ref, tab_ref):
    te = float(MOE_TILE)
    lane = lax.broadcasted_iota(jnp.int32, (1, LANES), 1)
    cnt = jnp.where(lane < N_EXPERTS, cnt_ref[0:1, :], 0.0)
    padded = jnp.floor((cnt + (te - 1.0)) * (1.0 / te)) * te
    r128 = lax.broadcasted_iota(jnp.int32, (LANES, LANES), 0)
    c128 = lax.broadcasted_iota(jnp.int32, (LANES, LANES), 1)
    before = jnp.where(r128 < c128, 1.0, 0.0)
    off = jnp.dot(jnp.broadcast_to(padded, (SUBLANES, LANES)), before, precision=HIGHEST,
                  preferred_element_type=F32)[0:1, :]
    s1 = s1_ref[...]
    s2 = s2_ref[...]
    v1 = jnp.where(s1 > 0.0, s1 - 1.0 + off, 0.0)
    v2 = jnp.where(s2 > 0.0, s2 - 1.0 + off, 0.0)
    ones8 = jnp.ones((SUBLANES, LANES), F32)
    p1 = lax.dot_general(ones8, v1, (((1,), (1,)), ((), ())), precision=HIGHEST, preferred_element_type=F32)
    p2 = lax.dot_general(ones8, v2, (((1,), (1,)), ((), ())), precision=HIGHEST, preferred_element_type=F32)
    row8 = lax.broadcasted_iota(jnp.int32, p1.shape, 0)
    pos_ref[...] = jnp.where(row8 == 0, p1, p2).astype(jnp.int32)

    def per_expert(row):
        return jnp.broadcast_to(row, (LANES, LANES)).T
    off_e = per_expert(off)
    end_e = per_expert(off + padded)
    cnt_e = per_expert(cnt)
    start = c128.astype(F32) * te
    is_e = r128 < N_EXPERTS
    tile_e = jnp.sum(jnp.where(is_e, jnp.where(end_e <= start, 1.0, 0.0), 0.0), axis=0, keepdims=True)
    rows = jnp.clip(cnt_e - (start - off_e), 0.0, te)
    owns = jnp.where(is_e, jnp.where(off_e <= start, jnp.where(start < end_e, rows, 0.0), 0.0), 0.0)
    n_rows = jnp.sum(owns, axis=0, keepdims=True)
    tile_e = jnp.minimum(tile_e, float(N_EXPERTS - 1))
    n_used = jnp.sum(jnp.where(n_rows > 0.0, 1.0, 0.0), axis=-1, keepdims=True)
    row_t = lax.broadcasted_iota(jnp.int32, (SUBLANES, LANES), 0)
    tab_ref[...] = jnp.where(row_t == 0, tile_e, jnp.where(row_t == 1, n_rows, n_used)).astype(jnp.int32)


def _plan(s1, s2, counts, tm):
    m = s1.shape[0]
    tok = pl.BlockSpec((tm, LANES), lambda i: (i, 0))
    return pl.pallas_call(
        _plan_kernel,
        grid=(m // tm,),
        in_specs=[tok, tok, pl.BlockSpec((SUBLANES, LANES), lambda i: (0, 0))],
        out_specs=[pl.BlockSpec((SUBLANES, tm), lambda i: (0, i)),
                   pl.BlockSpec((SUBLANES, LANES), lambda i: (0, 0))],
        out_shape=[jax.ShapeDtypeStruct((SUBLANES, m), jnp.int32),
                   jax.ShapeDtypeStruct((SUBLANES, LANES), jnp.int32)],
        compiler_params=_params(("arbitrary",)),
        name="moe_plan",
    )(s1, s2, counts)


DMA_UNROLL = 8


def _dispatch_kernel(pos_ref, rows_ref, h_ref, xs_hbm, zero_buf, zero_sem, sem, *, m, n_tiles):
    i = pl.program_id(0)
    tm = h_ref.shape[0]

    @pl.when(i == 0)
    def _():
        zero_buf[...] = jnp.zeros_like(zero_buf)

        def zero_copy(j):
            return pltpu.make_async_copy(zero_buf, xs_hbm.at[pl.ds(j * MOE_TILE, MOE_TILE)], zero_sem)

        def zero_start(j, c):
            @pl.when(rows_ref[j] < MOE_TILE)
            def _():
                zero_copy(j).start()
            return c

        def zero_wait(j, c):
            @pl.when(rows_ref[j] < MOE_TILE)
            def _():
                zero_copy(j).wait()
            return c

        lax.fori_loop(0, n_tiles, zero_start, 0)
        lax.fori_loop(0, n_tiles, zero_wait, 0)

    def body(r, c):
        t = i * tm + r
        src = h_ref.at[pl.ds(r, 1)]
        pltpu.make_async_copy(src, xs_hbm.at[pl.ds(pos_ref[t], 1)], sem).start()
        pltpu.make_async_copy(src, xs_hbm.at[pl.ds(pos_ref[m + t], 1)], sem).start()
        return c
    lax.fori_loop(0, tm, body, 0, unroll=DMA_UNROLL)

    pltpu.make_async_copy(xs_hbm.at[pl.ds(0, 2 * tm)], xs_hbm.at[pl.ds(0, 2 * tm)], sem).wait()


def _dispatch(h, pos, tile_rows, tm):
    m, d = h.shape
    n_tiles = _moe_num_tiles(m)
    grid_spec = pltpu.PrefetchScalarGridSpec(
        num_scalar_prefetch=2,
        grid=(m // tm,),
        in_specs=[pl.BlockSpec((tm, d), lambda i, pos, rows: (i, 0))],
        out_specs=pl.BlockSpec(memory_space=pl.ANY),
        scratch_shapes=[pltpu.VMEM((MOE_TILE, d), F32), pltpu.SemaphoreType.DMA(()),
                        pltpu.SemaphoreType.DMA(())],
    )
    return pl.pallas_call(
        functools.partial(_dispatch_kernel, m=m, n_tiles=n_tiles),
        grid_spec=grid_spec,
        out_shape=jax.ShapeDtypeStruct((n_tiles * MOE_TILE, d), F32),
        compiler_params=_params(("arbitrary",)),
        name="moe_dispatch",
    )(pos, tile_rows, h)


def _experts_kernel(te_ref, nr_ref, nu_ref, x_ref, wg_ref, wu_ref, wd_ref, y_ref, wgb, wub, wdb):
    j = pl.program_id(0)
    new_expert = jnp.logical_or(j == 0, te_ref[j] != te_ref[jnp.maximum(j - 1, 0)])

    @pl.when(jnp.logical_and(new_expert, nr_ref[j] > 0))
    def _():
        wgb[...] = wg_ref[0].astype(BF16)
        wub[...] = wu_ref[0].astype(BF16)
        wdb[...] = wd_ref[0].astype(BF16)

    @pl.when(nr_ref[j] > 0)
    def _():
        x = x_ref[...].astype(BF16)
        hid = _silu(_dot(x, wgb[...])) * _dot(x, wub[...])
        y_ref[...] = _dot(hid.astype(BF16), wdb[...])

    @pl.when(nr_ref[j] == 0)
    def _():
        y_ref[...] = jnp.zeros_like(y_ref)


def _experts_sorted(xs, tile_expert, tile_rows, n_used, wg, wu, wd, layer):
    n_rows, d = xs.shape
    _, n_e, _, f = wg.shape
    n_tiles = n_rows // MOE_TILE

    def used_tile(j, te, nr, nu):
        return (jnp.minimum(j, nu[0] - 1), 0)

    def expert_block(j, te, nr, nu):
        return (layer, te[j], 0, 0)

    grid_spec = pltpu.PrefetchScalarGridSpec(
        num_scalar_prefetch=3,
        grid=(n_tiles,),
        in_specs=[pl.BlockSpec((MOE_TILE, d), used_tile),
                  pl.BlockSpec((None, 1, d, f), expert_block),
                  pl.BlockSpec((None, 1, d, f), expert_block),
                  pl.BlockSpec((None, 1, f, d), expert_block)],
        out_specs=pl.BlockSpec((MOE_TILE, d), lambda j, te, nr, nu: (j, 0)),
        scratch_shapes=[pltpu.VMEM((d, f), BF16), pltpu.VMEM((d, f), BF16), pltpu.VMEM((f, d), BF16)],
    )
    return pl.pallas_call(
        _experts_kernel,
        grid_spec=grid_spec,
        out_shape=jax.ShapeDtypeStruct((n_rows, d), F32),
        compiler_params=_params(("arbitrary",)),
        name="moe_experts_sorted",
    )(tile_expert, tile_rows, n_used, xs, wg, wu, wd)


def _final_gather_kernel(pos_ref, h_ref, cw_ref, p_ref, wg_ref, wp_ref, g_ref, b_ref, ys_hbm, o_ref,
                         ybuf, sem, *, alpha, m):
    i = pl.program_id(0)
    slot = i % 2
    tm = h_ref.shape[0]

    last = pl.num_programs(0) - 1

    def start_rows(tile, dst_slot, r):
        t = tile * tm + r
        for pick in range(2):
            pltpu.make_async_copy(ys_hbm.at[pl.ds(pos_ref[pick * m + t], 1)],
                                  ybuf.at[dst_slot, pl.ds(pick * tm + r, 1)], sem.at[dst_slot]).start()

    def wait_tile(s):
        pltpu.make_async_copy(ys_hbm.at[pl.ds(0, 2 * tm)], ybuf.at[s], sem.at[s]).wait()

    @pl.when(i == 0)
    def _():
        def body(r, c):
            start_rows(0, 0, r)
            return c
        lax.fori_loop(0, tm, body, 0, unroll=DMA_UNROLL)

    nxt = jnp.minimum(i + 1, last)
    for r in range(tm):
        start_rows(nxt, 1 - slot, r)

    h = h_ref[...]
    gate = _sigmoid(_mm(h, wg_ref[...]))
    proj = _mm(p_ref[...], wp_ref[...])

    wait_tile(slot)
    cw = cw_ref[...]
    moe = cw[:, 0:1] * ybuf[slot, 0:tm] + cw[:, 1:2] * ybuf[slot, tm:2 * tm]
    r = alpha * h + moe + gate * proj
    o_ref[...] = _layer_norm_rows(r, g_ref[...], b_ref[...])

    @pl.when(i == last)
    def _():
        wait_tile(1 - slot)


def _final_gather(h, cw, p, pos, ys, wg, wp, gain, bias, layer, alpha, tm):
    m, d = h.shape
    dp = p.shape[-1]
    row = pl.BlockSpec((tm, d), lambda i, pos: (i, 0))
    vec = _layer_spec((1, d), layer, lambda i, pos: (0, 0))
    grid_spec = pltpu.PrefetchScalarGridSpec(
        num_scalar_prefetch=1,
        grid=(m // tm,),
        in_specs=[row, pl.BlockSpec((tm, LANES), lambda i, pos: (i, 0)),
                  _layer_spec((tm, dp), layer, lambda i, pos: (i, 0)),
                  _layer_spec((d, d), layer, lambda i, pos: (0, 0)),
                  _layer_spec((dp, d), layer, lambda i, pos: (0, 0)), vec, vec,
                  pl.BlockSpec(memory_space=pl.ANY)],
        out_specs=row,
        scratch_shapes=[pltpu.VMEM((2, 2 * tm, d), F32), pltpu.SemaphoreType.DMA((2,))],
    )
    return pl.pallas_call(
        functools.partial(_final_gather_kernel, alpha=alpha, m=m),
        grid_spec=grid_spec,
        out_shape=jax.ShapeDtypeStruct((m, d), F32),
        compiler_params=_params(("arbitrary",)),
        name="ple_ln2_gather",
    )(pos, h, cw, p, wg, wp, gain[:, None, :], bias[:, None, :], ys)


def _s5_discretise(lam_re, lam_im, log_dt, b_re, b_im):
    dt = jnp.exp(log_dt)[:, None]
    mag = jnp.exp(lam_re * dt)
    ab_re = mag * jnp.cos(lam_im * dt)
    ab_im = mag * jnp.sin(lam_im * dt)
    den = lam_re * lam_re + lam_im * lam_im
    nr = ab_re - 1.0
    zr = (nr * lam_re + ab_im * lam_im) / den
    zi = (ab_im * lam_re - nr * lam_im) / den
    bb_re = zr[..., None] * b_re - zi[..., None] * b_im
    bb_im = zr[..., None] * b_im + zi[..., None] * b_re
    return ab_re, ab_im, bb_re, bb_im


def _s5_prompt_operators(ab_re, ab_im, bb_re, bb_im, c_re, c_im, d_skip):
    g, n = ab_re.shape
    cch = bb_re.shape[-1]
    L = S5_CHUNK
    pw_re = [jnp.ones_like(ab_re)]
    pw_im = [jnp.zeros_like(ab_im)]
    for _ in range(L):
        pr, pi = pw_re[-1], pw_im[-1]
        pw_re.append(pr * ab_re - pi * ab_im)
        pw_im.append(pr * ab_im + pi * ab_re)
    a_re = jnp.stack(pw_re)
    a_im = jnp.stack(pw_im)
    w_re = a_re[:L, :, :, None] * bb_re - a_im[:L, :, :, None] * bb_im
    w_im = a_re[:L, :, :, None] * bb_im + a_im[:L, :, :, None] * bb_re
    kern = (jnp.einsum('gcn,kgnd->kgcd', c_re, w_re, precision=HIGHEST)
            - jnp.einsum('gcn,kgnd->kgcd', c_im, w_im, precision=HIGHEST))
    steps = jnp.arange(L)
    lag_is = (steps[None, None, :] - steps[None, :, None] == steps[:, None, None]).astype(F32)
    toe = jnp.einsum('kio,kgcd->gidoc', lag_is, kern, precision=HIGHEST)
    toe = toe.reshape(g, L * cch, L * cch)

    p_re = w_re[::-1].transpose(1, 0, 3, 2).reshape(g, L * cch, n)
    p_im = w_im[::-1].transpose(1, 0, 3, 2).reshape(g, L * cch, n)
    q_re = (jnp.einsum('gcn,tgn->gntc', c_re, a_re[1:]) - jnp.einsum('gcn,tgn->gntc', c_im, a_im[1:]))
    q_im = -(jnp.einsum('gcn,tgn->gntc', c_re, a_im[1:]) + jnp.einsum('gcn,tgn->gntc', c_im, a_re[1:]))
    q_re = q_re.reshape(g, n, L * cch)
    q_im = q_im.reshape(g, n, L * cch)

    hp = g // 2
    w = L * cch
    pe_re, po_re, pe_im, po_im = (x.astype(BF16) for x in (p_re[0::2], p_re[1::2], p_im[0::2], p_im[1::2]))
    qe_re, qo_re, qe_im, qo_im = (x.astype(BF16) for x in (q_re[0::2], q_re[1::2], q_im[0::2], q_im[1::2]))
    zp = jnp.zeros_like(pe_re)
    zq = jnp.zeros_like(qe_re)
    p2 = jnp.concatenate([jnp.concatenate([pe_re, zp, pe_im, zp], axis=2),
                          jnp.concatenate([zp, po_re, zp, po_im], axis=2)], axis=1)
    q2 = jnp.concatenate([jnp.concatenate([qe_re, zq], axis=2), jnp.concatenate([zq, qo_re], axis=2),
                          jnp.concatenate([qe_im, zq], axis=2), jnp.concatenate([zq, qo_im], axis=2)], axis=1)
    a_chunk = jnp.stack([a_re[L].reshape(hp, 2 * n), a_im[L].reshape(hp, 2 * n)], axis=1)
    d2 = jnp.broadcast_to(d_skip.reshape(hp, 2, 1, cch), (hp, 2, L, cch)).reshape(hp, 1, 2 * w)
    return toe.astype(BF16), p2, q2, a_chunk, d2


def _s5_prompt_kernel(u_ref, t_ref, p_ref, q_ref, a_ref, d_ref, y_ref, fre_ref, fim_ref,
                      s_sc, xin_sc, *, n_chunks, bt):
    half = a_ref.shape[-1]
    u0 = jnp.concatenate([u_ref[0, 0], u_ref[0, 1]], axis=1)
    u1 = jnp.concatenate([u_ref[1, 0], u_ref[1, 1]], axis=1)
    ub0 = u0.astype(BF16)
    ub1 = u1.astype(BF16)
    y_intra = jnp.concatenate([_dot(ub0, t_ref[0]), _dot(ub1, t_ref[1])], axis=1)
    s_sc[...] = _dot(jnp.concatenate([ub0, ub1], axis=1), p_ref[0])
    ar = a_ref[0, 0:1, :]
    ai = a_ref[0, 1:2, :]
    xr = jnp.zeros((bt, half), F32)
    xi = jnp.zeros((bt, half), F32)
    for j in range(n_chunks):
        rows = slice(j * bt, (j + 1) * bt)
        xin_sc[rows, 0:half] = xr
        xin_sc[rows, half:2 * half] = xi
        sr = s_sc[rows, 0:half]
        si = s_sc[rows, half:2 * half]
        xr, xi = ar * xr - ai * xi + sr, ar * xi + ai * xr + si
    fre_ref[...] = xr
    fim_ref[...] = xi
    y_carry = _dot(xin_sc[...].astype(BF16), q_ref[0])
    y = y_intra + y_carry + d_ref[0] * jnp.concatenate([u0, u1], axis=1)
    y = jax.nn.gelu(y)
    for k in range(4):
        y_ref[k // 2, k % 2] = y[:, k * LANES:(k + 1) * LANES]


S5_RELAYOUT_CHUNKS = 32
S5_RELAYOUT_GROUPS = LANES // S5_GROUP


def _to_chunks_kernel(x_ref, o_ref):
    L, cch = S5_CHUNK, S5_GROUP
    bt = x_ref.shape[0]
    nj = x_ref.shape[1] // L
    per_tile = LANES // cch
    for b in range(bt):
        at_pos = [x_ref[b, pl.ds(p, nj, stride=L), :] for p in range(L)]
        for g in range(S5_RELAYOUT_GROUPS):
            for h in range(L // per_tile):
                piece = jnp.concatenate(
                    [at_pos[h * per_tile + p][:, g * cch:(g + 1) * cch] for p in range(per_tile)], axis=1)
                o_ref[g, h, pl.ds(b, nj, stride=bt), :] = piece


def _from_chunks_kernel(y_ref, o_ref):
    L, cch = S5_CHUNK, S5_GROUP
    bt = o_ref.shape[0]
    nj = o_ref.shape[1] // L
    per_tile = LANES // cch
    for b in range(bt):
        tiles = [[y_ref[g, h, pl.ds(b, nj, stride=bt), :] for h in range(L // per_tile)]
                 for g in range(S5_RELAYOUT_GROUPS)]
        for p in range(L):
            h, q = divmod(p, per_tile)
            o_ref[b, pl.ds(p, nj, stride=L), :] = jnp.concatenate(
                [tiles[g][h][:, q * cch:(q + 1) * cch] for g in range(S5_RELAYOUT_GROUPS)], axis=1)


def _to_chunks(z, g, bt, t):
    L, cch, gt = S5_CHUNK, S5_GROUP, S5_RELAYOUT_GROUPS
    nj = min(S5_RELAYOUT_CHUNKS, t // L)
    n_chunks = t // L
    halves = L * cch // LANES
    return pl.pallas_call(
        _to_chunks_kernel,
        grid=(n_chunks // nj, g // gt),
        in_specs=[pl.BlockSpec((bt, nj * L, LANES), lambda j, lt: (0, j, lt))],
        out_specs=pl.BlockSpec((gt, halves, nj * bt, LANES), lambda j, lt: (lt, 0, j, 0)),
        out_shape=jax.ShapeDtypeStruct((g, halves, n_chunks * bt, LANES), F32),
        compiler_params=_params(("parallel", "parallel")),
        name="s5_to_chunks",
    )(z.reshape(bt, t, -1))


def _from_chunks(y_t, bt, t):
    L, cch, gt = S5_CHUNK, S5_GROUP, S5_RELAYOUT_GROUPS
    nj = min(S5_RELAYOUT_CHUNKS, t // L)
    g, halves = y_t.shape[:2]
    n_chunks = t // L
    return pl.pallas_call(
        _from_chunks_kernel,
        grid=(n_chunks // nj, g // gt),
        in_specs=[pl.BlockSpec((gt, halves, nj * bt, LANES), lambda j, lt: (lt, 0, j, 0))],
        out_specs=pl.BlockSpec((bt, nj * L, LANES), lambda j, lt: (0, j, lt)),
        out_shape=jax.ShapeDtypeStruct((bt, t, g * cch), F32),
        compiler_params=_params(("parallel", "parallel")),
        name="s5_from_chunks",
    )(y_t).reshape(bt * t, g * cch)


def _s5_prompt(z, ops, bt, t):
    toe, p2, q2, a_chunk, d2 = ops
    g = toe.shape[0]
    hp = g // 2
    L = S5_CHUNK
    cch = S5_GROUP
    n_chunks = t // L
    r = n_chunks * bt
    w = L * cch
    half = a_chunk.shape[-1]
    u_t = _to_chunks(z, g, bt, t)
    y_t, f_re, f_im = pl.pallas_call(
        functools.partial(_s5_prompt_kernel, n_chunks=n_chunks, bt=bt),
        grid=(hp,),
        in_specs=[pl.BlockSpec((2, w // LANES, r, LANES), lambda i: (i, 0, 0, 0)),
                  pl.BlockSpec((2, w, w), lambda i: (i, 0, 0)),
                  pl.BlockSpec((1, 2 * w, 2 * half), lambda i: (i, 0, 0)),
                  pl.BlockSpec((1, 2 * half, 2 * w), lambda i: (i, 0, 0)),
                  pl.BlockSpec((1, 2, half), lambda i: (i, 0, 0)),
                  pl.BlockSpec((1, 1, 2 * w), lambda i: (i, 0, 0))],
        out_specs=[pl.BlockSpec((2, w // LANES, r, LANES), lambda i: (i, 0, 0, 0)),
                   pl.BlockSpec((bt, half), lambda i: (0, i)),
                   pl.BlockSpec((bt, half), lambda i: (0, i))],
        out_shape=[jax.ShapeDtypeStruct((g, w // LANES, r, LANES), F32),
                   jax.ShapeDtypeStruct((bt, hp * half), F32),
                   jax.ShapeDtypeStruct((bt, hp * half), F32)],
        scratch_shapes=[pltpu.VMEM((r, 2 * half), F32), pltpu.VMEM((r, 2 * half), F32)],
        compiler_params=_params(("parallel",)),
        name="s5_prompt",
    )(u_t, toe, p2, q2, a_chunk, d2)
    return _from_chunks(y_t, bt, t), f_re, f_im


S5_SAMPLE_GROUPS = 8


def _s5_sample_operators(ab_re, ab_im, bb_re, bb_im, c_re, c_im, d_skip):
    g, n = ab_re.shape
    cch = bb_re.shape[-1]
    gb = S5_SAMPLE_GROUPS
    nb = g // gb
    eye = jnp.eye(gb, dtype=F32)
    b_ri = jnp.stack([bb_re, bb_im], axis=1).reshape(nb, gb, 2, n, cch)
    c_ri = jnp.stack([c_re, -c_im], axis=1).reshape(nb, gb, 2, cch, n)
    b8 = (b_ri.transpose(0, 1, 4, 2, 3)[:, :, :, :, None, :]
          * eye[None, :, None, None, :, None]).reshape(nb, gb * cch, 2 * gb * n)
    c8 = (c_ri.transpose(0, 2, 1, 4, 3)[:, :, :, :, None, :]
          * eye[None, None, :, None, :, None]).reshape(nb, 2 * gb * n, gb * cch)
    a8 = jnp.stack([ab_re.reshape(nb, gb * n), ab_im.reshape(nb, gb * n)], axis=1)
    d8 = d_skip.reshape(nb, 1, gb * cch)
    return b8, c8, a8, d8


def _s5_sample_kernel(u_ref, sr_ref, si_ref, b_ref, c_ref, a_ref, d_ref, y_ref, nr_ref, ni_ref):
    u = u_ref[...]
    half = sr_ref.shape[-1]
    bu = jnp.dot(u, b_ref[0], precision=HIGHEST, preferred_element_type=F32)
    ar = a_ref[0, 0:1, :]
    ai = a_ref[0, 1:2, :]
    sr = sr_ref[...]
    si = si_ref[...]
    xr = ar * sr - ai * si + bu[:, :half]
    xi = ar * si + ai * sr + bu[:, half:]
    nr_ref[...] = xr
    ni_ref[...] = xi
    y = jnp.dot(jnp.concatenate([xr, xi], axis=1), c_ref[0], precision=HIGHEST,
                preferred_element_type=F32) + d_ref[0] * u
    y_ref[...] = jax.nn.gelu(y)


def _s5_sample(z, s_re, s_im, ops):
    b8, c8, a8, d8 = ops
    nb = b8.shape[0]
    b = z.shape[0]
    wu = b8.shape[1]
    ws = a8.shape[-1]
    return pl.pallas_call(
        _s5_sample_kernel,
        grid=(nb,),
        in_specs=[pl.BlockSpec((b, wu), lambda i: (0, i)),
                  pl.BlockSpec((b, ws), lambda i: (0, i)),
                  pl.BlockSpec((b, ws), lambda i: (0, i)),
                  pl.BlockSpec((1, wu, 2 * ws), lambda i: (i, 0, 0)),
                  pl.BlockSpec((1, 2 * ws, wu), lambda i: (i, 0, 0)),
                  pl.BlockSpec((1, 2, ws), lambda i: (i, 0, 0)),
                  pl.BlockSpec((1, 1, wu), lambda i: (i, 0, 0))],
        out_specs=[pl.BlockSpec((b, wu), lambda i: (0, i)),
                   pl.BlockSpec((b, ws), lambda i: (0, i)),
                   pl.BlockSpec((b, ws), lambda i: (0, i))],
        out_shape=[jax.ShapeDtypeStruct((b, nb * wu), F32),
                   jax.ShapeDtypeStruct((b, nb * ws), F32),
                   jax.ShapeDtypeStruct((b, nb * ws), F32)],
        compiler_params=_params(("parallel",)),
        name="s5_sample",
    )(z, s_re, s_im, b8, c8, a8, d8)


HG_HEADS_PER_STEP = 8
HG_SAFE_SPAN = 60.0


def _hgrn_intra_factored(q, kk, v, g):
    c, sub = HG_CHUNK, HG_SUB
    n_sub = c // sub
    row = lax.broadcasted_iota(jnp.int32, (c, 1), 0)
    starts = [g[i * sub - 1:i * sub, :] if i else jnp.zeros_like(g[0:1, :]) for i in range(n_sub)]
    g_start_rows = jnp.concatenate([jnp.broadcast_to(s, (sub, s.shape[1])) for s in starts], axis=0)
    q_fac = (q * jnp.exp(g - g_start_rows)).astype(BF16)
    pad = jnp.zeros((LANES - c, kk.shape[1]), BF16)
    k_fac = []
    for i in range(n_sub):
        ki = kk * jnp.exp(jnp.where(row < (i + 1) * sub, starts[i] - g, -jnp.inf))
        k_fac += [ki.astype(BF16), pad]
    wide = _dot_nt(q_fac, jnp.concatenate(k_fac, axis=0))
    scores = jnp.concatenate([wide[i * sub:(i + 1) * sub, i * LANES:i * LANES + c] for i in range(n_sub)], axis=0)
    causal = lax.broadcasted_iota(jnp.int32, (c, c), 0) >= lax.broadcasted_iota(jnp.int32, (c, c), 1)
    return _dot(jnp.where(causal, scores, 0.0).astype(BF16), v.astype(BF16))


def _hgrn_intra_exact(q, kk, v, g):
    c, sub = HG_CHUNK, HG_SUB
    row = lax.broadcasted_iota(jnp.int32, (c, 1), 0)
    row_in_sub = row % sub
    neg = -jnp.inf
    blocks = [jnp.zeros((sub, c), F32)]
    for i in range(1, c // sub):
        g_start = g[i * sub - 1:i * sub, :]
        qi = q[i * sub:(i + 1) * sub, :] * jnp.exp(g[i * sub:(i + 1) * sub, :] - g_start)
        ki = kk * jnp.exp(jnp.where(row < i * sub, g_start - g, neg))
        blocks.append(_dot_nt(qi.astype(BF16), ki.astype(BF16)))
    o = _dot(jnp.concatenate(blocks, axis=0).astype(BF16), v.astype(BF16))
    o = o + jnp.sum(q * kk, axis=-1, keepdims=True) * v
    for d in range(1, sub):
        gs = pltpu.roll(g, d, 0)
        ks = pltpu.roll(kk, d, 0)
        vs = pltpu.roll(v, d, 0)
        dec = jnp.exp(jnp.where(row_in_sub >= d, g - gs, neg))
        o = o + jnp.sum(q * ks * dec, axis=-1, keepdims=True) * vs
    return o


def _hgrn_prompt_kernel(q_ref, f_ref, i_ref, g_ref, lb_ref, gain_ref, y_ref, sfin_ref, st_sc, inter_sc,
                        *, n_chunks):
    c = HG_CHUNK
    sub = HG_SUB
    hd = HG_HEAD_DIM
    n_h = st_sc.shape[0]
    t = pl.program_id(2)

    @pl.when(t == 0)
    def _():
        st_sc[...] = jnp.zeros_like(st_sc)

    gain = gain_ref[...]
    lb_all = lb_ref[...]
    tri = (lax.broadcasted_iota(jnp.int32, (c, c), 0) >= lax.broadcasted_iota(jnp.int32, (c, c), 1)).astype(BF16)

    def chunk(ci, carry):
        r0 = pl.multiple_of(ci * c, c)

        def forget_and_decay():
            f = lb_all + (1.0 - lb_all) * jax.nn.sigmoid(f_ref[pl.ds(r0, c), :])
            log_f = jnp.log(f)
            hi = log_f.astype(BF16)
            rest = log_f - hi.astype(F32)
            mid = rest.astype(BF16)
            lo = (rest - mid.astype(F32)).astype(BF16)
            sums = _dot(tri, jnp.concatenate([hi, mid, lo], axis=1))
            w = n_h * hd
            return f, (sums[:, 2 * w:] + sums[:, w:2 * w]) + sums[:, :w]

        def operands(hh, f, g):
            lanes = slice(hh * hd, (hh + 1) * hd)
            return _silu(q_ref[pl.ds(r0, c), lanes]), 1.0 - f[:, lanes], i_ref[pl.ds(r0, c), lanes], g[:, lanes]

        def finish(o, hh):
            lanes = slice(hh * hd, (hh + 1) * hd)
            o = o * lax.rsqrt(jnp.mean(o * o, axis=-1, keepdims=True) + RMS_EPS) * gain
            y_ref[pl.ds(r0, c), lanes] = o * _silu(g_ref[pl.ds(r0, c), lanes])

        f_all, g_all = forget_and_decay()
        span = jnp.zeros((1, hd), F32)
        for hh in range(n_h):
            q, kk, v, g = operands(hh, f_all, g_all)
            for i in range(c // sub):
                g_start = g[i * sub - 1:i * sub, :] if i else jnp.zeros_like(g[0:1, :])
                span = jnp.maximum(span, g_start - g[(i + 1) * sub - 1:(i + 1) * sub, :])
            g_last = g[c - 1:c, :]
            st = st_sc[hh]
            inter = _dot_nt((q * jnp.exp(g)).astype(BF16), st.astype(BF16))
            inter_sc[hh] = inter
            finish(inter + _hgrn_intra_factored(q, kk, v, g), hh)
            k_dec = kk * jnp.exp(g_last - g)
            st_sc[hh] = st * jnp.exp(g_last) + _dot_tn(v.astype(BF16), k_dec.astype(BF16))

        @pl.when(jnp.max(span) > HG_SAFE_SPAN)
        def _():
            f_again, g_again = forget_and_decay()
            for hh in range(n_h):
                q, kk, v, g = operands(hh, f_again, g_again)
                finish(inter_sc[hh] + _hgrn_intra_exact(q, kk, v, g), hh)

        return carry

    lax.fori_loop(0, n_chunks, chunk, 0)

    @pl.when(t == pl.num_programs(2) - 1)
    def _():
        for hh in range(n_h):
            sfin_ref[0, hh] = st_sc[hh].T


def _hgrn_prompt(z, lb, gain, bt, t):
    hd = HG_HEAD_DIM
    n_heads = lb.shape[0] // hd
    n_h = HG_HEADS_PER_STEP
    groups = n_heads // n_h
    w = n_h * hd
    tb = min(512, t)
    nt = t // tb
    m = bt * t

    def col(block):
        return pl.BlockSpec((tb, w), lambda b, h, s: (b * nt + s, block * groups + h))

    return pl.pallas_call(
        functools.partial(_hgrn_prompt_kernel, n_chunks=tb // HG_CHUNK),
        grid=(bt, groups, nt),
        in_specs=[col(1), col(2), col(3), col(4),
                  pl.BlockSpec((1, w), lambda b, h, s: (0, h)),
                  pl.BlockSpec((1, hd), lambda b, h, s: (0, 0))],
        out_specs=[pl.BlockSpec((tb, w), lambda b, h, s: (b * nt + s, h)),
                   pl.BlockSpec((1, n_h, hd, hd), lambda b, h, s: (b, h, 0, 0))],
        out_shape=[jax.ShapeDtypeStruct((m, n_heads * hd), F32),
                   jax.ShapeDtypeStruct((bt, n_heads, hd, hd), F32)],
        scratch_shapes=[pltpu.VMEM((n_h, hd, hd), F32), pltpu.VMEM((n_h, HG_CHUNK, hd), F32)],
        compiler_params=_params(("parallel", "parallel", "arbitrary")),
        name="hgrn_prompt",
    )(z, z, z, z, lb.reshape(1, -1), gain.reshape(1, hd))


HG_SAMPLE_KEYS = 32


def _hgrn_sample_kernel(q_ref, f_ref, i_ref, g_ref, lb_ref, gain_ref, s_ref, *rest, own, has_prev):
    y_ref, snew_ref, qt_sc, ft_sc, kt_sc, vt_sc, o_sc = rest[1:] if has_prev else rest
    kb = pl.program_id(1)
    for other in range(snew_ref.shape[0]):
        if other != own:
            snew_ref[other] = jnp.zeros(snew_ref.shape[1:], F32)

    @pl.when(kb == 0)
    def _():
        lb = lb_ref[...]
        f = lb + (1.0 - lb) * jax.nn.sigmoid(f_ref[...])
        qt_sc[...] = _silu(q_ref[...]).T
        ft_sc[...] = f.T
        kt_sc[...] = (1.0 - f).T
        vt_sc[...] = i_ref[...].T
        o_sc[...] = jnp.zeros_like(o_sc)

    vt = vt_sc[...]
    acc = o_sc[...]
    for kl in range(HG_SAMPLE_KEYS):
        k = kb * HG_SAMPLE_KEYS + kl
        st = s_ref[:, 0, kl, :].T
        s_new = st * ft_sc[pl.ds(k, 1), :] + vt * kt_sc[pl.ds(k, 1), :]
        acc = acc + s_new * qt_sc[pl.ds(k, 1), :]
        snew_ref[own, :, 0, kl, :] = s_new.T
    o_sc[...] = acc

    @pl.when(kb == pl.num_programs(1) - 1)
    def _():
        o = acc.T
        o = o * lax.rsqrt(jnp.mean(o * o, axis=-1, keepdims=True) + RMS_EPS) * gain_ref[...]
        y_ref[...] = o * _silu(g_ref[...])


def _hgrn_sample(z, states, layer, lb, gain, new_states=None):
    depth, b, n_heads, hd, _ = states.shape
    kbs = HG_SAMPLE_KEYS
    has_prev = new_states is not None

    def col(block):
        return pl.BlockSpec((b, hd), lambda h, k: (0, block * n_heads + h))

    in_specs = [col(1), col(2), col(3), col(4),
                pl.BlockSpec((1, hd), lambda h, k: (0, h)),
                pl.BlockSpec((1, hd), lambda h, k: (0, 0)),
                pl.BlockSpec((None, b, 1, kbs, hd), lambda h, k: (layer, 0, h, k, 0))]
    operands = [z, z, z, z, lb.reshape(1, -1), gain.reshape(1, hd), states]
    if has_prev:
        in_specs.append(pl.BlockSpec(memory_space=pl.ANY))
        operands.append(new_states)
        out_state = pl.BlockSpec((1, b, 1, kbs, hd), lambda h, k: (layer, 0, h, k, 0))
    else:
        out_state = pl.BlockSpec((depth, b, 1, kbs, hd), lambda h, k: (0, 0, h, k, 0))
    return pl.pallas_call(
        functools.partial(_hgrn_sample_kernel, own=0 if has_prev else layer, has_prev=has_prev),
        grid=(n_heads, hd // kbs),
        in_specs=in_specs,
        out_specs=[pl.BlockSpec((b, hd), lambda h, k: (0, h)), out_state],
        out_shape=[jax.ShapeDtypeStruct((b, n_heads * hd), F32),
                   jax.ShapeDtypeStruct(states.shape, F32)],
        scratch_shapes=[pltpu.VMEM((hd, b), F32)] * 5,
        input_output_aliases={len(operands) - 1: 1} if has_prev else {},
        compiler_params=_params(("parallel", "arbitrary")),
        name="hgrn_sample",
    )(*operands)


def _mix_to_hidden(x, z, y_a_pre, y_b, mw, dense, layer, alpha, tm):
    y_a = _glu(y_a_pre, dense["w_glu"], layer, tm)
    m = x.shape[0]
    merged = _merge(y_a, y_b, z, dense["w_ba"], dense["w_bb"], layer, min(MERGE_ROWS, m))
    return _ln1(merged, x, dense["w_o"], mw["ln1_g"], mw["ln1_b"], layer, alpha, tm)


def _layer_prompt(x, p, mw, lw, layer, alpha, bt, t):
    m = bt * t
    tm = min(512, m)
    dense = mw["dense_bf16"]
    z = _inproj(x, dense["w_in"], layer, min(INPROJ_ROWS, m))
    y_a_pre, f_re, f_im = _s5_prompt(z, lw["s5_prompt_ops"], bt, t)
    y_b, hg_new = _hgrn_prompt(z, lw["lb"], lw["gn_gain"], bt, t)
    h = _mix_to_hidden(x, z, y_a_pre, y_b, mw, dense, layer, alpha, tm)
    s1, s2, cw, counts = _router_sorted(h, lw["w_router"], lw["b_router"], tm)
    pos8, tab = _plan(s1, s2, counts, tm)
    pos = pos8[:2].reshape(2 * m)
    n_tiles = _moe_num_tiles(m)
    tile_expert, tile_rows, n_used = tab[0, :n_tiles], tab[1, :n_tiles], tab[2, :1]
    xs = _dispatch(h, pos, tile_rows, tm)
    ys = _experts_sorted(xs, tile_expert, tile_rows, n_used, mw["w_g"], mw["w_u"], mw["w_d"], layer)
    x_new = _final_gather(h, cw, p, pos, ys, dense["w_pg"], dense["w_pp"], mw["ln2_g"], mw["ln2_b"], layer,
                          alpha, min(tm, 256))
    return x_new, f_re, f_im, hg_new


def _layer_sample(x, p, s_re, s_im, hg_states, hg_new_states, mw, lw, layer, alpha):
    m = x.shape[0]
    dense = mw["dense_f32"]
    z = _inproj(x, dense["w_in"], layer, m)
    y_a_pre, n_re, n_im = _s5_sample(z, s_re, s_im, lw["s5_sample_ops"])
    y_b, hg_new = _hgrn_sample(z, hg_states, layer, lw["lb"], lw["gn_gain"], hg_new_states)
    h = _mix_to_hidden(x, z, y_a_pre, y_b, mw, dense, layer, alpha, m)
    comb = _router_dense(h, lw["w_router"], lw["b_router"], m)
    moe = _moe_dense(h, comb, mw["w_g"], mw["w_u"], mw["w_d"], layer, m)
    x_new = _final(h, moe, p, dense["w_pg"], dense["w_pp"], mw["ln2_g"], mw["ln2_b"], layer, alpha, m)
    return x_new, n_re, n_im, hg_new


def _layer_operands(i, lb_all, s5_lambda_re, s5_lambda_im, s5_log_dt, s5_b_re, s5_b_im, s5_c_re, s5_c_im, s5_d,
                    hg_norm_gain, w_group_router, b_group_router, w_expert_router, b_expert_router):
    disc = _s5_discretise(s5_lambda_re[i], s5_lambda_im[i], s5_log_dt[i], s5_b_re[i], s5_b_im[i])
    d_model = w_group_router.shape[1]
    pad = LANES - N_EXPERTS - N_GROUPS
    w_router = jnp.concatenate([w_expert_router[i], w_group_router[i], jnp.zeros((d_model, pad), F32)], axis=1)
    b_router = jnp.concatenate([b_expert_router[i], b_group_router[i], jnp.zeros((pad,), F32)]).reshape(1, LANES)
    return {
        "s5_width": s5_d.shape[1],
        "s5_prompt_ops": _s5_prompt_operators(*disc, s5_c_re[i], s5_c_im[i], s5_d[i]),
        "s5_sample_ops": _s5_sample_operators(*disc, s5_c_re[i], s5_c_im[i], s5_d[i]),
        "lb": lb_all[i],
        "gn_gain": hg_norm_gain[i],
        "w_router": w_router, "b_router": b_router,
    }


def kernel(x_prompt, x_sample, p_prompt, p_sample, state_s5_re, state_s5_im, state_hgrn, w_in, s5_lambda_re, s5_lambda_im, s5_log_dt, s5_b_re, s5_b_im, s5_c_re, s5_c_im, s5_d, s5_w_glu, hg_lower_bounds, hg_norm_gain, w_branch_a, w_branch_b, w_out, ln1_gain, ln1_bias, w_group_router, b_group_router, w_expert_router, b_expert_router, w_exp_gate, w_exp_up, w_exp_down, w_ple_proj, w_ple_gate, ln2_gain, ln2_bias):
    depth = w_in.shape[0]
    bt, t, d_model = x_prompt.shape
    bs = x_sample.shape[0]
    alpha = (2 * depth) ** 0.25
    n_groups, n_state = s5_lambda_re.shape[1:]

    lb_soft = jax.nn.softmax(hg_lower_bounds.astype(F32), axis=0)
    lb_all = jnp.cumsum(lb_soft, axis=0) - lb_soft[0]

    dense_f32 = {"w_in": w_in, "w_glu": s5_w_glu, "w_ba": w_branch_a, "w_bb": w_branch_b,
                 "w_o": w_out, "w_pp": w_ple_proj, "w_pg": w_ple_gate}
    mw = {"dense_f32": dense_f32,
          "dense_bf16": {k: v.astype(BF16) for k, v in dense_f32.items()},
          "w_g": w_exp_gate, "w_u": w_exp_up, "w_d": w_exp_down,
          "ln1_g": ln1_gain, "ln1_b": ln1_bias, "ln2_g": ln2_gain, "ln2_b": ln2_bias}
    pp = p_prompt.reshape(depth, bt * t, -1)
    ps = p_sample.reshape(depth, bs, -1)

    xp = x_prompt.reshape(bt * t, d_model)
    xs = x_sample.reshape(bs, d_model)
    outs = [[] for _ in range(5)]
    sample_hg = None
    for i in range(depth):
        lw = _layer_operands(i, lb_all, s5_lambda_re, s5_lambda_im, s5_log_dt, s5_b_re, s5_b_im, s5_c_re,
                             s5_c_im, s5_d, hg_norm_gain, w_group_router, b_group_router, w_expert_router,
                             b_expert_router)
        xp, a_re, a_im, a_hg = _layer_prompt(xp, pp, mw, lw, i, alpha, bt, t)
        xs, b_re, b_im, sample_hg = _layer_sample(
            xs, ps, state_s5_re[i].reshape(bs, n_groups * n_state), state_s5_im[i].reshape(bs, n_groups * n_state),
            state_hgrn, sample_hg, mw, lw, i, alpha)
        for lst, val in zip(outs, (a_re.reshape(bt, n_groups, n_state), a_im.reshape(bt, n_groups, n_state), a_hg,
                                   b_re.reshape(bs, n_groups, n_state), b_im.reshape(bs, n_groups, n_state))):
            lst.append(val)

    return (xp.reshape(bt, t, d_model), xs.reshape(bs, 1, d_model), *[jnp.stack(o) for o in outs], sample_hg)
```

```python
import functools
import math

import jax
import jax.numpy as jnp
from jax import lax
from jax.experimental import pallas as pl
from jax.experimental.pallas import tpu as pltpu

F32 = jnp.float32
BF16 = jnp.bfloat16
HIGHEST = lax.Precision.HIGHEST

LANES = 128
SUBLANES = 8
VMEM_LIMIT_BYTES = 56 * 1024 * 1024

S5_GROUP = 16
S5_STATE = 64
S5_CHUNK = 16
HG_HEAD_DIM = 128
HG_CHUNK = 64
HG_SUB = 16
N_GROUPS = 4
EXPERTS_PER_GROUP = 8
N_EXPERTS = N_GROUPS * EXPERTS_PER_GROUP
LN_EPS = 1e-5
RMS_EPS = 1e-6


def _params(semantics):
    return pltpu.CompilerParams(dimension_semantics=semantics, vmem_limit_bytes=VMEM_LIMIT_BYTES)


def _dot(a, b):
    return jnp.dot(a, b, preferred_element_type=F32)


def _mm(a, w):
    if w.dtype == F32:
        a_hi = a.astype(BF16)
        a_lo = (a - a_hi.astype(F32)).astype(BF16)
        w_hi = w.astype(BF16)
        w_lo = (w - w_hi.astype(F32)).astype(BF16)
        return _dot(a_hi, w_hi) + (_dot(a_hi, w_lo) + _dot(a_lo, w_hi))
    return jnp.dot(a.astype(BF16), w, preferred_element_type=F32)


def _dot_nt(a, b):
    return lax.dot_general(a, b, (((1,), (1,)), ((), ())), preferred_element_type=F32)


def _dot_tn(a, b):
    return lax.dot_general(a, b, (((0,), (0,)), ((), ())), preferred_element_type=F32)


def _sigmoid(x):
    return 0.5 * jnp.tanh(0.5 * x) + 0.5


def _silu(x):
    return x * _sigmoid(x)


def _layer_norm_rows(x, gain, bias):
    mu = jnp.mean(x, axis=-1, keepdims=True)
    xc = x - mu
    var = jnp.mean(xc * xc, axis=-1, keepdims=True)
    return xc * lax.rsqrt(var + LN_EPS) * gain + bias


INPROJ_ROWS = 1024


def _inproj_kernel(x_ref, w_ref, o_ref, xb_ref):
    if w_ref.dtype == F32:
        o_ref[...] = _mm(x_ref[...], w_ref[...])
        return

    @pl.when(pl.program_id(1) == 0)
    def _():
        xb_ref[...] = x_ref[...].astype(BF16)

    o_ref[...] = _dot(xb_ref[...], w_ref[...])


def _layer_spec(block, layer, index_map):
    return pl.BlockSpec((None,) + block, lambda *idx: (layer,) + index_map(*idx))


def _inproj(x, w, layer, tm, tn=1024):
    m, k = x.shape
    n = w.shape[-1]
    return pl.pallas_call(
        _inproj_kernel,
        grid=(m // tm, n // tn),
        in_specs=[pl.BlockSpec((tm, k), lambda i, j: (i, 0)),
                  _layer_spec((k, tn), layer, lambda i, j: (0, j))],
        out_specs=pl.BlockSpec((tm, tn), lambda i, j: (i, j)),
        out_shape=jax.ShapeDtypeStruct((m, n), F32),
        scratch_shapes=[pltpu.VMEM((tm, k), BF16)],
        compiler_params=_params(("parallel", "arbitrary")),
        name="inproj",
    )(x, w)


def _glu_kernel(a_ref, w_ref, o_ref):
    a = a_ref[...]
    o_ref[...] = a * _sigmoid(_mm(a, w_ref[...]))


def _glu(a, w, layer, tm):
    m, k = a.shape
    return pl.pallas_call(
        _glu_kernel,
        grid=(m // tm,),
        in_specs=[pl.BlockSpec((tm, k), lambda i: (i, 0)),
                  _layer_spec((k, k), layer, lambda i: (0, 0))],
        out_specs=pl.BlockSpec((tm, k), lambda i: (i, 0)),
        out_shape=jax.ShapeDtypeStruct((m, k), F32),
        compiler_params=_params(("parallel",)),
        name="s5_glu",
    )(a, w)


def _merge_kernel(ya_ref, yb_ref, ga_ref, gb_ref, wa_ref, wb_ref, o_ref):
    pa = _mm(ya_ref[...], wa_ref[...])
    pb = _mm(yb_ref[...], wb_ref[...])
    o_ref[...] = _sigmoid(ga_ref[...]) * pa + _sigmoid(gb_ref[...]) * pb


MERGE_ROWS = 1024
MERGE_COLS = 512


def _merge(ya, yb, z, wa, wb, layer, tm, tn=MERGE_COLS):
    m, k = ya.shape
    n = wa.shape[-1]
    nj = n // tn
    gate_a = 5 * k // tn
    gate_b = gate_a + nj
    return pl.pallas_call(
        _merge_kernel,
        grid=(m // tm, nj),
        in_specs=[pl.BlockSpec((tm, k), lambda i, j: (i, 0)),
                  pl.BlockSpec((tm, k), lambda i, j: (i, 0)),
                  pl.BlockSpec((tm, tn), lambda i, j: (i, gate_a + j)),
                  pl.BlockSpec((tm, tn), lambda i, j: (i, gate_b + j)),
                  _layer_spec((k, tn), layer, lambda i, j: (0, j)),
                  _layer_spec((k, tn), layer, lambda i, j: (0, j))],
        out_specs=pl.BlockSpec((tm, tn), lambda i, j: (i, j)),
        out_shape=jax.ShapeDtypeStruct((m, n), F32),
        compiler_params=_params(("parallel", "arbitrary")),
        name="branch_merge",
    )(ya, yb, z, z, wa, wb)


def _ln1_kernel(mg_ref, x_ref, w_ref, g_ref, b_ref, o_ref, *, alpha):
    r = alpha * x_ref[...] + _mm(mg_ref[...], w_ref[...])
    o_ref[...] = _layer_norm_rows(r, g_ref[...], b_ref[...])


def _ln1(merged, x, w, gain, bias, layer, alpha, tm):
    m, d = x.shape
    row = pl.BlockSpec((tm, d), lambda i: (i, 0))
    vec = _layer_spec((1, d), layer, lambda i: (0, 0))
    return pl.pallas_call(
        functools.partial(_ln1_kernel, alpha=alpha),
        grid=(m // tm,),
        in_specs=[row, row, _layer_spec((d, d), layer, lambda i: (0, 0)), vec, vec],
        out_specs=row,
        out_shape=jax.ShapeDtypeStruct((m, d), F32),
        compiler_params=_params(("parallel",)),
        name="outproj_ln1",
    )(merged, x, w, gain[:, None, :], bias[:, None, :])


def _final_kernel(h_ref, moe_ref, p_ref, wg_ref, wp_ref, g_ref, b_ref, o_ref, *, alpha):
    h = h_ref[...]
    gate = _sigmoid(_mm(h, wg_ref[...]))
    proj = _mm(p_ref[...], wp_ref[...])
    r = alpha * h + moe_ref[...] + gate * proj
    o_ref[...] = _layer_norm_rows(r, g_ref[...], b_ref[...])


def _final(h, moe, p, wg, wp, gain, bias, layer, alpha, tm):
    m, d = h.shape
    dp = p.shape[-1]
    row = pl.BlockSpec((tm, d), lambda i: (i, 0))
    vec = _layer_spec((1, d), layer, lambda i: (0, 0))
    return pl.pallas_call(
        functools.partial(_final_kernel, alpha=alpha),
        grid=(m // tm,),
        in_specs=[row, row, _layer_spec((tm, dp), layer, lambda i: (i, 0)),
                  _layer_spec((d, d), layer, lambda i: (0, 0)),
                  _layer_spec((dp, d), layer, lambda i: (0, 0)), vec, vec],
        out_specs=row,
        out_shape=jax.ShapeDtypeStruct((m, d), F32),
        compiler_params=_params(("parallel",)),
        name="ple_ln2",
    )(h, moe, p, wg, wp, gain[:, None, :], bias[:, None, :])


def _route(h, w, b):
    logits = _mm(h, w) + b
    lane = lax.broadcasted_iota(jnp.int32, logits.shape, 1)
    lane_f = lane.astype(F32)
    neg = -jnp.inf
    gl = jnp.where(lane >= N_EXPERTS, jnp.where(lane < N_EXPERTS + N_GROUPS, logits, neg), neg)
    gmax = jnp.max(gl, axis=-1, keepdims=True)
    denom = jnp.sum(jnp.exp(gl - gmax), axis=-1, keepdims=True)
    grp_p = 1.0 / denom
    gidx = jnp.min(jnp.where(gl == gmax, lane_f, 1e9), axis=-1, keepdims=True) - N_EXPERTS
    lane_grp = (lane // EXPERTS_PER_GROUP).astype(F32)
    el = jnp.where(lane < N_EXPERTS, jnp.where(lane_grp == gidx, logits, neg), neg)
    t1 = jnp.max(el, axis=-1, keepdims=True)
    i1 = jnp.min(jnp.where(el == t1, lane_f, 1e9), axis=-1, keepdims=True)
    el2 = jnp.where(lane_f == i1, neg, el)
    t2 = jnp.max(el2, axis=-1, keepdims=True)
    i2 = jnp.min(jnp.where(el2 == t2, lane_f, 1e9), axis=-1, keepdims=True)
    e2 = jnp.exp(t2 - t1)
    w1 = 1.0 / (1.0 + e2)
    w2 = e2 * w1
    return lane, lane_f, i1, i2, grp_p * w1, grp_p * w2


def _router_dense_kernel(h_ref, w_ref, b_ref, o_ref):
    _, lane_f, i1, i2, c1, c2 = _route(h_ref[...], w_ref[...], b_ref[...])
    o_ref[...] = jnp.where(lane_f == i1, c1, jnp.where(lane_f == i2, c2, 0.0))


def _router_dense(h, w_r, b_r, tm):
    m, d = h.shape
    return pl.pallas_call(
        _router_dense_kernel,
        grid=(m // tm,),
        in_specs=[pl.BlockSpec((tm, d), lambda i: (i, 0)),
                  pl.BlockSpec((d, LANES), lambda i: (0, 0)),
                  pl.BlockSpec((1, LANES), lambda i: (0, 0))],
        out_specs=pl.BlockSpec((tm, LANES), lambda i: (i, 0)),
        out_shape=jax.ShapeDtypeStruct((m, LANES), F32),
        compiler_params=_params(("parallel",)),
        name="router_dense",
    )(h, w_r, b_r)


def _moe_dense_kernel(h_ref, c_ref, wg_ref, wu_ref, wd_ref, o_ref):
    e = pl.program_id(1)

    @pl.when(e == 0)
    def _():
        o_ref[...] = jnp.zeros_like(o_ref)

    comb = c_ref[...]
    lane = lax.broadcasted_iota(jnp.int32, comb.shape, 1)
    col = jnp.sum(jnp.where(lane == e, comb, 0.0), axis=-1, keepdims=True)
    h = h_ref[...]
    hid = _silu(_mm(h, wg_ref[0])) * _mm(h, wu_ref[0]) * col
    o_ref[...] += _mm(hid, wd_ref[0])


def _moe_dense(h, comb, wg, wu, wd, layer, tm):
    m, d = h.shape
    _, n_e, _, f = wg.shape
    return pl.pallas_call(
        _moe_dense_kernel,
        grid=(m // tm, n_e),
        in_specs=[pl.BlockSpec((tm, d), lambda i, e: (i, 0)),
                  pl.BlockSpec((tm, LANES), lambda i, e: (i, 0)),
                  _layer_spec((1, d, f), layer, lambda i, e: (e, 0, 0)),
                  _layer_spec((1, d, f), layer, lambda i, e: (e, 0, 0)),
                  _layer_spec((1, f, d), layer, lambda i, e: (e, 0, 0))],
        out_specs=pl.BlockSpec((tm, d), lambda i, e: (i, 0)),
        out_shape=jax.ShapeDtypeStruct((m, d), F32),
        compiler_params=_params(("parallel", "arbitrary")),
        name="moe_dense",
    )(h, comb, wg, wu, wd)


MOE_TILE = 256


def _moe_num_tiles(m):
    return 2 * m // MOE_TILE + N_EXPERTS


def _router_sorted_kernel(h_ref, w_ref, b_ref, s1_ref, s2_ref, cw_ref, cnt_ref, carry_sc):
    @pl.when(pl.program_id(0) == 0)
    def _():
        carry_sc[...] = jnp.zeros_like(carry_sc)

    lane, lane_f, i1, i2, c1, c2 = _route(h_ref[...], w_ref[...], b_ref[...])
    sel1 = jnp.where(lane_f == i1, 1.0, 0.0)
    sel2 = jnp.where(lane_f == i2, 1.0, 0.0)
    cnt = sel1 + sel2
    tm = cnt.shape[0]
    strict_lower = (lax.broadcasted_iota(jnp.int32, (tm, tm), 0) > lax.broadcasted_iota(jnp.int32, (tm, tm), 1))
    prefix = _dot(strict_lower.astype(BF16), cnt.astype(BF16)) + carry_sc[0:1, :]
    s1_ref[...] = sel1 * (prefix + 1.0)
    s2_ref[...] = sel2 * (prefix + 1.0)
    cw_ref[...] = jnp.where(lane == 0, c1, jnp.where(lane == 1, c2, 0.0))
    carry_sc[...] = carry_sc[...] + jnp.sum(cnt, axis=0, keepdims=True)
    cnt_ref[...] = carry_sc[...]


def _router_sorted(h, w_r, b_r, tm):
    m, d = h.shape
    tok = pl.BlockSpec((tm, LANES), lambda i: (i, 0))
    return pl.pallas_call(
        _router_sorted_kernel,
        grid=(m // tm,),
        in_specs=[pl.BlockSpec((tm, d), lambda i: (i, 0)),
                  pl.BlockSpec((d, LANES), lambda i: (0, 0)),
                  pl.BlockSpec((1, LANES), lambda i: (0, 0))],
        out_specs=[tok, tok, tok, pl.BlockSpec((SUBLANES, LANES), lambda i: (0, 0))],
        out_shape=[jax.ShapeDtypeStruct((m, LANES), F32)] * 3 + [jax.ShapeDtypeStruct((SUBLANES, LANES), F32)],
        scratch_shapes=[pltpu.VMEM((SUBLANES, LANES), F32)],
        compiler_params=_params(("arbitrary",)),
        name="router_sorted",
    )(h, w_r, b_r)


def _plan_kernel(s1_ref, s2_ref, cnt_ref, pos_ref, tab_ref):
    te = float(MOE_TILE)
    lane = lax.broadcasted_iota(jnp.int32, (1, LANES), 1)
    cnt = jnp.where(lane < N_EXPERTS, cnt_ref[0:1, :], 0.0)
    padded = jnp.floor((cnt + (te - 1.0)) * (1.0 / te)) * te
    r128 = lax.broadcasted_iota(jnp.int32, (LANES, LANES), 0)
    c128 = lax.broadcasted_iota(jnp.int32, (LANES, LANES), 1)
    before = jnp.where(r128 < c128, 1.0, 0.0)
    off = jnp.dot(jnp.broadcast_to(padded, (SUBLANES, LANES)), before, precision=HIGHEST,
                  preferred_element_type=F32)[0:1, :]
    s1 = s1_ref[...]
    s2 = s2_ref[...]
    v1 = jnp.where(s1 > 0.0, s1 - 1.0 + off, 0.0)
    v2 = jnp.where(s2 > 0.0, s2 - 1.0 + off, 0.0)
    ones8 = jnp.ones((SUBLANES, LANES), F32)
    p1 = lax.dot_general(ones8, v1, (((1,), (1,)), ((), ())), precision=HIGHEST, preferred_element_type=F32)
    p2 = lax.dot_general(ones8, v2, (((1,), (1,)), ((), ())), precision=HIGHEST, preferred_element_type=F32)
    row8 = lax.broadcasted_iota(jnp.int32, p1.shape, 0)
    pos_ref[...] = jnp.where(row8 == 0, p1, p2).astype(jnp.int32)

    def per_expert(row):
        return jnp.broadcast_to(row, (LANES, LANES)).T
    off_e = per_expert(off)
    end_e = per_expert(off + padded)
    cnt_e = per_expert(cnt)
    start = c128.astype(F32) * te
    is_e = r128 < N_EXPERTS
    tile_e = jnp.sum(jnp.where(is_e, jnp.where(end_e <= start, 1.0, 0.0), 0.0), axis=0, keepdims=True)
    rows = jnp.clip(cnt_e - (start - off_e), 0.0, te)
    owns = jnp.where(is_e, jnp.where(off_e <= start, jnp.where(start < end_e, rows, 0.0), 0.0), 0.0)
    n_rows = jnp.sum(owns, axis=0, keepdims=True)
    tile_e = jnp.minimum(tile_e, float(N_EXPERTS - 1))
    n_used = jnp.sum(jnp.where(n_rows > 0.0, 1.0, 0.0), axis=-1, keepdims=True)
    row_t = lax.broadcasted_iota(jnp.int32, (SUBLANES, LANES), 0)
    tab_ref[...] = jnp.where(row_t == 0, tile_e, jnp.where(row_t == 1, n_rows, n_used)).astype(jnp.int32)


def _plan(s1, s2, counts, tm):
    m = s1.shape[0]
    tok = pl.BlockSpec((tm, LANES), lambda i: (i, 0))
    return pl.pallas_call(
        _plan_kernel,
        grid=(m // tm,),
        in_specs=[tok, tok, pl.BlockSpec((SUBLANES, LANES), lambda i: (0, 0))],
        out_specs=[pl.BlockSpec((SUBLANES, tm), lambda i: (0, i)),
                   pl.BlockSpec((SUBLANES, LANES), lambda i: (0, 0))],
        out_shape=[jax.ShapeDtypeStruct((SUBLANES, m), jnp.int32),
                   jax.ShapeDtypeStruct((SUBLANES, LANES), jnp.int32)],
        compiler_params=_params(("arbitrary",)),
        name="moe_plan",
    )(s1, s2, counts)


DMA_UNROLL = 8
DISPATCH_ROWS = 1024


def _dispatch_kernel(pos_ref, rows_ref, h_ref, xs_hbm, zero_buf, zero_sem, sem, *, m, n_tiles):
    i = pl.program_id(0)
    tm = h_ref.shape[0]

    @pl.when(i == 0)
    def _():
        zero_buf[...] = jnp.zeros_like(zero_buf)

        def zero_copy(j):
            return pltpu.make_async_copy(zero_buf, xs_hbm.at[pl.ds(j * MOE_TILE, MOE_TILE)], zero_sem)

        def zero_start(j, c):
            @pl.when(rows_ref[j] < MOE_TILE)
            def _():
                zero_copy(j).start()
            return c

        def zero_wait(j, c):
            @pl.when(rows_ref[j] < MOE_TILE)
            def _():
                zero_copy(j).wait()
            return c

        lax.fori_loop(0, n_tiles, zero_start, 0)
        lax.fori_loop(0, n_tiles, zero_wait, 0)

    def body(r, c):
        t = i * tm + r
        src = h_ref.at[pl.ds(r, 1)]
        pltpu.make_async_copy(src, xs_hbm.at[pl.ds(pos_ref[t], 1)], sem).start()
        pltpu.make_async_copy(src, xs_hbm.at[pl.ds(pos_ref[m + t], 1)], sem).start()
        return c
    lax.fori_loop(0, tm, body, 0, unroll=DMA_UNROLL)

    pltpu.make_async_copy(xs_hbm.at[pl.ds(0, 2 * tm)], xs_hbm.at[pl.ds(0, 2 * tm)], sem).wait()


def _dispatch(h, pos, tile_rows, tm):
    m, d = h.shape
    n_tiles = _moe_num_tiles(m)
    grid_spec = pltpu.PrefetchScalarGridSpec(
        num_scalar_prefetch=2,
        grid=(m // tm,),
        in_specs=[pl.BlockSpec((tm, d), lambda i, pos, rows: (i, 0))],
        out_specs=pl.BlockSpec(memory_space=pl.ANY),
        scratch_shapes=[pltpu.VMEM((MOE_TILE, d), F32), pltpu.SemaphoreType.DMA(()),
                        pltpu.SemaphoreType.DMA(())],
    )
    return pl.pallas_call(
        functools.partial(_dispatch_kernel, m=m, n_tiles=n_tiles),
        grid_spec=grid_spec,
        out_shape=jax.ShapeDtypeStruct((n_tiles * MOE_TILE, d), F32),
        compiler_params=_params(("arbitrary",)),
        name="moe_dispatch",
    )(pos, tile_rows, h)


def _experts_kernel(te_ref, nr_ref, nu_ref, x_ref, wg_ref, wu_ref, wd_ref, y_ref, wgb, wub, wdb):
    j = pl.program_id(0)
    new_expert = jnp.logical_or(j == 0, te_ref[j] != te_ref[jnp.maximum(j - 1, 0)])

    @pl.when(jnp.logical_and(new_expert, nr_ref[j] > 0))
    def _():
        wgb[...] = wg_ref[0].astype(BF16)
        wub[...] = wu_ref[0].astype(BF16)
        wdb[...] = wd_ref[0].astype(BF16)

    @pl.when(nr_ref[j] > 0)
    def _():
        x = x_ref[...].astype(BF16)
        hid = _silu(_dot(x, wgb[...])) * _dot(x, wub[...])
        y_ref[...] = _dot(hid.astype(BF16), wdb[...])

    @pl.when(nr_ref[j] == 0)
    def _():
        y_ref[...] = jnp.zeros_like(y_ref)


def _experts_sorted(xs, tile_expert, tile_rows, n_used, wg, wu, wd, layer):
    n_rows, d = xs.shape
    _, n_e, _, f = wg.shape
    n_tiles = n_rows // MOE_TILE

    def used_tile(j, te, nr, nu):
        return (jnp.minimum(j, nu[0] - 1), 0)

    def expert_block(j, te, nr, nu):
        return (layer, te[j], 0, 0)

    grid_spec = pltpu.PrefetchScalarGridSpec(
        num_scalar_prefetch=3,
        grid=(n_tiles,),
        in_specs=[pl.BlockSpec((MOE_TILE, d), used_tile),
                  pl.BlockSpec((None, 1, d, f), expert_block),
                  pl.BlockSpec((None, 1, d, f), expert_block),
                  pl.BlockSpec((None, 1, f, d), expert_block)],
        out_specs=pl.BlockSpec((MOE_TILE, d), lambda j, te, nr, nu: (j, 0)),
        scratch_shapes=[pltpu.VMEM((d, f), BF16), pltpu.VMEM((d, f), BF16), pltpu.VMEM((f, d), BF16)],
    )
    return pl.pallas_call(
        _experts_kernel,
        grid_spec=grid_spec,
        out_shape=jax.ShapeDtypeStruct((n_rows, d), F32),
        compiler_params=_params(("arbitrary",)),
        name="moe_experts_sorted",
    )(tile_expert, tile_rows, n_used, xs, wg, wu, wd)


def _final_gather_kernel(pos_ref, h_ref, cw_ref, p_ref, wg_ref, wp_ref, g_ref, b_ref, ys_hbm, o_ref,
                         ybuf, sem, *, alpha, m):
    i = pl.program_id(0)
    slot = i % 2
    tm = h_ref.shape[0]

    last = pl.num_programs(0) - 1

    def start_rows(tile, dst_slot, r):
        t = tile * tm + r
        for pick in range(2):
            pltpu.make_async_copy(ys_hbm.at[pl.ds(pos_ref[pick * m + t], 1)],
                                  ybuf.at[dst_slot, pl.ds(pick * tm + r, 1)], sem.at[dst_slot]).start()

    def wait_tile(s):
        pltpu.make_async_copy(ys_hbm.at[pl.ds(0, 2 * tm)], ybuf.at[s], sem.at[s]).wait()

    @pl.when(i == 0)
    def _():
        def body(r, c):
            start_rows(0, 0, r)
            return c
        lax.fori_loop(0, tm, body, 0, unroll=DMA_UNROLL)

    nxt = jnp.minimum(i + 1, last)
    for r in range(tm):
        start_rows(nxt, 1 - slot, r)

    h = h_ref[...]
    gate = _sigmoid(_mm(h, wg_ref[...]))
    proj = _mm(p_ref[...], wp_ref[...])

    wait_tile(slot)
    cw = cw_ref[...]
    moe = cw[:, 0:1] * ybuf[slot, 0:tm] + cw[:, 1:2] * ybuf[slot, tm:2 * tm]
    r = alpha * h + moe + gate * proj
    o_ref[...] = _layer_norm_rows(r, g_ref[...], b_ref[...])

    @pl.when(i == last)
    def _():
        wait_tile(1 - slot)


def _final_gather(h, cw, p, pos, ys, wg, wp, gain, bias, layer, alpha, tm):
    m, d = h.shape
    dp = p.shape[-1]
    row = pl.BlockSpec((tm, d), lambda i, pos: (i, 0))
    vec = _layer_spec((1, d), layer, lambda i, pos: (0, 0))
    grid_spec = pltpu.PrefetchScalarGridSpec(
        num_scalar_prefetch=1,
        grid=(m // tm,),
        in_specs=[row, pl.BlockSpec((tm, LANES), lambda i, pos: (i, 0)),
                  _layer_spec((tm, dp), layer, lambda i, pos: (i, 0)),
                  _layer_spec((d, d), layer, lambda i, pos: (0, 0)),
                  _layer_spec((dp, d), layer, lambda i, pos: (0, 0)), vec, vec,
                  pl.BlockSpec(memory_space=pl.ANY)],
        out_specs=row,
        scratch_shapes=[pltpu.VMEM((2, 2 * tm, d), F32), pltpu.SemaphoreType.DMA((2,))],
    )
    return pl.pallas_call(
        functools.partial(_final_gather_kernel, alpha=alpha, m=m),
        grid_spec=grid_spec,
        out_shape=jax.ShapeDtypeStruct((m, d), F32),
        compiler_params=_params(("arbitrary",)),
        name="ple_ln2_gather",
    )(pos, h, cw, p, wg, wp, gain[:, None, :], bias[:, None, :], ys)


def _s5_discretise(lam_re, lam_im, log_dt, b_re, b_im):
    dt = jnp.exp(log_dt)[:, None]
    mag = jnp.exp(lam_re * dt)
    ab_re = mag * jnp.cos(lam_im * dt)
    ab_im = mag * jnp.sin(lam_im * dt)
    den = lam_re * lam_re + lam_im * lam_im
    nr = ab_re - 1.0
    zr = (nr * lam_re + ab_im * lam_im) / den
    zi = (ab_im * lam_re - nr * lam_im) / den
    bb_re = zr[..., None] * b_re - zi[..., None] * b_im
    bb_im = zr[..., None] * b_im + zi[..., None] * b_re
    return ab_re, ab_im, bb_re, bb_im


def _s5_prompt_operators(ab_re, ab_im, bb_re, bb_im, c_re, c_im, d_skip):
    g, n = ab_re.shape
    cch = bb_re.shape[-1]
    L = S5_CHUNK
    pw_re = [jnp.ones_like(ab_re)]
    pw_im = [jnp.zeros_like(ab_im)]
    for _ in range(L):
        pr, pi = pw_re[-1], pw_im[-1]
        pw_re.append(pr * ab_re - pi * ab_im)
        pw_im.append(pr * ab_im + pi * ab_re)
    a_re = jnp.stack(pw_re)
    a_im = jnp.stack(pw_im)
    w_re = a_re[:L, :, :, None] * bb_re - a_im[:L, :, :, None] * bb_im
    w_im = a_re[:L, :, :, None] * bb_im + a_im[:L, :, :, None] * bb_re
    kern = (jnp.einsum('gcn,kgnd->kgcd', c_re, w_re, precision=HIGHEST)
            - jnp.einsum('gcn,kgnd->kgcd', c_im, w_im, precision=HIGHEST))
    steps = jnp.arange(L)
    lag_is = (steps[None, None, :] - steps[None, :, None] == steps[:, None, None]).astype(F32)
    toe = jnp.einsum('kio,kgcd->gidoc', lag_is, kern, precision=HIGHEST)
    toe = toe.reshape(g, L * cch, L * cch)

    p_re = w_re[::-1].transpose(1, 0, 3, 2).reshape(g, L * cch, n)
    p_im = w_im[::-1].transpose(1, 0, 3, 2).reshape(g, L * cch, n)
    q_re = (jnp.einsum('gcn,tgn->gntc', c_re, a_re[1:]) - jnp.einsum('gcn,tgn->gntc', c_im, a_im[1:]))
    q_im = -(jnp.einsum('gcn,tgn->gntc', c_re, a_im[1:]) + jnp.einsum('gcn,tgn->gntc', c_im, a_re[1:]))
    q_re = q_re.reshape(g, n, L * cch)
    q_im = q_im.reshape(g, n, L * cch)

    hp = g // 2
    w = L * cch
    pe_re, po_re, pe_im, po_im = (x.astype(BF16) for x in (p_re[0::2], p_re[1::2], p_im[0::2], p_im[1::2]))
    qe_re, qo_re, qe_im, qo_im = (x.astype(BF16) for x in (q_re[0::2], q_re[1::2], q_im[0::2], q_im[1::2]))
    zp = jnp.zeros_like(pe_re)
    zq = jnp.zeros_like(qe_re)
    p2 = jnp.concatenate([jnp.concatenate([pe_re, zp, pe_im, zp], axis=2),
                          jnp.concatenate([zp, po_re, zp, po_im], axis=2)], axis=1)
    q2 = jnp.concatenate([jnp.concatenate([qe_re, zq], axis=2), jnp.concatenate([zq, qo_re], axis=2),
                          jnp.concatenate([qe_im, zq], axis=2), jnp.concatenate([zq, qo_im], axis=2)], axis=1)
    a_chunk = jnp.stack([a_re[L].reshape(hp, 2 * n), a_im[L].reshape(hp, 2 * n)], axis=1)
    d2 = jnp.broadcast_to(d_skip.reshape(hp, 2, 1, cch), (hp, 2, L, cch)).reshape(hp, 1, 2 * w)
    return toe.astype(BF16), p2, q2, a_chunk, d2


def _s5_prompt_kernel(u_ref, t_ref, p_ref, q_ref, a_ref, d_ref, y_ref, fre_ref, fim_ref,
                      s_sc, xin_sc, *, n_chunks, bt):
    half = a_ref.shape[-1]
    u0 = jnp.concatenate([u_ref[0, 0], u_ref[0, 1]], axis=1)
    u1 = jnp.concatenate([u_ref[1, 0], u_ref[1, 1]], axis=1)
    ub0 = u0.astype(BF16)
    ub1 = u1.astype(BF16)
    y_intra = jnp.concatenate([_dot(ub0, t_ref[0]), _dot(ub1, t_ref[1])], axis=1)
    s_sc[...] = _dot(jnp.concatenate([ub0, ub1], axis=1), p_ref[0])
    ar = a_ref[0, 0:1, :]
    ai = a_ref[0, 1:2, :]
    xr = jnp.zeros((bt, half), F32)
    xi = jnp.zeros((bt, half), F32)
    for j in range(n_chunks):
        rows = slice(j * bt, (j + 1) * bt)
        xin_sc[rows, 0:half] = xr
        xin_sc[rows, half:2 * half] = xi
        sr = s_sc[rows, 0:half]
        si = s_sc[rows, half:2 * half]
        xr, xi = ar * xr - ai * xi + sr, ar * xi + ai * xr + si
    fre_ref[...] = xr
    fim_ref[...] = xi
    y_carry = _dot(xin_sc[...].astype(BF16), q_ref[0])
    y = y_intra + y_carry + d_ref[0] * jnp.concatenate([u0, u1], axis=1)
    y = jax.nn.gelu(y)
    for k in range(4):
        y_ref[k // 2, k % 2] = y[:, k * LANES:(k + 1) * LANES]


S5_RELAYOUT_CHUNKS = 32
S5_RELAYOUT_GROUPS = LANES // S5_GROUP


def _to_chunks_kernel(x_ref, o_ref):
    L, cch = S5_CHUNK, S5_GROUP
    bt = x_ref.shape[0]
    nj = x_ref.shape[1] // L
    per_tile = LANES // cch
    for b in range(bt):
        at_pos = [x_ref[b, pl.ds(p, nj, stride=L), :] for p in range(L)]
        for g in range(S5_RELAYOUT_GROUPS):
            for h in range(L // per_tile):
                piece = jnp.concatenate(
                    [at_pos[h * per_tile + p][:, g * cch:(g + 1) * cch] for p in range(per_tile)], axis=1)
                o_ref[g, h, pl.ds(b, nj, stride=bt), :] = piece


def _from_chunks_kernel(y_ref, o_ref):
    L, cch = S5_CHUNK, S5_GROUP
    bt = o_ref.shape[0]
    nj = o_ref.shape[1] // L
    per_tile = LANES // cch
    for b in range(bt):
        tiles = [[y_ref[g, h, pl.ds(b, nj, stride=bt), :] for h in range(L // per_tile)]
                 for g in range(S5_RELAYOUT_GROUPS)]
        for p in range(L):
            h, q = divmod(p, per_tile)
            o_ref[b, pl.ds(p, nj, stride=L), :] = jnp.concatenate(
                [tiles[g][h][:, q * cch:(q + 1) * cch] for g in range(S5_RELAYOUT_GROUPS)], axis=1)


def _to_chunks(z, g, bt, t):
    L, cch, gt = S5_CHUNK, S5_GROUP, S5_RELAYOUT_GROUPS
    nj = min(S5_RELAYOUT_CHUNKS, t // L)
    n_chunks = t // L
    halves = L * cch // LANES
    return pl.pallas_call(
        _to_chunks_kernel,
        grid=(n_chunks // nj, g // gt),
        in_specs=[pl.BlockSpec((bt, nj * L, LANES), lambda j, lt: (0, j, lt))],
        out_specs=pl.BlockSpec((gt, halves, nj * bt, LANES), lambda j, lt: (lt, 0, j, 0)),
        out_shape=jax.ShapeDtypeStruct((g, halves, n_chunks * bt, LANES), F32),
        compiler_params=_params(("parallel", "parallel")),
        name="s5_to_chunks",
    )(z.reshape(bt, t, -1))


def _from_chunks(y_t, bt, t):
    L, cch, gt = S5_CHUNK, S5_GROUP, S5_RELAYOUT_GROUPS
    nj = min(S5_RELAYOUT_CHUNKS, t // L)
    g, halves = y_t.shape[:2]
    n_chunks = t // L
    return pl.pallas_call(
        _from_chunks_kernel,
        grid=(n_chunks // nj, g // gt),
        in_specs=[pl.BlockSpec((gt, halves, nj * bt, LANES), lambda j, lt: (lt, 0, j, 0))],
        out_specs=pl.BlockSpec((bt, nj * L, LANES), lambda j, lt: (0, j, lt)),
        out_shape=jax.ShapeDtypeStruct((bt, t, g * cch), F32),
        compiler_params=_params(("parallel", "parallel")),
        name="s5_from_chunks",
    )(y_t).reshape(bt * t, g * cch)


def _s5_prompt(z, ops, bt, t):
    toe, p2, q2, a_chunk, d2 = ops
    g = toe.shape[0]
    hp = g // 2
    L = S5_CHUNK
    cch = S5_GROUP
    n_chunks = t // L
    r = n_chunks * bt
    w = L * cch
    half = a_chunk.shape[-1]
    u_t = _to_chunks(z, g, bt, t)
    y_t, f_re, f_im = pl.pallas_call(
        functools.partial(_s5_prompt_kernel, n_chunks=n_chunks, bt=bt),
        grid=(hp,),
        in_specs=[pl.BlockSpec((2, w // LANES, r, LANES), lambda i: (i, 0, 0, 0)),
                  pl.BlockSpec((2, w, w), lambda i: (i, 0, 0)),
                  pl.BlockSpec((1, 2 * w, 2 * half), lambda i: (i, 0, 0)),
                  pl.BlockSpec((1, 2 * half, 2 * w), lambda i: (i, 0, 0)),
                  pl.BlockSpec((1, 2, half), lambda i: (i, 0, 0)),
                  pl.BlockSpec((1, 1, 2 * w), lambda i: (i, 0, 0))],
        out_specs=[pl.BlockSpec((2, w // LANES, r, LANES), lambda i: (i, 0, 0, 0)),
                   pl.BlockSpec((bt, half), lambda i: (0, i)),
                   pl.BlockSpec((bt, half), lambda i: (0, i))],
        out_shape=[jax.ShapeDtypeStruct((g, w // LANES, r, LANES), F32),
                   jax.ShapeDtypeStruct((bt, hp * half), F32),
                   jax.ShapeDtypeStruct((bt, hp * half), F32)],
        scratch_shapes=[pltpu.VMEM((r, 2 * half), F32), pltpu.VMEM((r, 2 * half), F32)],
        compiler_params=_params(("parallel",)),
        name="s5_prompt",
    )(u_t, toe, p2, q2, a_chunk, d2)
    return _from_chunks(y_t, bt, t), f_re, f_im


S5_SAMPLE_GROUPS = 8


def _s5_sample_operators(ab_re, ab_im, bb_re, bb_im, c_re, c_im, d_skip):
    g, n = ab_re.shape
    cch = bb_re.shape[-1]
    gb = S5_SAMPLE_GROUPS
    nb = g // gb
    eye = jnp.eye(gb, dtype=F32)
    b_ri = jnp.stack([bb_re, bb_im], axis=1).reshape(nb, gb, 2, n, cch)
    c_ri = jnp.stack([c_re, -c_im], axis=1).reshape(nb, gb, 2, cch, n)
    b8 = (b_ri.transpose(0, 1, 4, 2, 3)[:, :, :, :, None, :]
          * eye[None, :, None, None, :, None]).reshape(nb, gb * cch, 2 * gb * n)
    c8 = (c_ri.transpose(0, 2, 1, 4, 3)[:, :, :, :, None, :]
          * eye[None, None, :, None, :, None]).reshape(nb, 2 * gb * n, gb * cch)
    a8 = jnp.stack([ab_re.reshape(nb, gb * n), ab_im.reshape(nb, gb * n)], axis=1)
    d8 = d_skip.reshape(nb, 1, gb * cch)
    return b8, c8, a8, d8


def _s5_sample_kernel(u_ref, sr_ref, si_ref, b_ref, c_ref, a_ref, d_ref, y_ref, nr_ref, ni_ref):
    u = u_ref[...]
    half = sr_ref.shape[-1]
    bu = jnp.dot(u, b_ref[0], precision=HIGHEST, preferred_element_type=F32)
    ar = a_ref[0, 0:1, :]
    ai = a_ref[0, 1:2, :]
    sr = sr_ref[...]
    si = si_ref[...]
    xr = ar * sr - ai * si + bu[:, :half]
    xi = ar * si + ai * sr + bu[:, half:]
    nr_ref[...] = xr
    ni_ref[...] = xi
    y = jnp.dot(jnp.concatenate([xr, xi], axis=1), c_ref[0], precision=HIGHEST,
                preferred_element_type=F32) + d_ref[0] * u
    y_ref[...] = jax.nn.gelu(y)


def _s5_sample(z, s_re, s_im, ops):
    b8, c8, a8, d8 = ops
    nb = b8.shape[0]
    b = z.shape[0]
    wu = b8.shape[1]
    ws = a8.shape[-1]
    return pl.pallas_call(
        _s5_sample_kernel,
        grid=(nb,),
        in_specs=[pl.BlockSpec((b, wu), lambda i: (0, i)),
                  pl.BlockSpec((b, ws), lambda i: (0, i)),
                  pl.BlockSpec((b, ws), lambda i: (0, i)),
                  pl.BlockSpec((1, wu, 2 * ws), lambda i: (i, 0, 0)),
                  pl.BlockSpec((1, 2 * ws, wu), lambda i: (i, 0, 0)),
                  pl.BlockSpec((1, 2, ws), lambda i: (i, 0, 0)),
                  pl.BlockSpec((1, 1, wu), lambda i: (i, 0, 0))],
        out_specs=[pl.BlockSpec((b, wu), lambda i: (0, i)),
                   pl.BlockSpec((b, ws), lambda i: (0, i)),
                   pl.BlockSpec((b, ws), lambda i: (0, i))],
        out_shape=[jax.ShapeDtypeStruct((b, nb * wu), F32),
                   jax.ShapeDtypeStruct((b, nb * ws), F32),
                   jax.ShapeDtypeStruct((b, nb * ws), F32)],
        compiler_params=_params(("parallel",)),
        name="s5_sample",
    )(z, s_re, s_im, b8, c8, a8, d8)


HG_HEADS_PER_STEP = 8
HG_SAFE_SPAN = 60.0


def _hgrn_intra_factored(q, kk, v, g):
    c, sub = HG_CHUNK, HG_SUB
    n_sub = c // sub
    row = lax.broadcasted_iota(jnp.int32, (c, 1), 0)
    starts = [g[i * sub - 1:i * sub, :] if i else jnp.zeros_like(g[0:1, :]) for i in range(n_sub)]
    g_start_rows = jnp.concatenate([jnp.broadcast_to(s, (sub, s.shape[1])) for s in starts], axis=0)
    q_fac = (q * jnp.exp(g - g_start_rows)).astype(BF16)
    pad = jnp.zeros((LANES - c, kk.shape[1]), BF16)
    k_fac = []
    for i in range(n_sub):
        ki = kk * jnp.exp(jnp.where(row < (i + 1) * sub, starts[i] - g, -jnp.inf))
        k_fac += [ki.astype(BF16), pad]
    wide = _dot_nt(q_fac, jnp.concatenate(k_fac, axis=0))
    scores = jnp.concatenate([wide[i * sub:(i + 1) * sub, i * LANES:i * LANES + c] for i in range(n_sub)], axis=0)
    causal = lax.broadcasted_iota(jnp.int32, (c, c), 0) >= lax.broadcasted_iota(jnp.int32, (c, c), 1)
    return _dot(jnp.where(causal, scores, 0.0).astype(BF16), v.astype(BF16))


def _hgrn_intra_exact(q, kk, v, g):
    c, sub = HG_CHUNK, HG_SUB
    row = lax.broadcasted_iota(jnp.int32, (c, 1), 0)
    row_in_sub = row % sub
    neg = -jnp.inf
    blocks = [jnp.zeros((sub, c), F32)]
    for i in range(1, c // sub):
        g_start = g[i * sub - 1:i * sub, :]
        qi = q[i * sub:(i + 1) * sub, :] * jnp.exp(g[i * sub:(i + 1) * sub, :] - g_start)
        ki = kk * jnp.exp(jnp.where(row < i * sub, g_start - g, neg))
        blocks.append(_dot_nt(qi.astype(BF16), ki.astype(BF16)))
    o = _dot(jnp.concatenate(blocks, axis=0).astype(BF16), v.astype(BF16))
    o = o + jnp.sum(q * kk, axis=-1, keepdims=True) * v
    for d in range(1, sub):
        gs = pltpu.roll(g, d, 0)
        ks = pltpu.roll(kk, d, 0)
        vs = pltpu.roll(v, d, 0)
        dec = jnp.exp(jnp.where(row_in_sub >= d, g - gs, neg))
        o = o + jnp.sum(q * ks * dec, axis=-1, keepdims=True) * vs
    return o


def _hgrn_prompt_kernel(q_ref, f_ref, i_ref, g_ref, lb_ref, gain_ref, y_ref, sfin_ref, st_sc, inter_sc,
                        *, n_chunks):
    c = HG_CHUNK
    sub = HG_SUB
    hd = HG_HEAD_DIM
    n_h = st_sc.shape[0]
    t = pl.program_id(2)

    @pl.when(t == 0)
    def _():
        st_sc[...] = jnp.zeros_like(st_sc)

    gain = gain_ref[...]
    lb_all = lb_ref[...]
    tri = (lax.broadcasted_iota(jnp.int32, (c, c), 0) >= lax.broadcasted_iota(jnp.int32, (c, c), 1)).astype(BF16)

    def chunk(ci, carry):
        r0 = pl.multiple_of(ci * c, c)

        def forget_and_decay():
            f = lb_all + (1.0 - lb_all) * jax.nn.sigmoid(f_ref[pl.ds(r0, c), :])
            log_f = jnp.log(f)
            hi = log_f.astype(BF16)
            rest = log_f - hi.astype(F32)
            mid = rest.astype(BF16)
            lo = (rest - mid.astype(F32)).astype(BF16)
            sums = _dot(tri, jnp.concatenate([hi, mid, lo], axis=1))
            w = n_h * hd
            return f, (sums[:, 2 * w:] + sums[:, w:2 * w]) + sums[:, :w]

        def operands(hh, f, g):
            lanes = slice(hh * hd, (hh + 1) * hd)
            return _silu(q_ref[pl.ds(r0, c), lanes]), 1.0 - f[:, lanes], i_ref[pl.ds(r0, c), lanes], g[:, lanes]

        def finish(o, hh):
            lanes = slice(hh * hd, (hh + 1) * hd)
            o = o * lax.rsqrt(jnp.mean(o * o, axis=-1, keepdims=True) + RMS_EPS) * gain
            y_ref[pl.ds(r0, c), lanes] = o * _silu(g_ref[pl.ds(r0, c), lanes])

        f_all, g_all = forget_and_decay()
        span = jnp.zeros((1, hd), F32)
        for hh in range(n_h):
            q, kk, v, g = operands(hh, f_all, g_all)
            for i in range(c // sub):
                g_start = g[i * sub - 1:i * sub, :] if i else jnp.zeros_like(g[0:1, :])
                span = jnp.maximum(span, g_start - g[(i + 1) * sub - 1:(i + 1) * sub, :])
            g_last = g[c - 1:c, :]
            st = st_sc[hh]
            inter = _dot_nt((q * jnp.exp(g)).astype(BF16), st.astype(BF16))
            inter_sc[hh] = inter
            finish(inter + _hgrn_intra_factored(q, kk, v, g), hh)
            k_dec = kk * jnp.exp(g_last - g)
            st_sc[hh] = st * jnp.exp(g_last) + _dot_tn(v.astype(BF16), k_dec.astype(BF16))

        @pl.when(jnp.max(span) > HG_SAFE_SPAN)
        def _():
            f_again, g_again = forget_and_decay()
            for hh in range(n_h):
                q, kk, v, g = operands(hh, f_again, g_again)
                finish(inter_sc[hh] + _hgrn_intra_exact(q, kk, v, g), hh)

        return carry

    lax.fori_loop(0, n_chunks, chunk, 0)

    @pl.when(t == pl.num_programs(2) - 1)
    def _():
        for hh in range(n_h):
            sfin_ref[0, hh] = st_sc[hh].T


def _hgrn_prompt(z, lb, gain, bt, t):
    hd = HG_HEAD_DIM
    n_heads = lb.shape[0] // hd
    n_h = HG_HEADS_PER_STEP
    groups = n_heads // n_h
    w = n_h * hd
    tb = min(512, t)
    nt = t // tb
    m = bt * t

    def col(block):
        return pl.BlockSpec((tb, w), lambda b, h, s: (b * nt + s, block * groups + h))

    return pl.pallas_call(
        functools.partial(_hgrn_prompt_kernel, n_chunks=tb // HG_CHUNK),
        grid=(bt, groups, nt),
        in_specs=[col(1), col(2), col(3), col(4),
                  pl.BlockSpec((1, w), lambda b, h, s: (0, h)),
                  pl.BlockSpec((1, hd), lambda b, h, s: (0, 0))],
        out_specs=[pl.BlockSpec((tb, w), lambda b, h, s: (b * nt + s, h)),
                   pl.BlockSpec((1, n_h, hd, hd), lambda b, h, s: (b, h, 0, 0))],
        out_shape=[jax.ShapeDtypeStruct((m, n_heads * hd), F32),
                   jax.ShapeDtypeStruct((bt, n_heads, hd, hd), F32)],
        scratch_shapes=[pltpu.VMEM((n_h, hd, hd), F32), pltpu.VMEM((n_h, HG_CHUNK, hd), F32)],
        compiler_params=_params(("parallel", "parallel", "arbitrary")),
        name="hgrn_prompt",
    )(z, z, z, z, lb.reshape(1, -1), gain.reshape(1, hd))


HG_SAMPLE_KEYS = 64


def _hgrn_sample_kernel(q_ref, f_ref, i_ref, g_ref, lb_ref, gain_ref, s_ref, *rest, own, has_prev):
    y_ref, snew_ref, qt_sc, ft_sc, kt_sc, vt_sc, o_sc = rest[1:] if has_prev else rest
    kb = pl.program_id(1)
    for other in range(snew_ref.shape[0]):
        if other != own:
            snew_ref[other] = jnp.zeros(snew_ref.shape[1:], F32)

    @pl.when(kb == 0)
    def _():
        lb = lb_ref[...]
        f = lb + (1.0 - lb) * jax.nn.sigmoid(f_ref[...])
        qt_sc[...] = _silu(q_ref[...]).T
        ft_sc[...] = f.T
        kt_sc[...] = (1.0 - f).T
        vt_sc[...] = i_ref[...].T
        o_sc[...] = jnp.zeros_like(o_sc)

    vt = vt_sc[...]
    acc = o_sc[...]
    for kl in range(HG_SAMPLE_KEYS):
        k = kb * HG_SAMPLE_KEYS + kl
        st = s_ref[:, 0, kl, :].T
        s_new = st * ft_sc[pl.ds(k, 1), :] + vt * kt_sc[pl.ds(k, 1), :]
        acc = acc + s_new * qt_sc[pl.ds(k, 1), :]
        snew_ref[own, :, 0, kl, :] = s_new.T
    o_sc[...] = acc

    @pl.when(kb == pl.num_programs(1) - 1)
    def _():
        o = acc.T
        o = o * lax.rsqrt(jnp.mean(o * o, axis=-1, keepdims=True) + RMS_EPS) * gain_ref[...]
        y_ref[...] = o * _silu(g_ref[...])


def _hgrn_sample(z, states, layer, lb, gain, new_states=None):
    depth, b, n_heads, hd, _ = states.shape
    kbs = HG_SAMPLE_KEYS
    has_prev = new_states is not None

    def col(block):
        return pl.BlockSpec((b, hd), lambda h, k: (0, block * n_heads + h))

    in_specs = [col(1), col(2), col(3), col(4),
                pl.BlockSpec((1, hd), lambda h, k: (0, h)),
                pl.BlockSpec((1, hd), lambda h, k: (0, 0)),
                pl.BlockSpec((None, b, 1, kbs, hd), lambda h, k: (layer, 0, h, k, 0))]
    operands = [z, z, z, z, lb.reshape(1, -1), gain.reshape(1, hd), states]
    if has_prev:
        in_specs.append(pl.BlockSpec(memory_space=pl.ANY))
        operands.append(new_states)
        out_state = pl.BlockSpec((1, b, 1, kbs, hd), lambda h, k: (layer, 0, h, k, 0))
    else:
        out_state = pl.BlockSpec((depth, b, 1, kbs, hd), lambda h, k: (0, 0, h, k, 0))
    return pl.pallas_call(
        functools.partial(_hgrn_sample_kernel, own=0 if has_prev else layer, has_prev=has_prev),
        grid=(n_heads, hd // kbs),
        in_specs=in_specs,
        out_specs=[pl.BlockSpec((b, hd), lambda h, k: (0, h)), out_state],
        out_shape=[jax.ShapeDtypeStruct((b, n_heads * hd), F32),
                   jax.ShapeDtypeStruct(states.shape, F32)],
        scratch_shapes=[pltpu.VMEM((hd, b), F32)] * 5,
        input_output_aliases={len(operands) - 1: 1} if has_prev else {},
        compiler_params=_params(("parallel", "arbitrary")),
        name="hgrn_sample",
    )(*operands)


def _mix_to_hidden(x, z, y_a_pre, y_b, mw, dense, layer, alpha, tm):
    y_a = _glu(y_a_pre, dense["w_glu"], layer, tm)
    m = x.shape[0]
    merged = _merge(y_a, y_b, z, dense["w_ba"], dense["w_bb"], layer, min(MERGE_ROWS, m))
    return _ln1(merged, x, dense["w_o"], mw["ln1_g"], mw["ln1_b"], layer, alpha, tm)


def _layer_prompt(x, p, mw, lw, layer, alpha, bt, t):
    m = bt * t
    tm = min(512, m)
    dense = mw["dense_bf16"]
    z = _inproj(x, dense["w_in"], layer, min(INPROJ_ROWS, m))
    y_a_pre, f_re, f_im = _s5_prompt(z, lw["s5_prompt_ops"], bt, t)
    y_b, hg_new = _hgrn_prompt(z, lw["lb"], lw["gn_gain"], bt, t)
    h = _mix_to_hidden(x, z, y_a_pre, y_b, mw, dense, layer, alpha, tm)
    s1, s2, cw, counts = _router_sorted(h, lw["w_router"], lw["b_router"], tm)
    pos8, tab = _plan(s1, s2, counts, tm)
    pos = pos8[:2].reshape(2 * m)
    n_tiles = _moe_num_tiles(m)
    tile_expert, tile_rows, n_used = tab[0, :n_tiles], tab[1, :n_tiles], tab[2, :1]
    xs = _dispatch(h, pos, tile_rows, min(DISPATCH_ROWS, m))
    ys = _experts_sorted(xs, tile_expert, tile_rows, n_used, mw["w_g"], mw["w_u"], mw["w_d"], layer)
    x_new = _final_gather(h, cw, p, pos, ys, dense["w_pg"], dense["w_pp"], mw["ln2_g"], mw["ln2_b"], layer,
                          alpha, min(tm, 256))
    return x_new, f_re, f_im, hg_new


def _layer_sample(x, p, s_re, s_im, hg_states, hg_new_states, mw, lw, layer, alpha):
    m = x.shape[0]
    dense = mw["dense_f32"]
    z = _inproj(x, dense["w_in"], layer, m)
    y_a_pre, n_re, n_im = _s5_sample(z, s_re, s_im, lw["s5_sample_ops"])
    y_b, hg_new = _hgrn_sample(z, hg_states, layer, lw["lb"], lw["gn_gain"], hg_new_states)
    h = _mix_to_hidden(x, z, y_a_pre, y_b, mw, dense, layer, alpha, m)
    comb = _router_dense(h, lw["w_router"], lw["b_router"], m)
    moe = _moe_dense(h, comb, mw["w_g"], mw["w_u"], mw["w_d"], layer, m)
    x_new = _final(h, moe, p, dense["w_pg"], dense["w_pp"], mw["ln2_g"], mw["ln2_b"], layer, alpha, m)
    return x_new, n_re, n_im, hg_new


def _layer_operands(i, lb_all, s5_lambda_re, s5_lambda_im, s5_log_dt, s5_b_re, s5_b_im, s5_c_re, s5_c_im, s5_d,
                    hg_norm_gain, w_group_router, b_group_router, w_expert_router, b_expert_router):
    disc = _s5_discretise(s5_lambda_re[i], s5_lambda_im[i], s5_log_dt[i], s5_b_re[i], s5_b_im[i])
    d_model = w_group_router.shape[1]
    pad = LANES - N_EXPERTS - N_GROUPS
    w_router = jnp.concatenate([w_expert_router[i], w_group_router[i], jnp.zeros((d_model, pad), F32)], axis=1)
    b_router = jnp.concatenate([b_expert_router[i], b_group_router[i], jnp.zeros((pad,), F32)]).reshape(1, LANES)
    return {
        "s5_width": s5_d.shape[1],
        "s5_prompt_ops": _s5_prompt_operators(*disc, s5_c_re[i], s5_c_im[i], s5_d[i]),
        "s5_sample_ops": _s5_sample_operators(*disc, s5_c_re[i], s5_c_im[i], s5_d[i]),
        "lb": lb_all[i],
        "gn_gain": hg_norm_gain[i],
        "w_router": w_router, "b_router": b_router,
    }


def kernel(x_prompt, x_sample, p_prompt, p_sample, state_s5_re, state_s5_im, state_hgrn, w_in, s5_lambda_re, s5_lambda_im, s5_log_dt, s5_b_re, s5_b_im, s5_c_re, s5_c_im, s5_d, s5_w_glu, hg_lower_bounds, hg_norm_gain, w_branch_a, w_branch_b, w_out, ln1_gain, ln1_bias, w_group_router, b_group_router, w_expert_router, b_expert_router, w_exp_gate, w_exp_up, w_exp_down, w_ple_proj, w_ple_gate, ln2_gain, ln2_bias):
    depth = w_in.shape[0]
    bt, t, d_model = x_prompt.shape
    bs = x_sample.shape[0]
    alpha = (2 * depth) ** 0.25
    n_groups, n_state = s5_lambda_re.shape[1:]

    lb_soft = jax.nn.softmax(hg_lower_bounds.astype(F32), axis=0)
    lb_all = jnp.cumsum(lb_soft, axis=0) - lb_soft[0]

    dense_f32 = {"w_in": w_in, "w_glu": s5_w_glu, "w_ba": w_branch_a, "w_bb": w_branch_b,
                 "w_o": w_out, "w_pp": w_ple_proj, "w_pg": w_ple_gate}
    mw = {"dense_f32": dense_f32,
          "dense_bf16": {k: v.astype(BF16) for k, v in dense_f32.items()},
          "w_g": w_exp_gate, "w_u": w_exp_up, "w_d": w_exp_down,
          "ln1_g": ln1_gain, "ln1_b": ln1_bias, "ln2_g": ln2_gain, "ln2_b": ln2_bias}
    pp = p_prompt.reshape(depth, bt * t, -1)
    ps = p_sample.reshape(depth, bs, -1)

    xp = x_prompt.reshape(bt * t, d_model)
    xs = x_sample.reshape(bs, d_model)
    outs = [[] for _ in range(5)]
    sample_hg = None
    for i in range(depth):
        lw = _layer_operands(i, lb_all, s5_lambda_re, s5_lambda_im, s5_log_dt, s5_b_re, s5_b_im, s5_c_re,
                             s5_c_im, s5_d, hg_norm_gain, w_group_router, b_group_router, w_expert_router,
                             b_expert_router)
        xp, a_re, a_im, a_hg = _layer_prompt(xp, pp, mw, lw, i, alpha, bt, t)
        xs, b_re, b_im, sample_hg = _layer_sample(
            xs, ps, state_s5_re[i].reshape(bs, n_groups * n_state), state_s5_im[i].reshape(bs, n_groups * n_state),
            state_hgrn, sample_hg, mw, lw, i, alpha)
        for lst, val in zip(outs, (a_re.reshape(bt, n_groups, n_state), a_im.reshape(bt, n_groups, n_state), a_hg,
                                   b_re.reshape(bs, n_groups, n_state), b_im.reshape(bs, n_groups, n_state))):
            lst.append(val)

    return (xp.reshape(bt, t, d_model), xs.reshape(bs, 1, d_model), *[jnp.stack(o) for o in outs], sample_hg)
```
